```python
import math
import jax, jax.numpy as jnp
from jax import lax
import numpy as np

D_MODEL = 4096
BATCH = 2
SEQ = 4096
DEPTH = 2

N_MIXERS = 2
HG_EXPAND = 128
HG_HEADS = D_MODEL // HG_EXPAND
HG_DK = HG_EXPAND
HG_DV = D_MODEL // HG_HEADS
HG_CHUNK = 64
ATT_HEAD_DIM = 64
ATT_HEADS = D_MODEL // ATT_HEAD_DIM
ATT_KV_HEADS = 8
ATT_GROUP = ATT_HEADS // ATT_KV_HEADS
WINDOW = 128
ATT_BLOCK = 128
N_BUCKETS = 32
MAX_DISTANCE = 128
N_GROUPS = 4
EXPERTS_PER_GROUP = 8
N_EXPERTS = N_GROUPS * EXPERTS_PER_GROUP
TOP_K_FINE = 2
D_EXPERT = D_MODEL // 8
MOE_BLOCK = 128
ALPHA = (2 * DEPTH) ** 0.25
BETA = (8 * DEPTH) ** -0.25
LN_EPS = 1e-5
RMS_EPS = 1e-6
N_LAYERS_A = (DEPTH + 1) // 2
N_LAYERS_B = DEPTH // 2

kernel_name = "hybrid_hgrn2_swa_sink_hmoe_deepnorm"

F32 = jnp.float32


def layer_norm(x, g, b):
    xf = x.astype(F32)
    mu = jnp.mean(xf, axis=-1, keepdims=True)
    var = jnp.mean(jnp.square(xf - mu), axis=-1, keepdims=True)
    y = (xf - mu) * lax.rsqrt(var + LN_EPS)
    return (y * g.astype(F32) + b.astype(F32)).astype(x.dtype)


def hgrn2_mixer(h, w_in, head_gain, w_out, lb):
    B, S, D = h.shape
    C = HG_CHUNK
    n_chunks = S // C
    proj = h @ w_in
    q, f, v, g = jnp.split(proj, 4, axis=-1)
    q = jax.nn.silu(q.astype(F32))
    forget = lb + (1.0 - lb) * jax.nn.sigmoid(f.astype(F32))
    k = 1.0 - forget
    log_f = jnp.log(forget)

    def to_chunks(t, d):
        return t.reshape(B, n_chunks, C, HG_HEADS, d).transpose(1, 0, 3, 2, 4)

    qc = to_chunks(q, HG_DK)
    kc = to_chunks(k, HG_DK)
    lc = to_chunks(log_f, HG_DK)
    vc = to_chunks(v.astype(F32), HG_DV)
    causal = jnp.tril(jnp.ones((C, C), dtype=bool))[:, :, None]

    def step(state, inp):
        qs, ks, vs, ls = inp
        b = jnp.cumsum(ls, axis=2)
        o_inter = jnp.einsum('bhtd,bhde->bhte', qs * jnp.exp(b), state)
        diff = b[:, :, :, None, :] - b[:, :, None, :, :]
        decay = jnp.exp(jnp.where(causal, diff, -jnp.inf))
        attn = jnp.einsum('bhtd,bhtsd,bhsd->bhts', qs, decay, ks)
        o_intra = jnp.einsum('bhts,bhse->bhte', attn, vs)
        b_last = b[:, :, -1:, :]
        new_state = jnp.exp(b_last[:, :, 0, :])[..., None] * state + jnp.einsum(
            'bhsd,bhse->bhde', ks * jnp.exp(b_last - b), vs)
        return new_state, o_inter + o_intra

    state0 = jnp.zeros((B, HG_HEADS, HG_DK, HG_DV), F32)
    _, o = lax.scan(step, state0, (qc, kc, vc, lc))
    o = o.transpose(1, 0, 3, 2, 4).reshape(B, S, HG_HEADS, HG_DV)
    o = o * lax.rsqrt(jnp.mean(jnp.square(o), axis=-1, keepdims=True) + RMS_EPS)
    o = o * head_gain.astype(F32).reshape(HG_HEADS, HG_DV)
    o = o.reshape(B, S, D) * jax.nn.silu(g.astype(F32))
    return o.astype(h.dtype) @ w_out


def t5_bucket(dist):
    max_exact = N_BUCKETS // 2
    n = jnp.maximum(dist, 0)
    large = max_exact + (jnp.log(jnp.maximum(n, 1).astype(F32) / max_exact)
                         / math.log(MAX_DISTANCE / max_exact)
                         * (N_BUCKETS - max_exact)).astype(jnp.int32)
    large = jnp.minimum(large, N_BUCKETS - 1)
    return jnp.where(n < max_exact, n, large)


def swa_sink_mixer(h, w_in, sinks, w_out, rel_bias):
    B, S, D = h.shape
    BLK = ATT_BLOCK
    nb = S // BLK
    kv_w = ATT_KV_HEADS * ATT_HEAD_DIM
    proj = h @ w_in
    q = proj[..., :D].reshape(B, nb, BLK, ATT_KV_HEADS, ATT_GROUP, ATT_HEAD_DIM)
    k = proj[..., D:D + kv_w].reshape(B, nb, BLK, ATT_KV_HEADS, ATT_HEAD_DIM)
    v = proj[..., D + kv_w:].reshape(B, nb, BLK, ATT_KV_HEADS, ATT_HEAD_DIM)

    def with_prev(t):
        prev = jnp.concatenate([jnp.zeros_like(t[:, :1]), t[:, :-1]], axis=1)
        return jnp.concatenate([prev, t], axis=2)

    kk = with_prev(k).transpose(1, 0, 2, 3, 4)
    vv = with_prev(v).transpose(1, 0, 2, 3, 4)
    qb = q.transpose(1, 0, 2, 3, 4, 5)

    qi = jnp.arange(BLK)[:, None]
    sj = jnp.arange(2 * BLK)[None, :]
    dist = qi + BLK - sj
    bias = rel_bias.astype(F32)[t5_bucket(dist)]
    bias = bias.transpose(2, 0, 1).reshape(ATT_KV_HEADS, ATT_GROUP, BLK, 2 * BLK)
    band = (dist >= 0) & (dist < WINDOW)
    kpos = jnp.arange(nb)[:, None, None] * BLK - BLK + sj[None]
    valid = band[None] & (kpos >= 0)
    sink = sinks.astype(F32).reshape(ATT_KV_HEADS, ATT_GROUP)[None, :, :, None, None]
    scale = ATT_HEAD_DIM ** -0.5

    def block(args):
        qn, kn, vn, vmask = args
        logits = jnp.einsum('bqkgd,bskd->bkgqs', qn.astype(F32), kn.astype(F32)) * scale + bias
        logits = jnp.where(vmask[None, None, None], logits, -jnp.inf)
        m = jnp.maximum(jnp.max(logits, axis=-1, keepdims=True), sink)
        p = jnp.exp(logits - m)
        probs = p / (jnp.sum(p, axis=-1, keepdims=True) + jnp.exp(sink - m))
        return jnp.einsum('bkgqs,bskd->bqkgd', probs, vn.astype(F32))

    o = lax.map(block, (qb, kk, vv, valid))
    o = o.transpose(1, 0, 2, 3, 4, 5).reshape(B, S, D).astype(h.dtype)
    return o @ w_out


def hier_moe(h, w_rg, b_rg, w_re, b_re, w_gu, w_dn):
    B, S, D = h.shape
    T = B * S
    xt = h.reshape(T, D)
    g_logits = (xt @ w_rg).astype(F32) + b_rg.astype(F32)
    g_prob = jax.nn.softmax(g_logits, axis=-1)
    g_sel = jnp.argmax(g_logits, axis=-1)
    p_group = jnp.take_along_axis(g_prob, g_sel[:, None], axis=-1)
    e_logits = ((xt @ w_re).astype(F32) + b_re.astype(F32)).reshape(T, N_GROUPS, EXPERTS_PER_GROUP)
    e_logits = jnp.take_along_axis(e_logits, g_sel[:, None, None], axis=1)[:, 0]
    e_prob = jax.nn.softmax(e_logits, axis=-1)
    top_p, top_i = lax.top_k(e_prob, TOP_K_FINE)
    gate = p_group * top_p / jnp.sum(top_p, axis=-1, keepdims=True)

    eid = (g_sel[:, None] * EXPERTS_PER_GROUP + top_i).reshape(-1).astype(jnp.int32)
    tok = jnp.repeat(jnp.arange(T, dtype=jnp.int32), TOP_K_FINE)
    wgt = gate.reshape(-1)
    A = T * TOP_K_FINE
    order = jnp.argsort(eid)
    eid_s, tok_s, wgt_s = eid[order], tok[order], wgt[order]
    counts = jnp.bincount(eid, length=N_EXPERTS).astype(jnp.int32)
    padded = (counts + MOE_BLOCK - 1) // MOE_BLOCK * MOE_BLOCK
    pad_end = jnp.cumsum(padded)
    pad_start = pad_end - padded
    start = jnp.cumsum(counts) - counts
    dest = pad_start[eid_s] + jnp.arange(A, dtype=jnp.int32) - start[eid_s]
    P = (A + MOE_BLOCK - 1) // MOE_BLOCK * MOE_BLOCK + N_EXPERTS * MOE_BLOCK
    n_blk = P // MOE_BLOCK
    tok_pad = jnp.zeros((P,), jnp.int32).at[dest].set(tok_s)
    wgt_pad = jnp.zeros((P,), F32).at[dest].set(wgt_s)
    blk_expert = jnp.minimum(
        jnp.searchsorted(pad_end, jnp.arange(n_blk, dtype=jnp.int32) * MOE_BLOCK, side='right'),
        N_EXPERTS - 1)
    x_blk = xt[tok_pad].reshape(n_blk, MOE_BLOCK, D)

    def expert_block(args):
        xb, e = args
        a, b = jnp.split(xb @ w_gu[e], 2, axis=-1)
        return (jax.nn.silu(a) * b) @ w_dn[e]

    y = lax.map(expert_block, (x_blk, blk_expert)).reshape(P, D)
    y = y.astype(F32) * wgt_pad[:, None]
    out = jax.ops.segment_sum(y, tok_pad, num_segments=T)
    return out.reshape(B, S, D).astype(h.dtype)


def setup_inputs(seed: int = 0) -> dict:
    key = jax.random.key(seed)
    ks = jax.random.split(key, 24)
    D = D_MODEL
    n = lambda k, shape, s: jax.random.normal(k, shape, F32) * s
    att_in = D + 2 * ATT_KV_HEADS * ATT_HEAD_DIM
    return {
        "x": n(ks[0], (BATCH, SEQ, D), 1.0),
        "c": n(ks[1], (BATCH, D), 1.0),
        "w_ada": n(ks[2], (DEPTH, D, 6 * D), 0.5 * D ** -0.5),
        "b_ada": n(ks[3], (DEPTH, 6 * D), 0.01),
        "ln_g": 1.0 + n(ks[4], (DEPTH, 2, D), 0.02),
        "ln_b": n(ks[5], (DEPTH, 2, D), 0.02),
        "w_in_a": n(ks[6], (N_LAYERS_A, D, 4 * D), D ** -0.5),
        "lb_logits": n(ks[7], (DEPTH + 1, D), 1.0),
        "head_gain_a": 1.0 + n(ks[8], (N_LAYERS_A, D), 0.02),
        "w_out_a": n(ks[9], (N_LAYERS_A, D, D), BETA * D ** -0.5),
        "w_in_b": n(ks[10], (N_LAYERS_B, D, att_in), D ** -0.5),
        "attn_sinks": n(ks[11], (N_LAYERS_B, ATT_HEADS), 1.0),
        "w_out_b": n(ks[12], (N_LAYERS_B, D, D), BETA * D ** -0.5),
        "rel_bias": n(ks[13], (N_BUCKETS, ATT_HEADS), 0.5),
        "w_router_group": n(ks[14], (DEPTH, D, N_GROUPS), D ** -0.5),
        "b_router_group": n(ks[15], (DEPTH, N_GROUPS), 0.01),
        "w_router_expert": n(ks[16], (DEPTH, D, N_EXPERTS), D ** -0.5),
        "b_router_expert": n(ks[17], (DEPTH, N_EXPERTS), 0.01),
        "w_gate_up": n(ks[18], (DEPTH, N_EXPERTS, D, 2 * D_EXPERT), D ** -0.5),
        "w_down": n(ks[19], (DEPTH, N_EXPERTS, D_EXPERT, D), BETA * D_EXPERT ** -0.5),
    }


def reference(x, c, w_ada, b_ada, ln_g, ln_b, w_in_a, lb_logits, head_gain_a, w_out_a,
              w_in_b, attn_sinks, w_out_b, rel_bias, w_router_group, b_router_group,
              w_router_expert, b_router_expert, w_gate_up, w_down):
    lb_all = jnp.cumsum(jax.nn.softmax(lb_logits.astype(F32), axis=0), axis=0)
    c_act = jax.nn.silu(c)
    for layer in range(DEPTH):
        mod = c_act @ w_ada[layer] + b_ada[layer]
        sh1, sc1, g1, sh2, sc2, g2 = jnp.split(mod[:, None, :], 6, axis=-1)
        hm = x * (1.0 + sc1) + sh1
        j = layer // N_MIXERS
        if layer % N_MIXERS == 0:
            y = hgrn2_mixer(hm, w_in_a[j], head_gain_a[j], w_out_a[j], lb_all[layer])
        else:
            y = swa_sink_mixer(hm, w_in_b[j], attn_sinks[j], w_out_b[j], rel_bias)
        x = layer_norm(ALPHA * x + (1.0 + g1) * y, ln_g[layer, 0], ln_b[layer, 0])
        hf = x * (1.0 + sc2) + sh2
        y = hier_moe(hf, w_router_group[layer], b_router_group[layer], w_router_expert[layer],
                     b_router_expert[layer], w_gate_up[layer], w_down[layer])
        x = layer_norm(ALPHA * x + (1.0 + g2) * y, ln_g[layer, 1], ln_b[layer, 1])
    return x
```

```python
import functools
import math

import jax
import jax.numpy as jnp
from jax import lax
from jax.experimental import pallas as pl
from jax.experimental.pallas import tpu as pltpu

F32 = jnp.float32
BF16 = jnp.bfloat16
I32 = jnp.int32

LANES = 128
SUBLANES = 8
V7X_VMEM_LIMIT_BYTES = 56 * 1024 * 1024

HG_HEAD_DIM = 128
ATT_HEAD_DIM = 64
ATT_BLOCK = 128
WINDOW = 128
N_BUCKETS = 32
MAX_DISTANCE = 128
LN_EPS = 1e-5
RMS_EPS = 1e-6
NEG_INF = float("-inf")


def _cparams(n_axes):
    return pltpu.CompilerParams(
        dimension_semantics=("arbitrary",) * n_axes,
        vmem_limit_bytes=V7X_VMEM_LIMIT_BYTES,
    )


def _sigmoid(x):
    return 1.0 / (1.0 + jnp.exp(-x))


def _silu(x):
    return x * _sigmoid(x)


def _dot_nt(a, b):
    return lax.dot_general(a, b, (((1,), (1,)), ((), ())), preferred_element_type=F32)


def _dot_tn(a, b):
    return lax.dot_general(a, b, (((0,), (0,)), ((), ())), preferred_element_type=F32)


def _dot(a, b):
    return jnp.dot(a, b, preferred_element_type=F32)


def _split_bf16(x, parts):
    out = []
    r = x
    for _ in range(parts):
        h = r.astype(BF16)
        out.append(h)
        r = r - h.astype(F32)
    return out


def _ada_kernel(c_ref, w_ref, b_ref, o_ref):
    ca = _silu(c_ref[...]).astype(BF16)
    o_ref[...] = _dot(ca, w_ref[...].astype(BF16)) + b_ref[...]


def _ada_modulation(c, w_ada, b_ada):
    nl, d, n6 = w_ada.shape
    bsz = c.shape[0]
    rows = -(-bsz // SUBLANES) * SUBLANES
    c8 = jnp.zeros((rows, d), F32).at[:bsz].set(c.astype(F32))
    tn = 512
    out = pl.pallas_call(
        _ada_kernel,
        grid=(nl, n6 // tn),
        in_specs=[
            pl.BlockSpec((rows, d), lambda l, j: (0, 0)),
            pl.BlockSpec((None, d, tn), lambda l, j: (l, 0, j)),
            pl.BlockSpec((None, 1, tn), lambda l, j: (l, 0, j)),
        ],
        out_specs=pl.BlockSpec((None, rows, tn), lambda l, j: (l, 0, j)),
        out_shape=jax.ShapeDtypeStruct((nl, rows, n6), F32),
        compiler_params=_cparams(2),
        name="ada_modulation",
    )(c8, w_ada, b_ada.reshape(nl, 1, n6))
    return out[:, :bsz].reshape(nl * bsz * 6, 1, d)


def _modulate_kernel(x_ref, sc_ref, sh_ref, o_ref):
    o_ref[...] = (x_ref[...] * (1.0 + sc_ref[...]) + sh_ref[...]).astype(o_ref.dtype)


def _modulate(x2, mod, sc_row, sh_row, bsz, seq):
    t, d = x2.shape
    bs = min(512, seq)
    nbs = seq // bs
    return pl.pallas_call(
        _modulate_kernel,
        grid=(t // bs,),
        in_specs=[
            pl.BlockSpec((bs, d), lambda i: (i, 0)),
            pl.BlockSpec((None, 1, d), lambda i: (sc_row(i // nbs), 0, 0)),
            pl.BlockSpec((None, 1, d), lambda i: (sh_row(i // nbs), 0, 0)),
        ],
        out_specs=pl.BlockSpec((bs, d), lambda i: (i, 0)),
        out_shape=jax.ShapeDtypeStruct((t, d), BF16),
        compiler_params=_cparams(1),
        name="modulate",
    )(x2, mod, mod)


def _matmul_kernel(x_ref, w_ref, o_ref, wbf_ref):
    @pl.when(pl.program_id(1) == 0)
    def _():
        wbf_ref[...] = w_ref[...].astype(BF16)

    o_ref[...] = _dot(x_ref[...], wbf_ref[...]).astype(o_ref.dtype)


def _matmul(x, w3, layer, out_dtype):
    m, k = x.shape
    n = w3.shape[2]
    bm = min(1024, m)
    bn = 512
    return pl.pallas_call(
        _matmul_kernel,
        grid=(n // bn, m // bm),
        in_specs=[
            pl.BlockSpec((bm, k), lambda j, i: (i, 0)),
            pl.BlockSpec((None, k, bn), lambda j, i: (layer, 0, j)),
        ],
        out_specs=pl.BlockSpec((bm, bn), lambda j, i: (i, j)),
        out_shape=jax.ShapeDtypeStruct((m, n), out_dtype),
        scratch_shapes=[pltpu.VMEM((k, bn), BF16)],
        compiler_params=_cparams(2),
        name="dense_projection",
    )(x, w3)


def _hgrn_kernel(q_ref, f_ref, v_ref, g_ref, lbl_ref, gain_ref, o_ref,
                 st_ref, b_ref, tri_ref, mask_ref, *, layer, chunk):
    c = chunk
    nlev = int(math.log2(c))
    first = (pl.program_id(0) == 0) & (pl.program_id(1) == 0) & (pl.program_id(2) == 0)

    @pl.when(first)
    def _():
        row = lax.broadcasted_iota(I32, (c, c), 0)
        col = lax.broadcasted_iota(I32, (c, c), 1)
        tri_ref[...] = jnp.where(row >= col, 1.0, 0.0).astype(BF16)
        x = row ^ col
        mask_ref[0] = jnp.where(x == 0, 1.0, 0.0).astype(F32)
        for lv in range(1, nlev):
            mask_ref[lv] = jnp.where(x < (1 << lv), 1.0, 0.0).astype(F32)

    @pl.when(pl.program_id(2) == 0)
    def _():
        st_ref[...] = jnp.zeros_like(st_ref)

    lbl = lbl_ref[...]
    nrow = lbl.shape[0]
    rows = [lbl[i:i + 1, :] for i in range(nrow)]
    mx = functools.reduce(jnp.maximum, rows)
    es = [jnp.exp(r - mx) for r in rows]
    lb = functools.reduce(lambda a, b: a + b, es[:layer + 1]) / functools.reduce(lambda a, b: a + b, es)

    q = _silu(q_ref[...].astype(F32))
    forget = lb + (1.0 - lb) * _sigmoid(f_ref[...].astype(F32))
    k = 1.0 - forget
    logf = jnp.log(forget)
    v = v_ref[...]

    hml = jnp.concatenate(_split_bf16(logf, 3), axis=1)
    bb = _dot(tri_ref[...], hml)
    b = bb[:, 0:LANES] + bb[:, LANES:2 * LANES] + bb[:, 2 * LANES:3 * LANES]
    b_ref[...] = b

    def bcast_row(r):
        return jnp.broadcast_to(b_ref[pl.ds(r, 1), :], (SUBLANES, HG_HEAD_DIM))

    rowi = lax.broadcasted_iota(I32, (c, HG_HEAD_DIM), 0)
    sub = lax.broadcasted_iota(I32, (SUBLANES, HG_HEAD_DIM), 0)
    ntile = c // SUBLANES

    qb = q.astype(BF16)
    kb = k.astype(BF16)
    attn = _dot_nt(qb, kb) * mask_ref[0]
    for lv in range(nlev):
        m = 1 << lv
        isq = (rowi & m) != 0
        if m == 1:
            e = jnp.where(isq, forget, 1.0)
        else:
            if m >= SUBLANES:
                n = 2 * m
                tiles = [bcast_row((j * SUBLANES // n) * n + m - 1) for j in range(ntile)]
            elif m == 4:
                tiles = [bcast_row(j * SUBLANES + 3) for j in range(ntile)]
            else:
                tiles = [jnp.where(sub < 4, bcast_row(j * SUBLANES + 1), bcast_row(j * SUBLANES + 5))
                         for j in range(ntile)]
            ref_pt = jnp.concatenate(tiles, axis=0)
            e = jnp.exp(jnp.where(isq, b - ref_pt, ref_pt - b))
        qt = jnp.where(isq, q * e, 0.0).astype(BF16)
        kt = jnp.where(isq, 0.0, k * e).astype(BF16)
        a = _dot_nt(qt, kt)
        if lv + 1 < nlev:
            a = a * mask_ref[lv + 1]
        attn = attn + a

    o = _dot(attn.astype(BF16), v)

    st = st_ref[...]
    o = o + _dot_nt((q * jnp.exp(b)).astype(BF16), st.astype(BF16))
    b_last = bcast_row(c - 1)
    kd = (k * jnp.exp(jnp.concatenate([b_last] * ntile, axis=0) - b)).astype(BF16)
    st_ref[...] = st * jnp.exp(b_last[0:1, :]) + _dot_tn(v, kd)

    o = o * lax.rsqrt(jnp.mean(o * o, axis=-1, keepdims=True) + RMS_EPS)
    o = o * gain_ref[...] * _silu(g_ref[...].astype(F32))
    o_ref[...] = o.astype(o_ref.dtype)


def _hgrn_mixer(proj, lb_logits, head_gain, layer, j, bsz, seq):
    t, d4 = proj.shape
    d = d4 // 4
    nh = d // HG_HEAD_DIM
    chunk = 256 if seq % 256 == 0 else 128
    nc = seq // chunk
    nlev = int(math.log2(chunk))

    def col(part):
        return lambda b, h, c: (b * nc + c, part * nh + h)

    return pl.pallas_call(
        functools.partial(_hgrn_kernel, layer=layer, chunk=chunk),
        grid=(bsz, nh, nc),
        in_specs=[
            pl.BlockSpec((chunk, HG_HEAD_DIM), col(0)),
            pl.BlockSpec((chunk, HG_HEAD_DIM), col(1)),
            pl.BlockSpec((chunk, HG_HEAD_DIM), col(2)),
            pl.BlockSpec((chunk, HG_HEAD_DIM), col(3)),
            pl.BlockSpec((lb_logits.shape[0], HG_HEAD_DIM), lambda b, h, c: (0, h)),
            pl.BlockSpec((None, 1, HG_HEAD_DIM), lambda b, h, c: (j, 0, h)),
        ],
        out_specs=pl.BlockSpec((chunk, HG_HEAD_DIM), lambda b, h, c: (b * nc + c, h)),
        out_shape=jax.ShapeDtypeStruct((t, d), BF16),
        scratch_shapes=[
            pltpu.VMEM((HG_HEAD_DIM, HG_HEAD_DIM), F32),
            pltpu.VMEM((chunk, HG_HEAD_DIM), F32),
            pltpu.VMEM((chunk, chunk), BF16),
            pltpu.VMEM((nlev, chunk, chunk), F32),
        ],
        compiler_params=_cparams(3),
        name="hgrn2_mixer",
    )(proj, proj, proj, proj, lb_logits.astype(F32), head_gain.astype(F32).reshape(head_gain.shape[0], 1, d))


def _t5_bucket(dist):
    max_exact = N_BUCKETS // 2
    n = jnp.maximum(dist, 0)
    large = max_exact + (jnp.log(jnp.maximum(n, 1).astype(F32) / max_exact)
                         / math.log(MAX_DISTANCE / max_exact)
                         * (N_BUCKETS - max_exact)).astype(I32)
    large = jnp.minimum(large, N_BUCKETS - 1)
    return jnp.where(n < max_exact, n, large)


def _attn_kernel(rb_ref, sink_ref, q_ref, kp_ref, kc_ref, vp_ref, vc_ref, bucket_ref, o_ref,
                 bias_ref, *, group, layer_j):
    blk = ATT_BLOCK
    hd = ATT_HEAD_DIM
    pr = pl.program_id(1)
    i = pl.program_id(2)
    heads_per_step = 2 * group

    @pl.when(i == 0)
    def _():
        qi = lax.broadcasted_iota(I32, (blk, 2 * blk), 0)
        sj = lax.broadcasted_iota(I32, (blk, 2 * blk), 1)
        dist = qi + blk - sj
        band = (dist >= 0) & (dist < WINDOW)
        bucket = bucket_ref[...]

        def per_head(hh, carry):
            h = pr * heads_per_step + hh
            tbl = jnp.zeros((blk, 2 * blk), F32)
            for bk in range(N_BUCKETS):
                tbl = jnp.where(bucket == bk, rb_ref[bk, h], tbl)
            bias_ref[hh] = jnp.where(band, tbl, NEG_INF)
            return carry

        lax.fori_loop(0, heads_per_step, per_head, 0)

    lane = lax.broadcasted_iota(I32, (2 * blk, LANES), 1)
    colj = lax.broadcasted_iota(I32, (1, 2 * blk), 1)
    pen = jnp.where((colj < blk) & (i == 0), NEG_INF, 0.0).astype(F32)

    kk = jnp.concatenate([kp_ref[...], kc_ref[...]], axis=0).astype(F32)
    vv = jnp.concatenate([vp_ref[...], vc_ref[...]], axis=0).astype(F32)
    q_all = q_ref[...]

    for c in range(2):
        if c == 0:
            klo = jnp.where(lane < hd, kk, 0.0)
            khi = pltpu.roll(klo, hd, axis=1)
            vlo = jnp.where(lane < hd, vv, 0.0)
            vhi = pltpu.roll(vlo, hd, axis=1)
        else:
            khi = jnp.where(lane >= hd, kk, 0.0)
            klo = pltpu.roll(khi, hd, axis=1)
            vhi = jnp.where(lane >= hd, vv, 0.0)
            vlo = pltpu.roll(vhi, hd, axis=1)
        kbd = jnp.concatenate([klo, khi], axis=0).astype(BF16)
        vbd = jnp.concatenate([vlo, vhi], axis=0).astype(BF16)
        for p in range(group // 2):
            tile = c * (group // 2) + p
            qp = q_all[:, tile * LANES:(tile + 1) * LANES]
            lg = _dot_nt(qp, kbd) * (hd ** -0.5)
            probs = []
            for hh in range(2):
                hl = c * group + 2 * p + hh
                sk = sink_ref[layer_j, pr * heads_per_step + hl]
                l = lg[:, hh * 2 * blk:(hh + 1) * 2 * blk] + bias_ref[hl] + pen
                mx = jnp.maximum(jnp.max(l, axis=-1, keepdims=True), sk)
                pe = jnp.exp(l - mx)
                den = jnp.sum(pe, axis=-1, keepdims=True) + jnp.exp(sk - mx)
                probs.append(pe / den)
            p2 = jnp.concatenate(probs, axis=1).astype(BF16)
            o_ref[:, tile * LANES:(tile + 1) * LANES] = _dot(p2, vbd).astype(o_ref.dtype)


def _attn_mixer(proj, sinks, rel_bias, layer_j, bsz, seq, d):
    t, att_in = proj.shape
    kvw = (att_in - d) // 2
    n_heads = d // ATT_HEAD_DIM
    kvh = kvw // ATT_HEAD_DIM
    group = n_heads // kvh
    assert kvh % 2 == 0 and group % 2 == 0
    blk = ATT_BLOCK
    nb = seq // blk
    npair = kvh // 2
    qw = 2 * group * ATT_HEAD_DIM
    k0 = d // LANES
    v0 = (d + kvw) // LANES

    qi = jnp.arange(blk)[:, None]
    sj = jnp.arange(2 * blk)[None, :]
    bucket = _t5_bucket(qi + blk - sj).astype(I32)

    def prev(i):
        return jnp.maximum(i - 1, 0)

    grid_spec = pltpu.PrefetchScalarGridSpec(
        num_scalar_prefetch=2,
        grid=(bsz, npair, nb),
        in_specs=[
            pl.BlockSpec((blk, qw), lambda b, p, i, *_: (b * nb + i, p)),
            pl.BlockSpec((blk, LANES), lambda b, p, i, *_: (b * nb + prev(i), k0 + p)),
            pl.BlockSpec((blk, LANES), lambda b, p, i, *_: (b * nb + i, k0 + p)),
            pl.BlockSpec((blk, LANES), lambda b, p, i, *_: (b * nb + prev(i), v0 + p)),
            pl.BlockSpec((blk, LANES), lambda b, p, i, *_: (b * nb + i, v0 + p)),
            pl.BlockSpec((blk, 2 * blk), lambda b, p, i, *_: (0, 0)),
        ],
        out_specs=pl.BlockSpec((blk, qw), lambda b, p, i, *_: (b * nb + i, p)),
        scratch_shapes=[pltpu.VMEM((2 * group, blk, 2 * blk), F32)],
    )
    return pl.pallas_call(
        functools.partial(_attn_kernel, group=group, layer_j=layer_j),
        grid_spec=grid_spec,
        out_shape=jax.ShapeDtypeStruct((t, d), BF16),
        compiler_params=_cparams(3),
        name="swa_sink_mixer",
    )(rel_bias.astype(F32), sinks.astype(F32), proj, proj, proj, proj, proj, bucket)


def _layer_norm_rows(z, g, b):
    mu = jnp.mean(z, axis=-1, keepdims=True)
    zc = z - mu
    var = jnp.mean(zc * zc, axis=-1, keepdims=True)
    return zc * lax.rsqrt(var + LN_EPS) * g + b


def _ln_router_kernel(x_ref, y_ref, gate_ref, sc_ref, sh_ref, lng_ref, lnb_ref, wr_ref, br_ref,
                      xo_ref, hf_ref, meta_ref, cnt_ref, carry_ref, *, alpha, n_groups, n_experts):
    i = pl.program_id(0)
    bm = x_ref.shape[0]
    epg = n_experts // n_groups

    @pl.when(i == 0)
    def _():
        carry_ref[...] = jnp.zeros_like(carry_ref)

    z = alpha * x_ref[...] + (1.0 + gate_ref[...]) * y_ref[...]
    xn = _layer_norm_rows(z, lng_ref[...], lnb_ref[...])
    xo_ref[...] = xn
    hf = xn * (1.0 + sc_ref[...]) + sh_ref[...]
    hf_ref[...] = hf

    xh, xl = _split_bf16(hf, 2)
    wh, wl = _split_bf16(wr_ref[...], 2)
    lg = _dot(xh, wh) + _dot(xh, wl) + _dot(xl, wh) + br_ref[...]

    lane = lax.broadcasted_iota(I32, (bm, LANES), 1)
    lanef = lane.astype(F32)
    big = float(LANES)

    gl = jnp.where(lane < n_groups, lg, NEG_INF)
    gmax = jnp.max(gl, axis=-1, keepdims=True)
    gsel = jnp.min(jnp.where(gl == gmax, lanef, big), axis=-1, keepdims=True)
    p_group = 1.0 / jnp.sum(jnp.exp(gl - gmax), axis=-1, keepdims=True)

    lo = n_groups + gsel * epg
    el = jnp.where((lanef >= lo) & (lanef < lo + epg), lg, NEG_INF)
    m1 = jnp.max(el, axis=-1, keepdims=True)
    i1 = jnp.min(jnp.where(el == m1, lanef, big), axis=-1, keepdims=True)
    el2 = jnp.where(lanef == i1, NEG_INF, el)
    m2 = jnp.max(el2, axis=-1, keepdims=True)
    i2 = jnp.min(jnp.where(el2 == m2, lanef, big), axis=-1, keepdims=True)
    e21 = jnp.exp(m2 - m1)
    g0 = p_group / (1.0 + e21)
    g1 = g0 * e21

    oh0 = lanef == i1
    oh1 = lanef == i2
    cnt = jnp.where(oh0 | oh1, 1.0, 0.0)
    row = lax.broadcasted_iota(I32, (bm, bm), 0)
    col = lax.broadcasted_iota(I32, (bm, bm), 1)
    stril = jnp.where(row > col, 1.0, 0.0).astype(BF16)
    before = _dot(stril, cnt.astype(BF16)) + carry_ref[...]
    r0 = jnp.sum(jnp.where(oh0, before, 0.0), axis=-1, keepdims=True)
    r1 = jnp.sum(jnp.where(oh1, before, 0.0), axis=-1, keepdims=True)
    carry_ref[...] = carry_ref[...] + jnp.sum(cnt, axis=0, keepdims=True)
    cnt_ref[...] = carry_ref[...]

    meta = jnp.where(lane == 0, i1 - n_groups, 0.0)
    meta = jnp.where(lane == 1, i2 - n_groups, meta)
    meta = jnp.where(lane == 2, g0, meta)
    meta = jnp.where(lane == 3, g1, meta)
    meta = jnp.where(lane == 4, r0, meta)
    meta = jnp.where(lane == 5, r1, meta)
    meta_ref[...] = meta


def _ln_router(x2, y2, mod, row_of, ln_g, ln_b, w_router, b_router, alpha, n_groups, n_experts, seq):
    t, d = x2.shape
    bm = 128
    nbs = seq // bm

    def mrow(which):
        return lambda i: (row_of(i // nbs, which), 0, 0)

    return pl.pallas_call(
        functools.partial(_ln_router_kernel, alpha=alpha, n_groups=n_groups, n_experts=n_experts),
        grid=(t // bm,),
        in_specs=[
            pl.BlockSpec((bm, d), lambda i: (i, 0)),
            pl.BlockSpec((bm, d), lambda i: (i, 0)),
            pl.BlockSpec((None, 1, d), mrow(2)),
            pl.BlockSpec((None, 1, d), mrow(4)),
            pl.BlockSpec((None, 1, d), mrow(3)),
            pl.BlockSpec((1, d), lambda i: (0, 0)),
            pl.BlockSpec((1, d), lambda i: (0, 0)),
            pl.BlockSpec((d, LANES), lambda i: (0, 0)),
            pl.BlockSpec((1, LANES), lambda i: (0, 0)),
        ],
        out_specs=[
            pl.BlockSpec((bm, d), lambda i: (i, 0)),
            pl.BlockSpec((bm, d), lambda i: (i, 0)),
            pl.BlockSpec((bm, LANES), lambda i: (i, 0)),
            pl.BlockSpec((1, LANES), lambda i: (0, 0)),
        ],
        out_shape=[
            jax.ShapeDtypeStruct((t, d), F32),
            jax.ShapeDtypeStruct((t, d), F32),
            jax.ShapeDtypeStruct((t, LANES), F32),
            jax.ShapeDtypeStruct((1, LANES), F32),
        ],
        scratch_shapes=[pltpu.VMEM((1, LANES), F32)],
        compiler_params=_cparams(1),
        name="ln_router",
    )(x2, y2, mod, mod, mod, ln_g, ln_b, w_router, b_router)


def _cast_rows(src_ref, dst_ref, rows=128):
    def body(r, carry):
        sl = pl.ds(pl.multiple_of(r * rows, rows), rows)
        dst_ref[sl, :] = src_ref[sl, :].astype(dst_ref.dtype)
        return carry
    lax.fori_loop(0, src_ref.shape[0] // rows, body, 0)


def _moe_kernel(blk_e_ref, nxt_e_ref, first_ref, nused_ref,
                tokc_ref, tokn_ref, hf_hbm, wgu_hbm, wdn_hbm, y_ref,
                xbuf, wgu_st, wdn_st, wgu_bf, wdn_bf, gsem, wsem, *, layer, d_expert):
    i = pl.program_id(0)
    bm = xbuf.shape[1]
    nused = nused_ref[0]
    slot = i % 2

    def row_copy(tok, r, s):
        return pltpu.make_async_copy(hf_hbm.at[pl.ds(tok, 1), :], xbuf.at[s, pl.ds(r, 1), :], gsem.at[s])

    def issue_rows(tok_ref, s):
        def body(r, carry):
            row_copy(tok_ref[0, r], r, s).start()
            return carry
        lax.fori_loop(0, bm, body, 0)

    def wait_rows(s):
        def body(r, carry):
            row_copy(0, r, s).wait()
            return carry
        lax.fori_loop(0, bm, body, 0)

    def weight_copies(e):
        return (pltpu.make_async_copy(wgu_hbm.at[layer, e], wgu_st, wsem.at[0]),
                pltpu.make_async_copy(wdn_hbm.at[layer, e], wdn_st, wsem.at[1]))

    @pl.when(i == 0)
    def _():
        issue_rows(tokc_ref, 0)
        for cp in weight_copies(blk_e_ref[0]):
            cp.start()

    @pl.when((i < nused) & (first_ref[i] == 1))
    def _():
        for cp in weight_copies(blk_e_ref[i]):
            cp.wait()
        _cast_rows(wgu_st, wgu_bf)
        _cast_rows(wdn_st, wdn_bf)

        @pl.when(nxt_e_ref[i] >= 0)
        def _():
            for cp in weight_copies(nxt_e_ref[i]):
                cp.start()

    @pl.when(i + 1 < nused)
    def _():
        issue_rows(tokn_ref, 1 - slot)

    @pl.when(i < nused)
    def _():
        wait_rows(slot)
        xb = xbuf[slot].astype(BF16)
        a = _dot(xb, wgu_bf[...])
        h = _silu(a[:, :d_expert]) * a[:, d_expert:]
        y_ref[...] = _dot(h.astype(BF16), wdn_bf[...])

    @pl.when(i >= nused)
    def _():
        y_ref[...] = jnp.zeros_like(y_ref)


def _moe_experts(hf, tok_pad, blk_e, nxt_e, first, nused, w_gate_up, w_down, layer, bm):
    t, d = hf.shape
    nb = blk_e.shape[0]
    d_expert = w_down.shape[2]
    tok3 = tok_pad.reshape(nb, 1, bm)
    grid_spec = pltpu.PrefetchScalarGridSpec(
        num_scalar_prefetch=4,
        grid=(nb,),
        in_specs=[
            pl.BlockSpec((None, 1, bm), lambda i, *_: (i, 0, 0), memory_space=pltpu.SMEM),
            pl.BlockSpec((None, 1, bm), lambda i, *_: (jnp.minimum(i + 1, nb - 1), 0, 0), memory_space=pltpu.SMEM),
            pl.BlockSpec(memory_space=pl.ANY),
            pl.BlockSpec(memory_space=pl.ANY),
            pl.BlockSpec(memory_space=pl.ANY),
        ],
        out_specs=pl.BlockSpec((bm, d), lambda i, *_: (i, 0)),
        scratch_shapes=[
            pltpu.VMEM((2, bm, d), F32),
            pltpu.VMEM((d, 2 * d_expert), F32),
            pltpu.VMEM((d_expert, d), F32),
            pltpu.VMEM((d, 2 * d_expert), BF16),
            pltpu.VMEM((d_expert, d), BF16),
            pltpu.SemaphoreType.DMA((2,)),
            pltpu.SemaphoreType.DMA((2,)),
        ],
    )
    return pl.pallas_call(
        functools.partial(_moe_kernel, layer=layer, d_expert=d_expert),
        grid_spec=grid_spec,
        out_shape=jax.ShapeDtypeStruct((nb * bm, d), F32),
        compiler_params=_cparams(1),
        name="moe_experts",
    )(blk_e, nxt_e, first, nused, tok3, tok3, hf, w_gate_up, w_down)


def _moe_plan(meta, counts_row, n_groups, n_experts, bm):
    t = meta.shape[0]
    eid = meta[:, 0:2].astype(I32)
    rank = meta[:, 4:6].astype(I32)
    counts = counts_row[0, n_groups:n_groups + n_experts].astype(I32)
    padded = (counts + bm - 1) // bm * bm
    pad_end = jnp.cumsum(padded)
    pad_start = pad_end - padded
    dest = pad_start[eid] + rank
    nb = (2 * t) // bm + n_experts
    nused = pad_end[-1] // bm
    ids = jnp.arange(nb, dtype=I32)
    raw_e = jnp.minimum(jnp.searchsorted(pad_end, ids * bm, side="right"), n_experts - 1).astype(I32)
    used = ids < nused
    blk_e = jnp.where(used, raw_e, raw_e[nused - 1])
    prev_e = jnp.concatenate([jnp.full((1,), -1, I32), blk_e[:-1]])
    first = (used & (blk_e != prev_e)).astype(I32)
    key = jnp.where(used, blk_e, n_experts)
    nxt_idx = jnp.searchsorted(key, blk_e, side="right")
    nxt_e = jnp.where(nxt_idx < nused, key[jnp.minimum(nxt_idx, nb - 1)], -1).astype(I32)
    tok = jnp.repeat(jnp.arange(t, dtype=I32), 2)
    tok_pad = jnp.zeros((nb * bm,), I32).at[dest.reshape(-1)].set(tok)
    return dest, tok_pad, blk_e, nxt_e, first, nused.reshape(1).astype(I32)


def _ln_combine_kernel(*refs, alpha, with_next):
    if with_next:
        (dc_ref, dn_ref, x_ref, meta_ref, gate_ref, lng_ref, lnb_ref, sc_ref, sh_ref, y_hbm,
         xo_ref, hm_ref, ybuf, sem) = refs
    else:
        (dc_ref, dn_ref, x_ref, meta_ref, gate_ref, lng_ref, lnb_ref, y_hbm,
         xo_ref, ybuf, sem) = refs
    i = pl.program_id(0)
    n = pl.num_programs(0)
    bm = x_ref.shape[0]
    slot = i % 2

    def row_copy(src, r, s):
        return pltpu.make_async_copy(y_hbm.at[pl.ds(src, 1), :], ybuf.at[s, pl.ds(r, 1), :], sem.at[s])

    def issue_rows(d_ref, s):
        def body(r, carry):
            row_copy(d_ref[0, r], r, s).start()
            return carry
        lax.fori_loop(0, 2 * bm, body, 0)

    def wait_rows(s):
        def body(r, carry):
            row_copy(0, r, s).wait()
            return carry
        lax.fori_loop(0, 2 * bm, body, 0)

    @pl.when(i == 0)
    def _():
        issue_rows(dc_ref, 0)

    @pl.when(i + 1 < n)
    def _():
        issue_rows(dn_ref, 1 - slot)

    wait_rows(slot)
    meta = meta_ref[...]
    y = ybuf[slot, 0:bm, :] * meta[:, 2:3] + ybuf[slot, bm:2 * bm, :] * meta[:, 3:4]
    z = alpha * x_ref[...] + (1.0 + gate_ref[...]) * y
    xn = _layer_norm_rows(z, lng_ref[...], lnb_ref[...])
    xo_ref[...] = xn
    if with_next:
        hm_ref[...] = (xn * (1.0 + sc_ref[...]) + sh_ref[...]).astype(hm_ref.dtype)


def _ln_combine(x2, meta, dest, ysort, mod, row_of, next_row_of, ln_g, ln_b, alpha, seq):
    t, d = x2.shape
    bm = 128
    nbs = seq // bm
    nblk = t // bm
    with_next = next_row_of is not None
    dest3 = dest.reshape(nblk, bm, 2).transpose(0, 2, 1).reshape(nblk, 1, 2 * bm)

    def mrow(fn, which):
        return lambda i: (fn(i // nbs, which), 0, 0)

    in_specs = [
        pl.BlockSpec((None, 1, 2 * bm), lambda i: (i, 0, 0), memory_space=pltpu.SMEM),
        pl.BlockSpec((None, 1, 2 * bm), lambda i: (jnp.minimum(i + 1, nblk - 1), 0, 0), memory_space=pltpu.SMEM),
        pl.BlockSpec((bm, d), lambda i: (i, 0)),
        pl.BlockSpec((bm, LANES), lambda i: (i, 0)),
        pl.BlockSpec((None, 1, d), mrow(row_of, 5)),
        pl.BlockSpec((1, d), lambda i: (0, 0)),
        pl.BlockSpec((1, d), lambda i: (0, 0)),
    ]
    args = [dest3, dest3, x2, meta, mod, ln_g, ln_b]
    out_specs = [pl.BlockSpec((bm, d), lambda i: (i, 0))]
    out_shape = [jax.ShapeDtypeStruct((t, d), F32)]
    if with_next:
        in_specs += [pl.BlockSpec((None, 1, d), mrow(next_row_of, 1)),
                     pl.BlockSpec((None, 1, d), mrow(next_row_of, 0))]
        args += [mod, mod]
        out_specs.append(pl.BlockSpec((bm, d), lambda i: (i, 0)))
        out_shape.append(jax.ShapeDtypeStruct((t, d), BF16))
    in_specs.append(pl.BlockSpec(memory_space=pl.ANY))
    args.append(ysort)
    return pl.pallas_call(
        functools.partial(_ln_combine_kernel, alpha=alpha, with_next=with_next),
        grid=(nblk,),
        in_specs=in_specs,
        out_specs=out_specs,
        out_shape=out_shape,
        scratch_shapes=[pltpu.VMEM((2, 2 * bm, d), F32), pltpu.SemaphoreType.DMA((2,))],
        compiler_params=_cparams(1),
        name="ln_moe_combine",
    )(*args)


def kernel(x, c, w_ada, b_ada, ln_g, ln_b, w_in_a, lb_logits, head_gain_a, w_out_a, w_in_b, attn_sinks, w_out_b, rel_bias, w_router_group, b_router_group, w_router_expert, b_router_expert, w_gate_up, w_down):
    bsz, seq, d = x.shape
    depth = w_ada.shape[0]
    n_groups = w_router_group.shape[2]
    n_experts = w_router_expert.shape[2]
    alpha = (2 * depth) ** 0.25
    t = bsz * seq
    moe_bm = 128

    mod = _ada_modulation(c, w_ada, b_ada)

    def row_of_layer(layer):
        return lambda b, which: (layer * bsz + b) * 6 + which

    x2 = x.reshape(t, d).astype(F32)
    row0 = row_of_layer(0)
    hm = _modulate(x2, mod, lambda b: row0(b, 1), lambda b: row0(b, 0), bsz, seq)

    for layer in range(depth):
        row_of = row_of_layer(layer)
        j = layer // 2
        if layer % 2 == 0:
            proj = _matmul(hm, w_in_a, j, BF16)
            o = _hgrn_mixer(proj, lb_logits, head_gain_a, layer, j, bsz, seq)
            y = _matmul(o, w_out_a, j, F32)
        else:
            proj = _matmul(hm, w_in_b, j, BF16)
            o = _attn_mixer(proj, attn_sinks, rel_bias, j, bsz, seq, d)
            y = _matmul(o, w_out_b, j, F32)

        w_router = jnp.zeros((d, LANES), F32)
        w_router = w_router.at[:, :n_groups].set(w_router_group[layer].astype(F32))
        w_router = w_router.at[:, n_groups:n_groups + n_experts].set(w_router_expert[layer].astype(F32))
        b_router = jnp.zeros((1, LANES), F32)
        b_router = b_router.at[0, :n_groups].set(b_router_group[layer].astype(F32))
        b_router = b_router.at[0, n_groups:n_groups + n_experts].set(b_router_expert[layer].astype(F32))

        x2, hf, meta, counts = _ln_router(
            x2, y, mod, row_of, ln_g[layer, 0:1].astype(F32), ln_b[layer, 0:1].astype(F32),
            w_router, b_router, alpha, n_groups, n_experts, seq)
        dest, tok_pad, blk_e, nxt_e, first, nused = _moe_plan(meta, counts, n_groups, n_experts, moe_bm)
        ysort = _moe_experts(hf, tok_pad, blk_e, nxt_e, first, nused, w_gate_up, w_down, layer, moe_bm)
        next_row_of = row_of_layer(layer + 1) if layer + 1 < depth else None
        outs = _ln_combine(x2, meta, dest, ysort, mod, row_of, next_row_of,
                           ln_g[layer, 1:2].astype(F32), ln_b[layer, 1:2].astype(F32), alpha, seq)
        x2 = outs[0]
        if next_row_of is not None:
            hm = outs[1]

    return x2.reshape(bsz, seq, d).astype(x.dtype)
```

```python
import functools
import math

import jax
import jax.numpy as jnp
from jax import lax
from jax.experimental import pallas as pl
from jax.experimental.pallas import tpu as pltpu

F32 = jnp.float32
BF16 = jnp.bfloat16
I32 = jnp.int32

LANES = 128
SUBLANES = 8
V7X_VMEM_LIMIT_BYTES = 56 * 1024 * 1024

HG_HEAD_DIM = 128
ATT_HEAD_DIM = 64
ATT_BLOCK = 128
WINDOW = 128
N_BUCKETS = 32
MAX_DISTANCE = 128
LN_EPS = 1e-5
RMS_EPS = 1e-6
NEG_INF = float("-inf")


def _cparams(n_axes):
    return pltpu.CompilerParams(
        dimension_semantics=("arbitrary",) * n_axes,
        vmem_limit_bytes=V7X_VMEM_LIMIT_BYTES,
    )


def _sigmoid(x):
    return 1.0 / (1.0 + jnp.exp(-x))


def _silu(x):
    return x * _sigmoid(x)


def _dot_nt(a, b):
    return lax.dot_general(a, b, (((1,), (1,)), ((), ())), preferred_element_type=F32)


def _dot_tn(a, b):
    return lax.dot_general(a, b, (((0,), (0,)), ((), ())), preferred_element_type=F32)


def _dot(a, b):
    return jnp.dot(a, b, preferred_element_type=F32)


def _split_bf16(x, parts):
    out = []
    r = x
    for _ in range(parts):
        h = r.astype(BF16)
        out.append(h)
        r = r - h.astype(F32)
    return out


def _ada_kernel(c_ref, w_ref, b_ref, o_ref):
    ca = _silu(c_ref[...]).astype(BF16)
    o_ref[...] = _dot(ca, w_ref[...].astype(BF16)) + b_ref[...]


def _ada_modulation(c, w_ada, b_ada):
    nl, d, n6 = w_ada.shape
    bsz = c.shape[0]
    rows = -(-bsz // SUBLANES) * SUBLANES
    c8 = jnp.zeros((rows, d), F32).at[:bsz].set(c.astype(F32))
    tn = 512
    out = pl.pallas_call(
        _ada_kernel,
        grid=(nl, n6 // tn),
        in_specs=[
            pl.BlockSpec((rows, d), lambda l, j: (0, 0)),
            pl.BlockSpec((None, d, tn), lambda l, j: (l, 0, j)),
            pl.BlockSpec((None, 1, tn), lambda l, j: (l, 0, j)),
        ],
        out_specs=pl.BlockSpec((None, rows, tn), lambda l, j: (l, 0, j)),
        out_shape=jax.ShapeDtypeStruct((nl, rows, n6), F32),
        compiler_params=_cparams(2),
        name="ada_modulation",
    )(c8, w_ada, b_ada.reshape(nl, 1, n6))
    return out[:, :bsz].reshape(nl * bsz * 6, 1, d)


def _modulate_kernel(x_ref, sc_ref, sh_ref, o_ref):
    o_ref[...] = (x_ref[...] * (1.0 + sc_ref[...]) + sh_ref[...]).astype(o_ref.dtype)


def _modulate(x2, mod, sc_row, sh_row, bsz, seq):
    t, d = x2.shape
    bs = min(512, seq)
    nbs = seq // bs
    return pl.pallas_call(
        _modulate_kernel,
        grid=(t // bs,),
        in_specs=[
            pl.BlockSpec((bs, d), lambda i: (i, 0)),
            pl.BlockSpec((None, 1, d), lambda i: (sc_row(i // nbs), 0, 0)),
            pl.BlockSpec((None, 1, d), lambda i: (sh_row(i // nbs), 0, 0)),
        ],
        out_specs=pl.BlockSpec((bs, d), lambda i: (i, 0)),
        out_shape=jax.ShapeDtypeStruct((t, d), BF16),
        compiler_params=_cparams(1),
        name="modulate",
    )(x2, mod, mod)


def _matmul_kernel(x_ref, w_ref, o_ref, wbf_ref):
    @pl.when(pl.program_id(1) == 0)
    def _():
        wbf_ref[...] = w_ref[...].astype(BF16)

    o_ref[...] = _dot(x_ref[...], wbf_ref[...]).astype(o_ref.dtype)


def _matmul(x, w3, layer, out_dtype):
    m, k = x.shape
    n = w3.shape[2]
    bm = min(1024, m)
    bn = 512
    return pl.pallas_call(
        _matmul_kernel,
        grid=(n // bn, m // bm),
        in_specs=[
            pl.BlockSpec((bm, k), lambda j, i: (i, 0)),
            pl.BlockSpec((None, k, bn), lambda j, i: (layer, 0, j)),
        ],
        out_specs=pl.BlockSpec((bm, bn), lambda j, i: (i, j)),
        out_shape=jax.ShapeDtypeStruct((m, n), out_dtype),
        scratch_shapes=[pltpu.VMEM((k, bn), BF16)],
        compiler_params=_cparams(2),
        name="dense_projection",
    )(x, w3)


def _hgrn_kernel(q_ref, f_ref, v_ref, g_ref, lbl_ref, gain_ref, o_ref,
                 st_ref, b_ref, tri_ref, mask_ref, *, layer, chunk):
    c = chunk
    nlev = int(math.log2(c))
    first = (pl.program_id(0) == 0) & (pl.program_id(1) == 0) & (pl.program_id(2) == 0)

    @pl.when(first)
    def _():
        row = lax.broadcasted_iota(I32, (c, c), 0)
        col = lax.broadcasted_iota(I32, (c, c), 1)
        tri_ref[...] = jnp.where(row >= col, 1.0, 0.0).astype(BF16)
        x = row ^ col
        mask_ref[0] = jnp.where(x == 0, 1.0, 0.0).astype(F32)
        for lv in range(1, nlev):
            mask_ref[lv] = jnp.where(x < (1 << lv), 1.0, 0.0).astype(F32)

    @pl.when(pl.program_id(2) == 0)
    def _():
        st_ref[...] = jnp.zeros_like(st_ref)

    lbl = lbl_ref[...]
    nrow = lbl.shape[0]
    rows = [lbl[i:i + 1, :] for i in range(nrow)]
    mx = functools.reduce(jnp.maximum, rows)
    es = [jnp.exp(r - mx) for r in rows]
    lb = functools.reduce(lambda a, b: a + b, es[:layer + 1]) / functools.reduce(lambda a, b: a + b, es)

    q = _silu(q_ref[...].astype(F32))
    forget = lb + (1.0 - lb) * _sigmoid(f_ref[...].astype(F32))
    k = 1.0 - forget
    logf = jnp.log(forget)
    v = v_ref[...]

    hml = jnp.concatenate(_split_bf16(logf, 3), axis=1)
    bb = _dot(tri_ref[...], hml)
    b = bb[:, 0:LANES] + bb[:, LANES:2 * LANES] + bb[:, 2 * LANES:3 * LANES]
    b_ref[...] = b

    def bcast_row(r):
        return jnp.broadcast_to(b_ref[pl.ds(r, 1), :], (SUBLANES, HG_HEAD_DIM))

    rowi = lax.broadcasted_iota(I32, (c, HG_HEAD_DIM), 0)
    sub = lax.broadcasted_iota(I32, (SUBLANES, HG_HEAD_DIM), 0)
    ntile = c // SUBLANES

    qb = q.astype(BF16)
    kb = k.astype(BF16)
    attn = _dot_nt(qb, kb) * mask_ref[0]
    for lv in range(nlev):
        m = 1 << lv
        isq = (rowi & m) != 0
        if m == 1:
            e = jnp.where(isq, forget, 1.0)
        else:
            if m >= SUBLANES:
                n = 2 * m
                tiles = [bcast_row((j * SUBLANES // n) * n + m - 1) for j in range(ntile)]
            elif m == 4:
                tiles = [bcast_row(j * SUBLANES + 3) for j in range(ntile)]
            else:
                tiles = [jnp.where(sub < 4, bcast_row(j * SUBLANES + 1), bcast_row(j * SUBLANES + 5))
                         for j in range(ntile)]
            ref_pt = jnp.concatenate(tiles, axis=0)
            e = jnp.exp(jnp.where(isq, b - ref_pt, ref_pt - b))
        qt = jnp.where(isq, q * e, 0.0).astype(BF16)
        kt = jnp.where(isq, 0.0, k * e).astype(BF16)
        a = _dot_nt(qt, kt)
        if lv + 1 < nlev:
            a = a * mask_ref[lv + 1]
        attn = attn + a

    o = _dot(attn.astype(BF16), v)

    st = st_ref[...]
    o = o + _dot_nt((q * jnp.exp(b)).astype(BF16), st.astype(BF16))
    b_last = bcast_row(c - 1)
    kd = (k * jnp.exp(jnp.concatenate([b_last] * ntile, axis=0) - b)).astype(BF16)
    st_ref[...] = st * jnp.exp(b_last[0:1, :]) + _dot_tn(v, kd)

    o = o * lax.rsqrt(jnp.mean(o * o, axis=-1, keepdims=True) + RMS_EPS)
    o = o * gain_ref[...] * _silu(g_ref[...].astype(F32))
    o_ref[...] = o.astype(o_ref.dtype)


def _hgrn_mixer(proj, lb_logits, head_gain, layer, j, bsz, seq):
    t, d4 = proj.shape
    d = d4 // 4
    nh = d // HG_HEAD_DIM
    chunk = 256 if seq % 256 == 0 else 128
    nc = seq // chunk
    nlev = int(math.log2(chunk))

    def col(part):
        return lambda b, h, c: (b * nc + c, part * nh + h)

    return pl.pallas_call(
        functools.partial(_hgrn_kernel, layer=layer, chunk=chunk),
        grid=(bsz, nh, nc),
        in_specs=[
            pl.BlockSpec((chunk, HG_HEAD_DIM), col(0)),
            pl.BlockSpec((chunk, HG_HEAD_DIM), col(1)),
            pl.BlockSpec((chunk, HG_HEAD_DIM), col(2)),
            pl.BlockSpec((chunk, HG_HEAD_DIM), col(3)),
            pl.BlockSpec((lb_logits.shape[0], HG_HEAD_DIM), lambda b, h, c: (0, h)),
            pl.BlockSpec((None, 1, HG_HEAD_DIM), lambda b, h, c: (j, 0, h)),
        ],
        out_specs=pl.BlockSpec((chunk, HG_HEAD_DIM), lambda b, h, c: (b * nc + c, h)),
        out_shape=jax.ShapeDtypeStruct((t, d), BF16),
        scratch_shapes=[
            pltpu.VMEM((HG_HEAD_DIM, HG_HEAD_DIM), F32),
            pltpu.VMEM((chunk, HG_HEAD_DIM), F32),
            pltpu.VMEM((chunk, chunk), BF16),
            pltpu.VMEM((nlev, chunk, chunk), F32),
        ],
        compiler_params=_cparams(3),
        name="hgrn2_mixer",
    )(proj, proj, proj, proj, lb_logits.astype(F32), head_gain.astype(F32).reshape(head_gain.shape[0], 1, d))


def _t5_bucket(dist):
    max_exact = N_BUCKETS // 2
    n = jnp.maximum(dist, 0)
    large = max_exact + (jnp.log(jnp.maximum(n, 1).astype(F32) / max_exact)
                         / math.log(MAX_DISTANCE / max_exact)
                         * (N_BUCKETS - max_exact)).astype(I32)
    large = jnp.minimum(large, N_BUCKETS - 1)
    return jnp.where(n < max_exact, n, large)


def _attn_kernel(rb_ref, sink_ref, q_ref, kp_ref, kc_ref, vp_ref, vc_ref, bucket_ref, o_ref,
                 bias_ref, *, group, layer_j):
    blk = ATT_BLOCK
    hd = ATT_HEAD_DIM
    pr = pl.program_id(1)
    i = pl.program_id(2)
    heads_per_step = 2 * group

    @pl.when(i == 0)
    def _():
        qi = lax.broadcasted_iota(I32, (blk, 2 * blk), 0)
        sj = lax.broadcasted_iota(I32, (blk, 2 * blk), 1)
        dist = qi + blk - sj
        band = (dist >= 0) & (dist < WINDOW)
        bucket = bucket_ref[...]

        def per_head(hh, carry):
            h = pr * heads_per_step + hh
            tbl = jnp.zeros((blk, 2 * blk), F32)
            for bk in range(N_BUCKETS):
                tbl = jnp.where(bucket == bk, rb_ref[bk, h], tbl)
            bias_ref[hh] = jnp.where(band, tbl, NEG_INF)
            bias_ref[heads_per_step + hh] = jnp.where(band & (sj >= blk), tbl, NEG_INF)
            return carry

        lax.fori_loop(0, heads_per_step, per_head, 0)

    lane = lax.broadcasted_iota(I32, (2 * blk, LANES), 1)
    table0 = jnp.where(i == 0, heads_per_step, 0)

    kk = jnp.concatenate([kp_ref[...], kc_ref[...]], axis=0).astype(F32)
    vv = jnp.concatenate([vp_ref[...], vc_ref[...]], axis=0).astype(F32)
    q_all = q_ref[...] * (hd ** -0.5)

    kbds, vbds = [], []
    for c in range(2):
        if c == 0:
            klo = jnp.where(lane < hd, kk, 0.0)
            khi = pltpu.roll(klo, hd, axis=1)
            vlo = jnp.where(lane < hd, vv, 0.0)
            vhi = pltpu.roll(vlo, hd, axis=1)
        else:
            khi = jnp.where(lane >= hd, kk, 0.0)
            klo = pltpu.roll(khi, hd, axis=1)
            vhi = jnp.where(lane >= hd, vv, 0.0)
            vlo = pltpu.roll(vhi, hd, axis=1)
        kbds.append(jnp.concatenate([klo, khi], axis=0).astype(BF16))
        vbds.append(jnp.concatenate([vlo, vhi], axis=0).astype(BF16))

    half = group // 2
    tiles = [(c, p) for c in range(2) for p in range(half)]
    lgs = [_dot_nt(q_all[:, (c * half + p) * LANES:(c * half + p + 1) * LANES], kbds[c]) for c, p in tiles]
    p2s, rinvs = [], []
    for (c, p), lg in zip(tiles, lgs):
        probs, rinv = [], []
        for hh in range(2):
            hl = c * group + 2 * p + hh
            sk = sink_ref[layer_j, pr * heads_per_step + hl]
            l = lg[:, hh * 2 * blk:(hh + 1) * 2 * blk] + bias_ref[table0 + hl]
            mx = jnp.maximum(jnp.max(l, axis=-1, keepdims=True), sk)
            pe = jnp.exp(l - mx)
            rinv.append(1.0 / (jnp.sum(pe, axis=-1, keepdims=True) + jnp.exp(sk - mx)))
            probs.append(pe.astype(BF16))
        p2s.append(jnp.concatenate(probs, axis=1))
        rinvs.append(rinv)
    lane_o = lax.broadcasted_iota(I32, (blk, LANES), 1)
    for (c, p), p2, rinv in zip(tiles, p2s, rinvs):
        tile = c * half + p
        o = _dot(p2, vbds[c]) * jnp.where(lane_o < hd, rinv[0], rinv[1])
        o_ref[:, tile * LANES:(tile + 1) * LANES] = o.astype(o_ref.dtype)


def _attn_mixer(proj, sinks, rel_bias, layer_j, bsz, seq, d):
    t, att_in = proj.shape
    kvw = (att_in - d) // 2
    n_heads = d // ATT_HEAD_DIM
    kvh = kvw // ATT_HEAD_DIM
    group = n_heads // kvh
    assert kvh % 2 == 0 and group % 2 == 0
    blk = ATT_BLOCK
    nb = seq // blk
    npair = kvh // 2
    qw = 2 * group * ATT_HEAD_DIM
    k0 = d // LANES
    v0 = (d + kvw) // LANES

    qi = jnp.arange(blk)[:, None]
    sj = jnp.arange(2 * blk)[None, :]
    bucket = _t5_bucket(qi + blk - sj).astype(I32)

    def prev(i):
        return jnp.maximum(i - 1, 0)

    grid_spec = pltpu.PrefetchScalarGridSpec(
        num_scalar_prefetch=2,
        grid=(bsz, npair, nb),
        in_specs=[
            pl.BlockSpec((blk, qw), lambda b, p, i, *_: (b * nb + i, p)),
            pl.BlockSpec((blk, LANES), lambda b, p, i, *_: (b * nb + prev(i), k0 + p)),
            pl.BlockSpec((blk, LANES), lambda b, p, i, *_: (b * nb + i, k0 + p)),
            pl.BlockSpec((blk, LANES), lambda b, p, i, *_: (b * nb + prev(i), v0 + p)),
            pl.BlockSpec((blk, LANES), lambda b, p, i, *_: (b * nb + i, v0 + p)),
            pl.BlockSpec((blk, 2 * blk), lambda b, p, i, *_: (0, 0)),
        ],
        out_specs=pl.BlockSpec((blk, qw), lambda b, p, i, *_: (b * nb + i, p)),
        scratch_shapes=[pltpu.VMEM((4 * group, blk, 2 * blk), F32)],
    )
    return pl.pallas_call(
        functools.partial(_attn_kernel, group=group, layer_j=layer_j),
        grid_spec=grid_spec,
        out_shape=jax.ShapeDtypeStruct((t, d), BF16),
        compiler_params=_cparams(3),
        name="swa_sink_mixer",
    )(rel_bias.astype(F32), sinks.astype(F32), proj, proj, proj, proj, proj, bucket)


def _layer_norm_rows(z, g, b):
    mu = jnp.mean(z, axis=-1, keepdims=True)
    zc = z - mu
    var = jnp.mean(zc * zc, axis=-1, keepdims=True)
    return zc * lax.rsqrt(var + LN_EPS) * g + b


def _ln_router_kernel(x_ref, y_ref, gate_ref, sc_ref, sh_ref, lng_ref, lnb_ref, wr_ref, br_ref,
                      xo_ref, hf_ref, meta_ref, cnt_ref, carry_ref, *, alpha, n_groups, n_experts):
    i = pl.program_id(0)
    bm = x_ref.shape[0]
    epg = n_experts // n_groups

    @pl.when(i == 0)
    def _():
        carry_ref[...] = jnp.zeros_like(carry_ref)

    z = alpha * x_ref[...] + (1.0 + gate_ref[...]) * y_ref[...]
    xn = _layer_norm_rows(z, lng_ref[...], lnb_ref[...])
    xo_ref[...] = xn
    hf = xn * (1.0 + sc_ref[...]) + sh_ref[...]
    hf_ref[...] = hf

    xh, xl = _split_bf16(hf, 2)
    wh, wl = _split_bf16(wr_ref[...], 2)
    lg = _dot(xh, wh) + _dot(xh, wl) + _dot(xl, wh) + br_ref[...]

    lane = lax.broadcasted_iota(I32, (bm, LANES), 1)
    lanef = lane.astype(F32)
    big = float(LANES)

    gl = jnp.where(lane < n_groups, lg, NEG_INF)
    gmax = jnp.max(gl, axis=-1, keepdims=True)
    gsel = jnp.min(jnp.where(gl == gmax, lanef, big), axis=-1, keepdims=True)
    p_group = 1.0 / jnp.sum(jnp.exp(gl - gmax), axis=-1, keepdims=True)

    lo = n_groups + gsel * epg
    el = jnp.where((lanef >= lo) & (lanef < lo + epg), lg, NEG_INF)
    m1 = jnp.max(el, axis=-1, keepdims=True)
    i1 = jnp.min(jnp.where(el == m1, lanef, big), axis=-1, keepdims=True)
    el2 = jnp.where(lanef == i1, NEG_INF, el)
    m2 = jnp.max(el2, axis=-1, keepdims=True)
    i2 = jnp.min(jnp.where(el2 == m2, lanef, big), axis=-1, keepdims=True)
    e21 = jnp.exp(m2 - m1)
    g0 = p_group / (1.0 + e21)
    g1 = g0 * e21

    oh0 = lanef == i1
    oh1 = lanef == i2
    cnt = jnp.where(oh0 | oh1, 1.0, 0.0)
    row = lax.broadcasted_iota(I32, (bm, bm), 0)
    col = lax.broadcasted_iota(I32, (bm, bm), 1)
    stril = jnp.where(row > col, 1.0, 0.0).astype(BF16)
    before = _dot(stril, cnt.astype(BF16)) + carry_ref[...]
    r0 = jnp.sum(jnp.where(oh0, before, 0.0), axis=-1, keepdims=True)
    r1 = jnp.sum(jnp.where(oh1, before, 0.0), axis=-1, keepdims=True)
    carry_ref[...] = carry_ref[...] + jnp.sum(cnt, axis=0, keepdims=True)
    cnt_ref[...] = carry_ref[...]

    meta = jnp.where(lane == 0, i1 - n_groups, 0.0)
    meta = jnp.where(lane == 1, i2 - n_groups, meta)
    meta = jnp.where(lane == 2, g0, meta)
    meta = jnp.where(lane == 3, g1, meta)
    meta = jnp.where(lane == 4, r0, meta)
    meta = jnp.where(lane == 5, r1, meta)
    meta_ref[...] = meta


def _ln_router(x2, y2, mod, row_of, ln_g, ln_b, w_router, b_router, alpha, n_groups, n_experts, seq):
    t, d = x2.shape
    bm = 128
    nbs = seq // bm

    def mrow(which):
        return lambda i: (row_of(i // nbs, which), 0, 0)

    return pl.pallas_call(
        functools.partial(_ln_router_kernel, alpha=alpha, n_groups=n_groups, n_experts=n_experts),
        grid=(t // bm,),
        in_specs=[
            pl.BlockSpec((bm, d), lambda i: (i, 0)),
            pl.BlockSpec((bm, d), lambda i: (i, 0)),
            pl.BlockSpec((None, 1, d), mrow(2)),
            pl.BlockSpec((None, 1, d), mrow(4)),
            pl.BlockSpec((None, 1, d), mrow(3)),
            pl.BlockSpec((1, d), lambda i: (0, 0)),
            pl.BlockSpec((1, d), lambda i: (0, 0)),
            pl.BlockSpec((d, LANES), lambda i: (0, 0)),
            pl.BlockSpec((1, LANES), lambda i: (0, 0)),
        ],
        out_specs=[
            pl.BlockSpec((bm, d), lambda i: (i, 0)),
            pl.BlockSpec((bm, d), lambda i: (i, 0)),
            pl.BlockSpec((bm, LANES), lambda i: (i, 0)),
            pl.BlockSpec((1, LANES), lambda i: (0, 0)),
        ],
        out_shape=[
            jax.ShapeDtypeStruct((t, d), F32),
            jax.ShapeDtypeStruct((t, d), F32),
            jax.ShapeDtypeStruct((t, LANES), F32),
            jax.ShapeDtypeStruct((1, LANES), F32),
        ],
        scratch_shapes=[pltpu.VMEM((1, LANES), F32)],
        compiler_params=_cparams(1),
        name="ln_router",
    )(x2, y2, mod, mod, mod, ln_g, ln_b, w_router, b_router)


def _cast_rows(src_ref, dst_ref, rows=128):
    def body(r, carry):
        sl = pl.ds(pl.multiple_of(r * rows, rows), rows)
        dst_ref[sl, :] = src_ref[sl, :].astype(dst_ref.dtype)
        return carry
    lax.fori_loop(0, src_ref.shape[0] // rows, body, 0)


def _moe_kernel(blk_e_ref, nxt_e_ref, first_ref, nused_ref,
                tokc_ref, tokn_ref, hf_hbm, wgu_hbm, wdn_hbm, y_ref,
                xbuf, wgu_st, wdn_st, wgu_bf, wdn_bf, gsem, wsem, *, layer, d_expert):
    i = pl.program_id(0)
    bm = xbuf.shape[1]
    nused = nused_ref[0]
    slot = i % 2

    def row_copy(tok, r, s):
        return pltpu.make_async_copy(hf_hbm.at[pl.ds(tok, 1), :], xbuf.at[s, pl.ds(r, 1), :], gsem.at[s])

    def issue_rows(tok_ref, s):
        def body(r, carry):
            row_copy(tok_ref[0, r], r, s).start()
            return carry
        lax.fori_loop(0, bm, body, 0, unroll=8)

    def wait_rows(s):
        pltpu.make_async_copy(hf_hbm.at[pl.ds(0, bm), :], xbuf.at[s], gsem.at[s]).wait()

    def weight_copies(e):
        return (pltpu.make_async_copy(wgu_hbm.at[layer, e], wgu_st, wsem.at[0]),
                pltpu.make_async_copy(wdn_hbm.at[layer, e], wdn_st, wsem.at[1]))

    @pl.when(i == 0)
    def _():
        issue_rows(tokc_ref, 0)
        for cp in weight_copies(blk_e_ref[0]):
            cp.start()

    @pl.when((i < nused) & (first_ref[i] == 1))
    def _():
        for cp in weight_copies(blk_e_ref[i]):
            cp.wait()
        _cast_rows(wgu_st, wgu_bf)
        _cast_rows(wdn_st, wdn_bf)

        @pl.when(nxt_e_ref[i] >= 0)
        def _():
            for cp in weight_copies(nxt_e_ref[i]):
                cp.start()

    @pl.when(i + 1 < nused)
    def _():
        issue_rows(tokn_ref, 1 - slot)

    @pl.when(i < nused)
    def _():
        wait_rows(slot)
        xb = xbuf[slot].astype(BF16)
        a = _dot(xb, wgu_bf[...])
        h = _silu(a[:, :d_expert]) * a[:, d_expert:]
        y_ref[...] = _dot(h.astype(BF16), wdn_bf[...])

    @pl.when(i >= nused)
    def _():
        y_ref[...] = jnp.zeros_like(y_ref)


def _moe_experts(hf, tok_pad, blk_e, nxt_e, first, nused, w_gate_up, w_down, layer, bm):
    t, d = hf.shape
    nb = blk_e.shape[0]
    d_expert = w_down.shape[2]
    tok3 = tok_pad.reshape(nb, 1, bm)
    grid_spec = pltpu.PrefetchScalarGridSpec(
        num_scalar_prefetch=4,
        grid=(nb,),
        in_specs=[
            pl.BlockSpec((None, 1, bm), lambda i, *_: (i, 0, 0), memory_space=pltpu.SMEM),
            pl.BlockSpec((None, 1, bm), lambda i, *_: (jnp.minimum(i + 1, nb - 1), 0, 0), memory_space=pltpu.SMEM),
            pl.BlockSpec(memory_space=pl.ANY),
            pl.BlockSpec(memory_space=pl.ANY),
            pl.BlockSpec(memory_space=pl.ANY),
        ],
        out_specs=pl.BlockSpec((bm, d), lambda i, *_: (i, 0)),
        scratch_shapes=[
            pltpu.VMEM((2, bm, d), F32),
            pltpu.VMEM((d, 2 * d_expert), F32),
            pltpu.VMEM((d_expert, d), F32),
            pltpu.VMEM((d, 2 * d_expert), BF16),
            pltpu.VMEM((d_expert, d), BF16),
            pltpu.SemaphoreType.DMA((2,)),
            pltpu.SemaphoreType.DMA((2,)),
        ],
    )
    return pl.pallas_call(
        functools.partial(_moe_kernel, layer=layer, d_expert=d_expert),
        grid_spec=grid_spec,
        out_shape=jax.ShapeDtypeStruct((nb * bm, d), F32),
        compiler_params=_cparams(1),
        name="moe_experts",
    )(blk_e, nxt_e, first, nused, tok3, tok3, hf, w_gate_up, w_down)


def _moe_plan(meta, counts_row, n_groups, n_experts, bm):
    t = meta.shape[0]
    eid = meta[:, 0:2].astype(I32)
    rank = meta[:, 4:6].astype(I32)
    counts = counts_row[0, n_groups:n_groups + n_experts].astype(I32)
    padded = (counts + bm - 1) // bm * bm
    pad_end = jnp.cumsum(padded)
    pad_start = pad_end - padded
    dest = pad_start[eid] + rank
    nb = (2 * t) // bm + n_experts
    nused = pad_end[-1] // bm
    ids = jnp.arange(nb, dtype=I32)
    raw_e = jnp.minimum(jnp.sum((pad_end[None, :] <= (ids * bm)[:, None]).astype(I32), axis=1), n_experts - 1)
    used = ids < nused
    blk_e = jnp.where(used, raw_e, raw_e[nused - 1])
    prev_e = jnp.concatenate([jnp.full((1,), -1, I32), blk_e[:-1]])
    first = (used & (blk_e != prev_e)).astype(I32)
    key = jnp.where(used, blk_e, n_experts)
    nxt_idx = jnp.sum((key[None, :] <= blk_e[:, None]).astype(I32), axis=1)
    nxt_e = jnp.where(nxt_idx < nused, key[jnp.minimum(nxt_idx, nb - 1)], -1).astype(I32)
    tok = jnp.repeat(jnp.arange(t, dtype=I32), 2)
    tok_pad = jnp.zeros((nb * bm,), I32).at[dest.reshape(-1)].set(tok)
    return dest, tok_pad, blk_e, nxt_e, first, nused.reshape(1).astype(I32)


def _ln_combine_kernel(*refs, alpha, with_next):
    if with_next:
        (dc_ref, dn_ref, x_ref, meta_ref, gate_ref, lng_ref, lnb_ref, sc_ref, sh_ref, y_hbm,
         xo_ref, hm_ref, ybuf, sem) = refs
    else:
        (dc_ref, dn_ref, x_ref, meta_ref, gate_ref, lng_ref, lnb_ref, y_hbm,
         xo_ref, ybuf, sem) = refs
    i = pl.program_id(0)
    n = pl.num_programs(0)
    bm = x_ref.shape[0]
    slot = i % 2

    def row_copy(src, r, s):
        return pltpu.make_async_copy(y_hbm.at[pl.ds(src, 1), :], ybuf.at[s, pl.ds(r, 1), :], sem.at[s])

    def issue_rows(d_ref, s):
        def body(r, carry):
            row_copy(d_ref[0, r], r, s).start()
            return carry
        lax.fori_loop(0, 2 * bm, body, 0, unroll=8)

    def wait_rows(s):
        pltpu.make_async_copy(y_hbm.at[pl.ds(0, 2 * bm), :], ybuf.at[s], sem.at[s]).wait()

    @pl.when(i == 0)
    def _():
        issue_rows(dc_ref, 0)

    @pl.when(i + 1 < n)
    def _():
        issue_rows(dn_ref, 1 - slot)

    wait_rows(slot)
    meta = meta_ref[...]
    y = ybuf[slot, 0:bm, :] * meta[:, 2:3] + ybuf[slot, bm:2 * bm, :] * meta[:, 3:4]
    z = alpha * x_ref[...] + (1.0 + gate_ref[...]) * y
    xn = _layer_norm_rows(z, lng_ref[...], lnb_ref[...])
    xo_ref[...] = xn
    if with_next:
        hm_ref[...] = (xn * (1.0 + sc_ref[...]) + sh_ref[...]).astype(hm_ref.dtype)


def _ln_combine(x2, meta, dest, ysort, mod, row_of, next_row_of, ln_g, ln_b, alpha, seq):
    t, d = x2.shape
    bm = 128
    nbs = seq // bm
    nblk = t // bm
    with_next = next_row_of is not None
    dest3 = dest.reshape(nblk, bm, 2).transpose(0, 2, 1).reshape(nblk, 1, 2 * bm)

    def mrow(fn, which):
        return lambda i: (fn(i // nbs, which), 0, 0)

    in_specs = [
        pl.BlockSpec((None, 1, 2 * bm), lambda i: (i, 0, 0), memory_space=pltpu.SMEM),
        pl.BlockSpec((None, 1, 2 * bm), lambda i: (jnp.minimum(i + 1, nblk - 1), 0, 0), memory_space=pltpu.SMEM),
        pl.BlockSpec((bm, d), lambda i: (i, 0)),
        pl.BlockSpec((bm, LANES), lambda i: (i, 0)),
        pl.BlockSpec((None, 1, d), mrow(row_of, 5)),
        pl.BlockSpec((1, d), lambda i: (0, 0)),
        pl.BlockSpec((1, d), lambda i: (0, 0)),
    ]
    args = [dest3, dest3, x2, meta, mod, ln_g, ln_b]
    out_specs = [pl.BlockSpec((bm, d), lambda i: (i, 0))]
    out_shape = [jax.ShapeDtypeStruct((t, d), F32)]
    if with_next:
        in_specs += [pl.BlockSpec((None, 1, d), mrow(next_row_of, 1)),
                     pl.BlockSpec((None, 1, d), mrow(next_row_of, 0))]
        args += [mod, mod]
        out_specs.append(pl.BlockSpec((bm, d), lambda i: (i, 0)))
        out_shape.append(jax.ShapeDtypeStruct((t, d), BF16))
    in_specs.append(pl.BlockSpec(memory_space=pl.ANY))
    args.append(ysort)
    return pl.pallas_call(
        functools.partial(_ln_combine_kernel, alpha=alpha, with_next=with_next),
        grid=(nblk,),
        in_specs=in_specs,
        out_specs=out_specs,
        out_shape=out_shape,
        scratch_shapes=[pltpu.VMEM((2, 2 * bm, d), F32), pltpu.SemaphoreType.DMA((2,))],
        compiler_params=_cparams(1),
        name="ln_moe_combine",
    )(*args)


def kernel(x, c, w_ada, b_ada, ln_g, ln_b, w_in_a, lb_logits, head_gain_a, w_out_a, w_in_b, attn_sinks, w_out_b, rel_bias, w_router_group, b_router_group, w_router_expert, b_router_expert, w_gate_up, w_down):
    bsz, seq, d = x.shape
    depth = w_ada.shape[0]
    n_groups = w_router_group.shape[2]
    n_experts = w_router_expert.shape[2]
    alpha = (2 * depth) ** 0.25
    t = bsz * seq
    moe_bm = 128

    mod = _ada_modulation(c, w_ada, b_ada)

    def row_of_layer(layer):
        return lambda b, which: (layer * bsz + b) * 6 + which

    x2 = x.reshape(t, d).astype(F32)
    row0 = row_of_layer(0)
    hm = _modulate(x2, mod, lambda b: row0(b, 1), lambda b: row0(b, 0), bsz, seq)

    for layer in range(depth):
        row_of = row_of_layer(layer)
        j = layer // 2
        if layer % 2 == 0:
            proj = _matmul(hm, w_in_a, j, BF16)
            o = _hgrn_mixer(proj, lb_logits, head_gain_a, layer, j, bsz, seq)
            y = _matmul(o, w_out_a, j, F32)
        else:
            proj = _matmul(hm, w_in_b, j, BF16)
            o = _attn_mixer(proj, attn_sinks, rel_bias, j, bsz, seq, d)
            y = _matmul(o, w_out_b, j, F32)

        n_pad = LANES - n_groups - n_experts
        w_router = jnp.concatenate(
            [w_router_group[layer].astype(F32), w_router_expert[layer].astype(F32), jnp.zeros((d, n_pad), F32)], axis=1)
        b_router = jnp.concatenate(
            [b_router_group[layer].astype(F32), b_router_expert[layer].astype(F32), jnp.zeros((n_pad,), F32)]
        ).reshape(1, LANES)

        x2, hf, meta, counts = _ln_router(
            x2, y, mod, row_of, ln_g[layer, 0:1].astype(F32), ln_b[layer, 0:1].astype(F32),
            w_router, b_router, alpha, n_groups, n_experts, seq)
        dest, tok_pad, blk_e, nxt_e, first, nused = _moe_plan(meta, counts, n_groups, n_experts, moe_bm)
        ysort = _moe_experts(hf, tok_pad, blk_e, nxt_e, first, nused, w_gate_up, w_down, layer, moe_bm)
        next_row_of = row_of_layer(layer + 1) if layer + 1 < depth else None
        outs = _ln_combine(x2, meta, dest, ysort, mod, row_of, next_row_of,
                           ln_g[layer, 1:2].astype(F32), ln_b[layer, 1:2].astype(F32), alpha, seq)
        x2 = outs[0]
        if next_row_of is not None:
            hm = outs[1]

    return x2.reshape(bsz, seq, d).astype(x.dtype)
```

```python
import functools
import math

import jax
import jax.numpy as jnp
from jax import lax
from jax.experimental import pallas as pl
from jax.experimental.pallas import tpu as pltpu

F32 = jnp.float32
BF16 = jnp.bfloat16
I32 = jnp.int32

LANES = 128
SUBLANES = 8
V7X_VMEM_LIMIT_BYTES = 56 * 1024 * 1024

HG_HEAD_DIM = 128
ATT_HEAD_DIM = 64
ATT_BLOCK = 128
WINDOW = 128
N_BUCKETS = 32
MAX_DISTANCE = 128
LN_EPS = 1e-5
RMS_EPS = 1e-6
NEG_INF = float("-inf")


def _cparams(n_axes):
    return pltpu.CompilerParams(
        dimension_semantics=("arbitrary",) * n_axes,
        vmem_limit_bytes=V7X_VMEM_LIMIT_BYTES,
    )


def _sigmoid(x):
    return 0.5 * jnp.tanh(0.5 * x) + 0.5


def _silu(x):
    return x * _sigmoid(x)


def _dot_nt(a, b):
    return lax.dot_general(a, b, (((1,), (1,)), ((), ())), preferred_element_type=F32)


def _dot_tn(a, b):
    return lax.dot_general(a, b, (((0,), (0,)), ((), ())), preferred_element_type=F32)


def _dot(a, b):
    return jnp.dot(a, b, preferred_element_type=F32)


def _pack_bf16_pairs(x):
    n = x.shape[1] // 2
    bits = pltpu.bitcast(x.astype(BF16).astype(F32), jnp.uint32)
    return (bits[:, :n] >> 16) | (bits[:, n:] & jnp.uint32(0xFFFF0000))


def _unpack_bf16_pairs(w):
    lo = pltpu.bitcast(w << 16, F32)
    hi = pltpu.bitcast(w & jnp.uint32(0xFFFF0000), F32)
    return jnp.concatenate([lo, hi], axis=1)


def _split_bf16(x, parts):
    out = []
    r = x
    for _ in range(parts):
        h = r.astype(BF16)
        out.append(h)
        r = r - h.astype(F32)
    return out


def _ada_kernel(c_ref, w_ref, b_ref, o_ref):
    ca = _silu(c_ref[...]).astype(BF16)
    o_ref[...] = _dot(ca, w_ref[...].astype(BF16)) + b_ref[...]


def _ada_modulation(c, w_ada, b_ada):
    nl, d, n6 = w_ada.shape
    bsz = c.shape[0]
    rows = -(-bsz // SUBLANES) * SUBLANES
    c8 = jnp.zeros((rows, d), F32).at[:bsz].set(c.astype(F32))
    tn = 512
    out = pl.pallas_call(
        _ada_kernel,
        grid=(nl, n6 // tn),
        in_specs=[
            pl.BlockSpec((rows, d), lambda l, j: (0, 0)),
            pl.BlockSpec((None, d, tn), lambda l, j: (l, 0, j)),
            pl.BlockSpec((None, 1, tn), lambda l, j: (l, 0, j)),
        ],
        out_specs=pl.BlockSpec((None, rows, tn), lambda l, j: (l, 0, j)),
        out_shape=jax.ShapeDtypeStruct((nl, rows, n6), F32),
        compiler_params=_cparams(2),
        name="ada_modulation",
    )(c8, w_ada, b_ada.reshape(nl, 1, n6))
    return out[:, :bsz].reshape(nl * bsz * 6, 1, d)


def _modulate_kernel(x_ref, sc_ref, sh_ref, o_ref):
    o_ref[...] = (x_ref[...] * (1.0 + sc_ref[...]) + sh_ref[...]).astype(o_ref.dtype)


def _modulate(x2, mod, sc_row, sh_row, bsz, seq):
    t, d = x2.shape
    bs = min(512, seq)
    nbs = seq // bs
    return pl.pallas_call(
        _modulate_kernel,
        grid=(t // bs,),
        in_specs=[
            pl.BlockSpec((bs, d), lambda i: (i, 0)),
            pl.BlockSpec((None, 1, d), lambda i: (sc_row(i // nbs), 0, 0)),
            pl.BlockSpec((None, 1, d), lambda i: (sh_row(i // nbs), 0, 0)),
        ],
        out_specs=pl.BlockSpec((bs, d), lambda i: (i, 0)),
        out_shape=jax.ShapeDtypeStruct((t, d), BF16),
        compiler_params=_cparams(1),
        name="modulate",
    )(x2, mod, mod)


def _matmul_kernel(x_ref, w_ref, o_ref, wbf_ref):
    @pl.when(pl.program_id(1) == 0)
    def _():
        wbf_ref[...] = w_ref[...].astype(BF16)

    o_ref[...] = _dot(x_ref[...], wbf_ref[...]).astype(o_ref.dtype)


def _matmul(x, w3, layer, out_dtype):
    m, k = x.shape
    n = w3.shape[2]
    bm = min(1024, m)
    bn = 512
    return pl.pallas_call(
        _matmul_kernel,
        grid=(n // bn, m // bm),
        in_specs=[
            pl.BlockSpec((bm, k), lambda j, i: (i, 0)),
            pl.BlockSpec((None, k, bn), lambda j, i: (layer, 0, j)),
        ],
        out_specs=pl.BlockSpec((bm, bn), lambda j, i: (i, j)),
        out_shape=jax.ShapeDtypeStruct((m, n), out_dtype),
        scratch_shapes=[pltpu.VMEM((k, bn), BF16)],
        compiler_params=_cparams(2),
        name="dense_projection",
    )(x, w3)


HG_BASE = 16
HG_BASE_MAX_DECAY = 86.0


def _hgrn_kernel(q_ref, f_ref, v_ref, g_ref, lbl_ref, gain_ref, o_ref,
                 st_ref, b_ref, oi_ref, rest_ref, tri_ref, mask_ref, bmask_ref, cmask_ref, *, layer, chunk):
    c = chunk
    hc = c // 2
    nlev = int(math.log2(c))
    base_lv = int(math.log2(HG_BASE))
    first = (pl.program_id(0) == 0) & (pl.program_id(1) == 0) & (pl.program_id(2) == 0)

    @pl.when(first)
    def _():
        row = lax.broadcasted_iota(I32, (c, c), 0)
        col = lax.broadcasted_iota(I32, (c, c), 1)
        tri_ref[...] = jnp.where(row >= col, 1.0, 0.0).astype(BF16)
        x = row ^ col
        mask_ref[0] = jnp.where(x == 0, 1.0, 0.0).astype(F32)
        for lv in range(1, base_lv + 1):
            mask_ref[lv] = jnp.where(x < (1 << lv), 1.0, 0.0).astype(F32)
        bmask_ref[...] = jnp.where((x < HG_BASE) & (row >= col), 1.0, 0.0).astype(F32)
        xh = lax.broadcasted_iota(I32, (hc, hc), 0) ^ lax.broadcasted_iota(I32, (hc, hc), 1)
        for lv in range(base_lv, nlev - 1):
            cmask_ref[lv - base_lv] = jnp.where(xh < (1 << lv), 1.0, 0.0).astype(F32)

    @pl.when(pl.program_id(2) == 0)
    def _():
        st_ref[...] = jnp.zeros_like(st_ref)

    lbl = lbl_ref[...]
    rows = [lbl[i:i + 1, :] for i in range(lbl.shape[0])]
    mx = functools.reduce(jnp.maximum, rows)
    es = [jnp.exp(r - mx) for r in rows]
    lb = functools.reduce(lambda a, b: a + b, es[:layer + 1]) / functools.reduce(lambda a, b: a + b, es)

    q = _silu(q_ref[...].astype(F32))
    forget = lb + (1.0 - lb) * _sigmoid(f_ref[...].astype(F32))
    k = 1.0 - forget
    logf = jnp.log2(forget)
    v = v_ref[...]

    hml = jnp.concatenate(_split_bf16(logf, 3), axis=1)
    bb = _dot(tri_ref[...], hml)
    b = bb[:, 0:LANES] + bb[:, LANES:2 * LANES] + bb[:, 2 * LANES:3 * LANES]
    b_ref[...] = b

    def b_row(r, n):
        return jnp.broadcast_to(b_ref[pl.ds(r, 1), :], (n, HG_HEAD_DIM))

    ends = b_ref[pl.ds(HG_BASE - 1, c // HG_BASE, stride=HG_BASE), :]
    blk_i = lax.broadcasted_iota(I32, ends.shape, 0)
    starts = jnp.where(blk_i == 0, 0.0, pltpu.roll(ends, 1, axis=0))
    base_ok = jnp.max(starts - ends) <= HG_BASE_MAX_DECAY

    nbig = nlev - base_lv
    qs_l, ks_l, vs_l = [], [], []
    for lv in range(base_lv, nlev):
        m = 1 << lv
        n = 2 * m
        qs, ks, vs = [], [], []
        for a in range(c // n):
            mid = b_row(a * n + m - 1, m)
            qs.append(q[a * n + m:(a + 1) * n] * jnp.exp2(b[a * n + m:(a + 1) * n] - mid))
            ks.append(k[a * n:a * n + m] * jnp.exp2(mid - b[a * n:a * n + m]))
            vs.append(v[a * n:a * n + m])
        qs_l.append(jnp.concatenate(qs, axis=0).astype(BF16))
        ks_l.append(jnp.concatenate(ks, axis=0).astype(BF16))
        vs_l.append(jnp.concatenate(vs, axis=0))
    start = jnp.concatenate(
        [jnp.zeros((HG_BASE, HG_HEAD_DIM), F32)]
        + [b_row(j * HG_BASE - 1, HG_BASE) for j in range(1, c // HG_BASE)], axis=0)
    dlt = jnp.maximum(b - start, -HG_BASE_MAX_DECAY)
    qb16 = (q * jnp.exp2(dlt)).astype(BF16)
    kb16 = (k * jnp.exp2(-dlt)).astype(BF16)
    b_last = b_row(c - 1, c)
    qe = (q * jnp.exp2(b)).astype(BF16)
    kd = (k * jnp.exp2(b_last - b)).astype(BF16)
    st = st_ref[...]

    a_l = [_dot_nt(qs_l[i], ks_l[i]) for i in range(nbig)]
    a16 = _dot_nt(qb16, kb16)
    o_inter = _dot_nt(qe, st.astype(BF16))
    st_ref[...] = st * jnp.exp2(b_last[0:1, :]) + _dot_tn(v, kd)

    a_l = [(a_l[i] * cmask_ref[i] if i < nbig - 1 else a_l[i]).astype(BF16) for i in range(nbig)]
    a16 = (a16 * bmask_ref[...]).astype(BF16)
    o_l = [_dot(a_l[i], vs_l[i]) for i in range(nbig)]
    oi_ref[...] = _dot(a16, v)

    pieces = [None] * (c // HG_BASE)
    for i in range(nbig):
        m = HG_BASE << i
        per = m // HG_BASE
        for a in range(c // (2 * m)):
            for u in range(per):
                dst = (a * 2 * m + m) // HG_BASE + u
                src = o_l[i][(a * per + u) * HG_BASE:(a * per + u + 1) * HG_BASE]
                pieces[dst] = src if pieces[dst] is None else pieces[dst] + src
    zero_slab = jnp.zeros((HG_BASE, HG_HEAD_DIM), F32)
    rest_ref[...] = o_inter + jnp.concatenate([zero_slab if p is None else p for p in pieces], axis=0)

    @pl.when(jnp.logical_not(base_ok))
    def _():
        rowi = lax.broadcasted_iota(I32, (c, HG_HEAD_DIM), 0)
        sub = lax.broadcasted_iota(I32, (SUBLANES, HG_HEAD_DIM), 0)
        ntile = c // SUBLANES
        attn = _dot_nt(q.astype(BF16), k.astype(BF16)) * mask_ref[0]
        for lv in range(base_lv):
            m = 1 << lv
            isq = (rowi & m) != 0
            if m == 1:
                e = jnp.where(isq, forget, 1.0)
            else:
                if m >= SUBLANES:
                    tiles = [b_row((j * SUBLANES // (2 * m)) * 2 * m + m - 1, SUBLANES) for j in range(ntile)]
                elif m == 4:
                    tiles = [b_row(j * SUBLANES + 3, SUBLANES) for j in range(ntile)]
                else:
                    tiles = [jnp.where(sub < 4, b_row(j * SUBLANES + 1, SUBLANES), b_row(j * SUBLANES + 5, SUBLANES))
                             for j in range(ntile)]
                mid = jnp.concatenate(tiles, axis=0)
                e = jnp.exp2(jnp.where(isq, b - mid, mid - b))
            qt = jnp.where(isq, q * e, 0.0).astype(BF16)
            kt = jnp.where(isq, 0.0, k * e).astype(BF16)
            attn = attn + _dot_nt(qt, kt) * mask_ref[lv + 1]
        oi_ref[...] = _dot(attn.astype(BF16), v)

    o = oi_ref[...] + rest_ref[...]

    o = o * lax.rsqrt(jnp.mean(o * o, axis=-1, keepdims=True) + RMS_EPS)
    o = o * gain_ref[...] * _silu(g_ref[...].astype(F32))
    o_ref[...] = o.astype(o_ref.dtype)


def _hgrn_mixer(proj, lb_logits, head_gain, layer, j, bsz, seq):
    t, d4 = proj.shape
    d = d4 // 4
    nh = d // HG_HEAD_DIM
    chunk = 256 if seq % 256 == 0 else 128
    nc = seq // chunk
    nlev = int(math.log2(chunk))
    base_lv = int(math.log2(HG_BASE))

    def col(part):
        return lambda b, h, c: (b * nc + c, part * nh + h)

    return pl.pallas_call(
        functools.partial(_hgrn_kernel, layer=layer, chunk=chunk),
        grid=(bsz, nh, nc),
        in_specs=[
            pl.BlockSpec((chunk, HG_HEAD_DIM), col(0)),
            pl.BlockSpec((chunk, HG_HEAD_DIM), col(1)),
            pl.BlockSpec((chunk, HG_HEAD_DIM), col(2)),
            pl.BlockSpec((chunk, HG_HEAD_DIM), col(3)),
            pl.BlockSpec((lb_logits.shape[0], HG_HEAD_DIM), lambda b, h, c: (0, h)),
            pl.BlockSpec((None, 1, HG_HEAD_DIM), lambda b, h, c: (j, 0, h)),
        ],
        out_specs=pl.BlockSpec((chunk, HG_HEAD_DIM), lambda b, h, c: (b * nc + c, h)),
        out_shape=jax.ShapeDtypeStruct((t, d), BF16),
        scratch_shapes=[
            pltpu.VMEM((HG_HEAD_DIM, HG_HEAD_DIM), F32),
            pltpu.VMEM((chunk, HG_HEAD_DIM), F32),
            pltpu.VMEM((chunk, HG_HEAD_DIM), F32),
            pltpu.VMEM((chunk, HG_HEAD_DIM), F32),
            pltpu.VMEM((chunk, chunk), BF16),
            pltpu.VMEM((base_lv + 1, chunk, chunk), F32),
            pltpu.VMEM((chunk, chunk), F32),
            pltpu.VMEM((nlev - 1 - base_lv, chunk // 2, chunk // 2), F32),
        ],
        compiler_params=_cparams(3),
        name="hgrn2_mixer",
    )(proj, proj, proj, proj, lb_logits.astype(F32), head_gain.astype(F32).reshape(head_gain.shape[0], 1, d))


def _t5_bucket(dist):
    max_exact = N_BUCKETS // 2
    n = jnp.maximum(dist, 0)
    large = max_exact + (jnp.log(jnp.maximum(n, 1).astype(F32) / max_exact)
                         / math.log(MAX_DISTANCE / max_exact)
                         * (N_BUCKETS - max_exact)).astype(I32)
    large = jnp.minimum(large, N_BUCKETS - 1)
    return jnp.where(n < max_exact, n, large)


def _attn_kernel(rb_ref, sink_ref, q_ref, kp_ref, kc_ref, vp_ref, vc_ref, bucket_ref, o_ref,
                 bias_ref, *, group, layer_j):
    blk = ATT_BLOCK
    hd = ATT_HEAD_DIM
    pr = pl.program_id(1)
    i = pl.program_id(2)
    heads_per_step = 2 * group

    @pl.when(i == 0)
    def _():
        qi = lax.broadcasted_iota(I32, (blk, 2 * blk), 0)
        sj = lax.broadcasted_iota(I32, (blk, 2 * blk), 1)
        dist = qi + blk - sj
        band = (dist >= 0) & (dist < WINDOW)
        bucket = bucket_ref[...]

        def per_head(hh, carry):
            h = pr * heads_per_step + hh
            tbl = jnp.zeros((blk, 2 * blk), F32)
            for bk in range(N_BUCKETS):
                tbl = jnp.where(bucket == bk, rb_ref[bk, h], tbl)
            bias_ref[hh] = jnp.where(band, tbl, NEG_INF)
            bias_ref[heads_per_step + hh] = jnp.where(band & (sj >= blk), tbl, NEG_INF)
            return carry

        lax.fori_loop(0, heads_per_step, per_head, 0)

    lane = lax.broadcasted_iota(I32, (2 * blk, LANES), 1)
    table0 = jnp.where(i == 0, heads_per_step, 0)

    kk = jnp.concatenate([kp_ref[...], kc_ref[...]], axis=0).astype(F32)
    vv = jnp.concatenate([vp_ref[...], vc_ref[...]], axis=0).astype(F32)
    q_all = q_ref[...] * (hd ** -0.5)

    kbds, vbds = [], []
    for c in range(2):
        if c == 0:
            klo = jnp.where(lane < hd, kk, 0.0)
            khi = pltpu.roll(klo, hd, axis=1)
            vlo = jnp.where(lane < hd, vv, 0.0)
            vhi = pltpu.roll(vlo, hd, axis=1)
        else:
            khi = jnp.where(lane >= hd, kk, 0.0)
            klo = pltpu.roll(khi, hd, axis=1)
            vhi = jnp.where(lane >= hd, vv, 0.0)
            vlo = pltpu.roll(vhi, hd, axis=1)
        kbds.append(jnp.concatenate([klo, khi], axis=0).astype(BF16))
        vbds.append(jnp.concatenate([vlo, vhi], axis=0).astype(BF16))

    half = group // 2
    tiles = [(c, p) for c in range(2) for p in range(half)]
    lgs = [_dot_nt(q_all[:, (c * half + p) * LANES:(c * half + p + 1) * LANES], kbds[c]) for c, p in tiles]
    p2s, rinvs = [], []
    for (c, p), lg in zip(tiles, lgs):
        probs, rinv = [], []
        for hh in range(2):
            hl = c * group + 2 * p + hh
            sk = sink_ref[layer_j, pr * heads_per_step + hl]
            l = lg[:, hh * 2 * blk:(hh + 1) * 2 * blk] + bias_ref[table0 + hl]
            mx = jnp.maximum(jnp.max(l, axis=-1, keepdims=True), sk)
            pe = jnp.exp(l - mx)
            rinv.append(1.0 / (jnp.sum(pe, axis=-1, keepdims=True) + jnp.exp(sk - mx)))
            probs.append(pe.astype(BF16))
        p2s.append(jnp.concatenate(probs, axis=1))
        rinvs.append(rinv)
    lane_o = lax.broadcasted_iota(I32, (blk, LANES), 1)
    for (c, p), p2, rinv in zip(tiles, p2s, rinvs):
        tile = c * half + p
        o = _dot(p2, vbds[c]) * jnp.where(lane_o < hd, rinv[0], rinv[1])
        o_ref[:, tile * LANES:(tile + 1) * LANES] = o.astype(o_ref.dtype)


def _attn_mixer(proj, sinks, rel_bias, layer_j, bsz, seq, d):
    t, att_in = proj.shape
    kvw = (att_in - d) // 2
    n_heads = d // ATT_HEAD_DIM
    kvh = kvw // ATT_HEAD_DIM
    group = n_heads // kvh
    assert kvh % 2 == 0 and group % 2 == 0
    blk = ATT_BLOCK
    nb = seq // blk
    npair = kvh // 2
    qw = 2 * group * ATT_HEAD_DIM
    k0 = d // LANES
    v0 = (d + kvw) // LANES

    qi = jnp.arange(blk)[:, None]
    sj = jnp.arange(2 * blk)[None, :]
    bucket = _t5_bucket(qi + blk - sj).astype(I32)

    def prev(i):
        return jnp.maximum(i - 1, 0)

    grid_spec = pltpu.PrefetchScalarGridSpec(
        num_scalar_prefetch=2,
        grid=(bsz, npair, nb),
        in_specs=[
            pl.BlockSpec((blk, qw), lambda b, p, i, *_: (b * nb + i, p)),
            pl.BlockSpec((blk, LANES), lambda b, p, i, *_: (b * nb + prev(i), k0 + p)),
            pl.BlockSpec((blk, LANES), lambda b, p, i, *_: (b * nb + i, k0 + p)),
            pl.BlockSpec((blk, LANES), lambda b, p, i, *_: (b * nb + prev(i), v0 + p)),
            pl.BlockSpec((blk, LANES), lambda b, p, i, *_: (b * nb + i, v0 + p)),
            pl.BlockSpec((blk, 2 * blk), lambda b, p, i, *_: (0, 0)),
        ],
        out_specs=pl.BlockSpec((blk, qw), lambda b, p, i, *_: (b * nb + i, p)),
        scratch_shapes=[pltpu.VMEM((4 * group, blk, 2 * blk), F32)],
    )
    return pl.pallas_call(
        functools.partial(_attn_kernel, group=group, layer_j=layer_j),
        grid_spec=grid_spec,
        out_shape=jax.ShapeDtypeStruct((t, d), BF16),
        compiler_params=_cparams(3),
        name="swa_sink_mixer",
    )(rel_bias.astype(F32), sinks.astype(F32), proj, proj, proj, proj, proj, bucket)


def _layer_norm_rows(z, g, b):
    mu = jnp.mean(z, axis=-1, keepdims=True)
    zc = z - mu
    var = jnp.mean(zc * zc, axis=-1, keepdims=True)
    return zc * lax.rsqrt(var + LN_EPS) * g + b


def _ln_router_kernel(x_ref, y_ref, gate_ref, sc_ref, sh_ref, lng_ref, lnb_ref, wr_ref, br_ref,
                      xo_ref, hf_ref, meta_ref, cnt_ref, carry_ref, wsplit_ref, *, alpha, n_groups, n_experts):
    i = pl.program_id(0)
    bm = x_ref.shape[0]
    epg = n_experts // n_groups

    @pl.when(i == 0)
    def _():
        carry_ref[...] = jnp.zeros_like(carry_ref)
        wh, wl = _split_bf16(wr_ref[...], 2)
        wsplit_ref[:, 0:LANES] = wh
        wsplit_ref[:, LANES:2 * LANES] = wl

    z = alpha * x_ref[...] + (1.0 + gate_ref[...]) * y_ref[...]
    xn = _layer_norm_rows(z, lng_ref[...], lnb_ref[...])
    xo_ref[...] = xn
    hf = xn * (1.0 + sc_ref[...]) + sh_ref[...]
    hf_ref[...] = _pack_bf16_pairs(hf)

    xh, xl = _split_bf16(hf, 2)
    hh = _dot(xh, wsplit_ref[...])
    lg = hh[:, 0:LANES] + hh[:, LANES:2 * LANES] + _dot(xl, wsplit_ref[:, 0:LANES]) + br_ref[...]

    lane = lax.broadcasted_iota(I32, (bm, LANES), 1)
    lanef = lane.astype(F32)
    big = float(LANES)

    gl = jnp.where(lane < n_groups, lg, NEG_INF)
    gmax = jnp.max(gl, axis=-1, keepdims=True)
    gsel = jnp.min(jnp.where(gl == gmax, lanef, big), axis=-1, keepdims=True)
    p_group = 1.0 / jnp.sum(jnp.exp(gl - gmax), axis=-1, keepdims=True)

    lo = n_groups + gsel * epg
    el = jnp.where((lanef >= lo) & (lanef < lo + epg), lg, NEG_INF)
    m1 = jnp.max(el, axis=-1, keepdims=True)
    i1 = jnp.min(jnp.where(el == m1, lanef, big), axis=-1, keepdims=True)
    el2 = jnp.where(lanef == i1, NEG_INF, el)
    m2 = jnp.max(el2, axis=-1, keepdims=True)
    i2 = jnp.min(jnp.where(el2 == m2, lanef, big), axis=-1, keepdims=True)
    e21 = jnp.exp(m2 - m1)
    g0 = p_group / (1.0 + e21)
    g1 = g0 * e21

    oh0 = lanef == i1
    oh1 = lanef == i2
    cnt = jnp.where(oh0 | oh1, 1.0, 0.0)
    row = lax.broadcasted_iota(I32, (bm, bm), 0)
    col = lax.broadcasted_iota(I32, (bm, bm), 1)
    stril = jnp.where(row > col, 1.0, 0.0).astype(BF16)
    before = _dot(stril, cnt.astype(BF16)) + carry_ref[...]
    r0 = jnp.sum(jnp.where(oh0, before, 0.0), axis=-1, keepdims=True)
    r1 = jnp.sum(jnp.where(oh1, before, 0.0), axis=-1, keepdims=True)
    carry_ref[...] = carry_ref[...] + jnp.sum(cnt, axis=0, keepdims=True)
    cnt_ref[...] = carry_ref[...]

    meta = jnp.where(lane == 0, i1 - n_groups, 0.0)
    meta = jnp.where(lane == 1, i2 - n_groups, meta)
    meta = jnp.where(lane == 2, g0, meta)
    meta = jnp.where(lane == 3, g1, meta)
    meta = jnp.where(lane == 4, r0, meta)
    meta = jnp.where(lane == 5, r1, meta)
    meta_ref[...] = meta


def _ln_router(x2, y2, mod, row_of, ln_g, ln_b, w_router, b_router, alpha, n_groups, n_experts, seq):
    t, d = x2.shape
    bm = 128
    nbs = seq // bm

    def mrow(which):
        return lambda i: (row_of(i // nbs, which), 0, 0)

    return pl.pallas_call(
        functools.partial(_ln_router_kernel, alpha=alpha, n_groups=n_groups, n_experts=n_experts),
        grid=(t // bm,),
        in_specs=[
            pl.BlockSpec((bm, d), lambda i: (i, 0)),
            pl.BlockSpec((bm, d), lambda i: (i, 0)),
            pl.BlockSpec((None, 1, d), mrow(2)),
            pl.BlockSpec((None, 1, d), mrow(4)),
            pl.BlockSpec((None, 1, d), mrow(3)),
            pl.BlockSpec((1, d), lambda i: (0, 0)),
            pl.BlockSpec((1, d), lambda i: (0, 0)),
            pl.BlockSpec((d, LANES), lambda i: (0, 0)),
            pl.BlockSpec((1, LANES), lambda i: (0, 0)),
        ],
        out_specs=[
            pl.BlockSpec((bm, d), lambda i: (i, 0)),
            pl.BlockSpec((bm, d // 2), lambda i: (i, 0)),
            pl.BlockSpec((bm, LANES), lambda i: (i, 0)),
            pl.BlockSpec((1, LANES), lambda i: (0, 0)),
        ],
        out_shape=[
            jax.ShapeDtypeStruct((t, d), F32),
            jax.ShapeDtypeStruct((t, d // 2), jnp.uint32),
            jax.ShapeDtypeStruct((t, LANES), F32),
            jax.ShapeDtypeStruct((1, LANES), F32),
        ],
        scratch_shapes=[pltpu.VMEM((1, LANES), F32), pltpu.VMEM((d, 2 * LANES), BF16)],
        compiler_params=_cparams(1),
        name="ln_router",
    )(x2, y2, mod, mod, mod, ln_g, ln_b, w_router, b_router)


def _cast_rows(src_ref, dst_ref, rows=128):
    def body(r, carry):
        sl = pl.ds(pl.multiple_of(r * rows, rows), rows)
        dst_ref[sl, :] = src_ref[sl, :].astype(dst_ref.dtype)
        return carry
    lax.fori_loop(0, src_ref.shape[0] // rows, body, 0)


def _moe_kernel(blk_e_ref, nxt_e_ref, first_ref, nused_ref,
                tokc_ref, tokn_ref, hf_hbm, wgu_hbm, wdn_hbm, y_ref,
                xbuf, wgu_st, wdn_st, wgu_bf, wdn_bf, gsem, wsem, *, layer, d_expert):
    i = pl.program_id(0)
    bm = xbuf.shape[1]
    nused = nused_ref[0]
    slot = i % 2

    def row_copy(tok, r, s):
        return pltpu.make_async_copy(hf_hbm.at[pl.ds(tok, 1), :], xbuf.at[s, pl.ds(r, 1), :], gsem.at[s])

    def issue_rows(tok_ref, s):
        def body(r, carry):
            row_copy(tok_ref[0, r], r, s).start()
            return carry
        lax.fori_loop(0, bm, body, 0, unroll=8)

    def wait_rows(s):
        pltpu.make_async_copy(hf_hbm.at[pl.ds(0, bm), :], xbuf.at[s], gsem.at[s]).wait()

    def weight_copies(e):
        return (pltpu.make_async_copy(wgu_hbm.at[layer, e], wgu_st, wsem.at[0]),
                pltpu.make_async_copy(wdn_hbm.at[layer, e], wdn_st, wsem.at[1]))

    @pl.when(i == 0)
    def _():
        issue_rows(tokc_ref, 0)
        for cp in weight_copies(blk_e_ref[0]):
            cp.start()

    @pl.when((i < nused) & (first_ref[i] == 1))
    def _():
        for cp in weight_copies(blk_e_ref[i]):
            cp.wait()
        _cast_rows(wgu_st, wgu_bf)
        _cast_rows(wdn_st, wdn_bf)

        @pl.when(nxt_e_ref[i] >= 0)
        def _():
            for cp in weight_copies(nxt_e_ref[i]):
                cp.start()

    @pl.when(i + 1 < nused)
    def _():
        issue_rows(tokn_ref, 1 - slot)

    @pl.when(i < nused)
    def _():
        wait_rows(slot)
        xb = _unpack_bf16_pairs(xbuf[slot]).astype(BF16)
        a = _dot(xb, wgu_bf[...])
        h = _silu(a[:, :d_expert]) * a[:, d_expert:]
        y_ref[...] = _pack_bf16_pairs(_dot(h.astype(BF16), wdn_bf[...]))

    @pl.when(i >= nused)
    def _():
        y_ref[...] = jnp.zeros_like(y_ref)


def _moe_experts(hf, tok_pad, blk_e, nxt_e, first, nused, w_gate_up, w_down, layer, bm):
    d = w_down.shape[3]
    nb = blk_e.shape[0]
    d_expert = w_down.shape[2]
    tok3 = tok_pad.reshape(nb, 1, bm)
    grid_spec = pltpu.PrefetchScalarGridSpec(
        num_scalar_prefetch=4,
        grid=(nb,),
        in_specs=[
            pl.BlockSpec((None, 1, bm), lambda i, *_: (i, 0, 0), memory_space=pltpu.SMEM),
            pl.BlockSpec((None, 1, bm), lambda i, *_: (jnp.minimum(i + 1, nb - 1), 0, 0), memory_space=pltpu.SMEM),
            pl.BlockSpec(memory_space=pl.ANY),
            pl.BlockSpec(memory_space=pl.ANY),
            pl.BlockSpec(memory_space=pl.ANY),
        ],
        out_specs=pl.BlockSpec((bm, d // 2), lambda i, *_: (i, 0)),
        scratch_shapes=[
            pltpu.VMEM((2, bm, d // 2), jnp.uint32),
            pltpu.VMEM((d, 2 * d_expert), F32),
            pltpu.VMEM((d_expert, d), F32),
            pltpu.VMEM((d, 2 * d_expert), BF16),
            pltpu.VMEM((d_expert, d), BF16),
            pltpu.SemaphoreType.DMA((2,)),
            pltpu.SemaphoreType.DMA((2,)),
        ],
    )
    return pl.pallas_call(
        functools.partial(_moe_kernel, layer=layer, d_expert=d_expert),
        grid_spec=grid_spec,
        out_shape=jax.ShapeDtypeStruct((nb * bm, d // 2), jnp.uint32),
        compiler_params=_cparams(1),
        name="moe_experts",
    )(blk_e, nxt_e, first, nused, tok3, tok3, hf, w_gate_up, w_down)


def _moe_plan(meta, counts_row, n_groups, n_experts, bm):
    t = meta.shape[0]
    eid = meta[:, 0:2].astype(I32)
    rank = meta[:, 4:6].astype(I32)
    counts = counts_row[0, n_groups:n_groups + n_experts].astype(I32)
    padded = (counts + bm - 1) // bm * bm
    pad_end = jnp.cumsum(padded)
    pad_start = pad_end - padded
    dest = pad_start[eid] + rank
    nb = (2 * t) // bm + n_experts
    nused = pad_end[-1] // bm
    ids = jnp.arange(nb, dtype=I32)
    raw_e = jnp.minimum(jnp.sum((pad_end[None, :] <= (ids * bm)[:, None]).astype(I32), axis=1), n_experts - 1)
    used = ids < nused
    blk_e = jnp.where(used, raw_e, raw_e[nused - 1])
    prev_e = jnp.concatenate([jnp.full((1,), -1, I32), blk_e[:-1]])
    first = (used & (blk_e != prev_e)).astype(I32)
    key = jnp.where(used, blk_e, n_experts)
    nxt_idx = jnp.sum((key[None, :] <= blk_e[:, None]).astype(I32), axis=1)
    nxt_e = jnp.where(nxt_idx < nused, key[jnp.minimum(nxt_idx, nb - 1)], -1).astype(I32)
    tok = jnp.repeat(jnp.arange(t, dtype=I32), 2)
    tok_pad = jnp.zeros((nb * bm,), I32).at[dest.reshape(-1)].set(tok)
    return dest, tok_pad, blk_e, nxt_e, first, nused.reshape(1).astype(I32)


def _ln_combine_kernel(*refs, alpha, with_next):
    if with_next:
        (dc_ref, dn_ref, x_ref, meta_ref, gate_ref, lng_ref, lnb_ref, sc_ref, sh_ref, y_hbm,
         xo_ref, hm_ref, ybuf, sem) = refs
    else:
        (dc_ref, dn_ref, x_ref, meta_ref, gate_ref, lng_ref, lnb_ref, y_hbm,
         xo_ref, ybuf, sem) = refs
    i = pl.program_id(0)
    n = pl.num_programs(0)
    bm = x_ref.shape[0]
    slot = i % 2

    def row_copy(src, r, s):
        return pltpu.make_async_copy(y_hbm.at[pl.ds(src, 1), :], ybuf.at[s, pl.ds(r, 1), :], sem.at[s])

    def issue_rows(d_ref, s):
        def body(r, carry):
            row_copy(d_ref[0, r], r, s).start()
            return carry
        lax.fori_loop(0, 2 * bm, body, 0, unroll=8)

    def wait_rows(s):
        pltpu.make_async_copy(y_hbm.at[pl.ds(0, 2 * bm), :], ybuf.at[s], sem.at[s]).wait()

    @pl.when(i == 0)
    def _():
        issue_rows(dc_ref, 0)

    @pl.when(i + 1 < n)
    def _():
        issue_rows(dn_ref, 1 - slot)

    wait_rows(slot)
    meta = meta_ref[...]
    y = (_unpack_bf16_pairs(ybuf[slot, 0:bm, :]) * meta[:, 2:3]
         + _unpack_bf16_pairs(ybuf[slot, bm:2 * bm, :]) * meta[:, 3:4])
    z = alpha * x_ref[...] + (1.0 + gate_ref[...]) * y
    xn = _layer_norm_rows(z, lng_ref[...], lnb_ref[...])
    xo_ref[...] = xn
    if with_next:
        hm_ref[...] = (xn * (1.0 + sc_ref[...]) + sh_ref[...]).astype(hm_ref.dtype)


def _ln_combine(x2, meta, dest, ysort, mod, row_of, next_row_of, ln_g, ln_b, alpha, seq):
    t, d = x2.shape
    bm = 128
    nbs = seq // bm
    nblk = t // bm
    with_next = next_row_of is not None
    dest3 = dest.reshape(nblk, bm, 2).transpose(0, 2, 1).reshape(nblk, 1, 2 * bm)

    def mrow(fn, which):
        return lambda i: (fn(i // nbs, which), 0, 0)

    in_specs = [
        pl.BlockSpec((None, 1, 2 * bm), lambda i: (i, 0, 0), memory_space=pltpu.SMEM),
        pl.BlockSpec((None, 1, 2 * bm), lambda i: (jnp.minimum(i + 1, nblk - 1), 0, 0), memory_space=pltpu.SMEM),
        pl.BlockSpec((bm, d), lambda i: (i, 0)),
        pl.BlockSpec((bm, LANES), lambda i: (i, 0)),
        pl.BlockSpec((None, 1, d), mrow(row_of, 5)),
        pl.BlockSpec((1, d), lambda i: (0, 0)),
        pl.BlockSpec((1, d), lambda i: (0, 0)),
    ]
    args = [dest3, dest3, x2, meta, mod, ln_g, ln_b]
    out_specs = [pl.BlockSpec((bm, d), lambda i: (i, 0))]
    out_shape = [jax.ShapeDtypeStruct((t, d), F32)]
    if with_next:
        in_specs += [pl.BlockSpec((None, 1, d), mrow(next_row_of, 1)),
                     pl.BlockSpec((None, 1, d), mrow(next_row_of, 0))]
        args += [mod, mod]
        out_specs.append(pl.BlockSpec((bm, d), lambda i: (i, 0)))
        out_shape.append(jax.ShapeDtypeStruct((t, d), BF16))
    in_specs.append(pl.BlockSpec(memory_space=pl.ANY))
    args.append(ysort)
    return pl.pallas_call(
        functools.partial(_ln_combine_kernel, alpha=alpha, with_next=with_next),
        grid=(nblk,),
        in_specs=in_specs,
        out_specs=out_specs,
        out_shape=out_shape,
        scratch_shapes=[pltpu.VMEM((2, 2 * bm, d // 2), jnp.uint32), pltpu.SemaphoreType.DMA((2,))],
        compiler_params=_cparams(1),
        name="ln_moe_combine",
    )(*args)


def kernel(x, c, w_ada, b_ada, ln_g, ln_b, w_in_a, lb_logits, head_gain_a, w_out_a, w_in_b, attn_sinks, w_out_b, rel_bias, w_router_group, b_router_group, w_router_expert, b_router_expert, w_gate_up, w_down):
    bsz, seq, d = x.shape
    depth = w_ada.shape[0]
    n_groups = w_router_group.shape[2]
    n_experts = w_router_expert.shape[2]
    alpha = (2 * depth) ** 0.25
    t = bsz * seq
    moe_bm = 128

    mod = _ada_modulation(c, w_ada, b_ada)

    def row_of_layer(layer):
        return lambda b, which: (layer * bsz + b) * 6 + which

    x2 = x.reshape(t, d).astype(F32)
    row0 = row_of_layer(0)
    hm = _modulate(x2, mod, lambda b: row0(b, 1), lambda b: row0(b, 0), bsz, seq)

    for layer in range(depth):
        row_of = row_of_layer(layer)
        j = layer // 2
        if layer % 2 == 0:
            proj = _matmul(hm, w_in_a, j, BF16)
            o = _hgrn_mixer(proj, lb_logits, head_gain_a, layer, j, bsz, seq)
            y = _matmul(o, w_out_a, j, F32)
        else:
            proj = _matmul(hm, w_in_b, j, BF16)
            o = _attn_mixer(proj, attn_sinks, rel_bias, j, bsz, seq, d)
            y = _matmul(o, w_out_b, j, F32)

        n_pad = LANES - n_groups - n_experts
        w_router = jnp.concatenate(
            [w_router_group[layer].astype(F32), w_router_expert[layer].astype(F32), jnp.zeros((d, n_pad), F32)], axis=1)
        b_router = jnp.concatenate(
            [b_router_group[layer].astype(F32), b_router_expert[layer].astype(F32), jnp.zeros((n_pad,), F32)]
        ).reshape(1, LANES)

        x2, hf, meta, counts = _ln_router(
            x2, y, mod, row_of, ln_g[layer, 0:1].astype(F32), ln_b[layer, 0:1].astype(F32),
            w_router, b_router, alpha, n_groups, n_experts, seq)
        dest, tok_pad, blk_e, nxt_e, first, nused = _moe_plan(meta, counts, n_groups, n_experts, moe_bm)
        ysort = _moe_experts(hf, tok_pad, blk_e, nxt_e, first, nused, w_gate_up, w_down, layer, moe_bm)
        next_row_of = row_of_layer(layer + 1) if layer + 1 < depth else None
        outs = _ln_combine(x2, meta, dest, ysort, mod, row_of, next_row_of,
                           ln_g[layer, 1:2].astype(F32), ln_b[layer, 1:2].astype(F32), alpha, seq)
        x2 = outs[0]
        if next_row_of is not None:
            hm = outs[1]

    return x2.reshape(bsz, seq, d).astype(x.dtype)
```

```python
import functools
import math

import jax
import jax.numpy as jnp
from jax import lax
from jax.experimental import pallas as pl
from jax.experimental.pallas import tpu as pltpu

F32 = jnp.float32
BF16 = jnp.bfloat16
I32 = jnp.int32

LANES = 128
SUBLANES = 8
V7X_VMEM_LIMIT_BYTES = 56 * 1024 * 1024

HG_HEAD_DIM = 128
ATT_HEAD_DIM = 64
ATT_BLOCK = 128
WINDOW = 128
N_BUCKETS = 32
MAX_DISTANCE = 128
LN_EPS = 1e-5
RMS_EPS = 1e-6
NEG_INF = float("-inf")


def _cparams(n_axes):
    return pltpu.CompilerParams(
        dimension_semantics=("arbitrary",) * n_axes,
        vmem_limit_bytes=V7X_VMEM_LIMIT_BYTES,
    )


def _sigmoid(x):
    return 0.5 * jnp.tanh(0.5 * x) + 0.5


def _silu(x):
    return x * _sigmoid(x)


def _dot_nt(a, b):
    return lax.dot_general(a, b, (((1,), (1,)), ((), ())), preferred_element_type=F32)


def _dot_tn(a, b):
    return lax.dot_general(a, b, (((0,), (0,)), ((), ())), preferred_element_type=F32)


def _dot(a, b):
    return jnp.dot(a, b, preferred_element_type=F32)


def _pack_bf16_pairs(x):
    n = x.shape[1] // 2
    bits = pltpu.bitcast(x.astype(BF16).astype(F32), jnp.uint32)
    return (bits[:, :n] >> 16) | (bits[:, n:] & jnp.uint32(0xFFFF0000))


def _unpack_bf16_pairs(w):
    lo = pltpu.bitcast(w << 16, F32)
    hi = pltpu.bitcast(w & jnp.uint32(0xFFFF0000), F32)
    return jnp.concatenate([lo, hi], axis=1)


def _split_bf16(x, parts):
    out = []
    r = x
    for _ in range(parts):
        h = r.astype(BF16)
        out.append(h)
        r = r - h.astype(F32)
    return out


def _ada_kernel(c_ref, w_ref, b_ref, o_ref):
    ca = _silu(c_ref[...]).astype(BF16)
    o_ref[...] = _dot(ca, w_ref[...].astype(BF16)) + b_ref[...]


def _ada_modulation(c, w_ada, b_ada):
    nl, d, n6 = w_ada.shape
    bsz = c.shape[0]
    rows = -(-bsz // SUBLANES) * SUBLANES
    c8 = jnp.zeros((rows, d), F32).at[:bsz].set(c.astype(F32))
    tn = 512
    out = pl.pallas_call(
        _ada_kernel,
        grid=(nl, n6 // tn),
        in_specs=[
            pl.BlockSpec((rows, d), lambda l, j: (0, 0)),
            pl.BlockSpec((None, d, tn), lambda l, j: (l, 0, j)),
            pl.BlockSpec((None, 1, tn), lambda l, j: (l, 0, j)),
        ],
        out_specs=pl.BlockSpec((None, rows, tn), lambda l, j: (l, 0, j)),
        out_shape=jax.ShapeDtypeStruct((nl, rows, n6), F32),
        compiler_params=_cparams(2),
        name="ada_modulation",
    )(c8, w_ada, b_ada.reshape(nl, 1, n6))
    return out[:, :bsz].reshape(nl * bsz * 6, 1, d)


def _modulate_kernel(x_ref, sc_ref, sh_ref, o_ref):
    o_ref[...] = (x_ref[...] * (1.0 + sc_ref[...]) + sh_ref[...]).astype(o_ref.dtype)


def _modulate(x2, mod, sc_row, sh_row, bsz, seq):
    t, d = x2.shape
    bs = min(512, seq)
    nbs = seq // bs
    return pl.pallas_call(
        _modulate_kernel,
        grid=(t // bs,),
        in_specs=[
            pl.BlockSpec((bs, d), lambda i: (i, 0)),
            pl.BlockSpec((None, 1, d), lambda i: (sc_row(i // nbs), 0, 0)),
            pl.BlockSpec((None, 1, d), lambda i: (sh_row(i // nbs), 0, 0)),
        ],
        out_specs=pl.BlockSpec((bs, d), lambda i: (i, 0)),
        out_shape=jax.ShapeDtypeStruct((t, d), BF16),
        compiler_params=_cparams(1),
        name="modulate",
    )(x2, mod, mod)


def _matmul_kernel(x_ref, w_ref, o_ref, wbf_ref):
    @pl.when(pl.program_id(1) == 0)
    def _():
        wbf_ref[...] = w_ref[...].astype(BF16)

    o_ref[...] = _dot(x_ref[...], wbf_ref[...]).astype(o_ref.dtype)


def _matmul(x, w3, layer, out_dtype):
    m, k = x.shape
    n = w3.shape[2]
    bm = min(1024, m)
    bn = 512
    return pl.pallas_call(
        _matmul_kernel,
        grid=(n // bn, m // bm),
        in_specs=[
            pl.BlockSpec((bm, k), lambda j, i: (i, 0)),
            pl.BlockSpec((None, k, bn), lambda j, i: (layer, 0, j)),
        ],
        out_specs=pl.BlockSpec((bm, bn), lambda j, i: (i, j)),
        out_shape=jax.ShapeDtypeStruct((m, n), out_dtype),
        scratch_shapes=[pltpu.VMEM((k, bn), BF16)],
        compiler_params=_cparams(2),
        name="dense_projection",
    )(x, w3)


HG_BASE = 16
HG_BASE_MAX_DECAY = 86.0


def _hgrn_kernel(q_ref, f_ref, v_ref, g_ref, lbl_ref, gain_ref, o_ref,
                 st_ref, b_ref, oi_ref, rest_ref, tri_ref, mask_ref, bmask_ref, cmask_ref, *, layer, chunk, heads):
    c = chunk
    hc = c // 2
    dh = HG_HEAD_DIM
    nlev = int(math.log2(c))
    base_lv = int(math.log2(HG_BASE))
    nbig = nlev - base_lv
    hs = range(heads)
    first = (pl.program_id(0) == 0) & (pl.program_id(1) == 0) & (pl.program_id(2) == 0)

    @pl.when(first)
    def _():
        row = lax.broadcasted_iota(I32, (c, c), 0)
        col = lax.broadcasted_iota(I32, (c, c), 1)
        tri_ref[...] = jnp.where(row >= col, 1.0, 0.0).astype(BF16)
        x = row ^ col
        mask_ref[0] = jnp.where(x == 0, 1.0, 0.0).astype(F32)
        for lv in range(1, base_lv + 1):
            mask_ref[lv] = jnp.where(x < (1 << lv), 1.0, 0.0).astype(F32)
        bmask_ref[...] = jnp.where((x < HG_BASE) & (row >= col), 1.0, 0.0).astype(F32)
        xh = lax.broadcasted_iota(I32, (hc, hc), 0) ^ lax.broadcasted_iota(I32, (hc, hc), 1)
        for lv in range(base_lv, nlev - 1):
            cmask_ref[lv - base_lv] = jnp.where(xh < (1 << lv), 1.0, 0.0).astype(F32)

    @pl.when(pl.program_id(2) == 0)
    def _():
        st_ref[...] = jnp.zeros_like(st_ref)

    lbl = lbl_ref[...]
    rows = [lbl[i:i + 1, :] for i in range(lbl.shape[0])]
    mx = functools.reduce(jnp.maximum, rows)
    es = [jnp.exp(r - mx) for r in rows]
    lb = functools.reduce(lambda a, b: a + b, es[:layer + 1]) / functools.reduce(lambda a, b: a + b, es)

    q_all = _silu(q_ref[...].astype(F32))
    forget_all = lb + (1.0 - lb) * _sigmoid(f_ref[...].astype(F32))
    k_all = 1.0 - forget_all
    v_all = v_ref[...]

    w = heads * dh
    bb = _dot(tri_ref[...], jnp.concatenate(_split_bf16(jnp.log2(forget_all), 3), axis=1))
    b_all = bb[:, 0:w] + bb[:, w:2 * w] + bb[:, 2 * w:3 * w]

    def lanes(x, h):
        return x[:, h * dh:(h + 1) * dh]

    for h in hs:
        b_ref[h] = lanes(b_all, h)

    q = [lanes(q_all, h) for h in hs]
    k = [lanes(k_all, h) for h in hs]
    v = [lanes(v_all, h) for h in hs]
    b = [lanes(b_all, h) for h in hs]

    def b_row(h, r, n):
        return jnp.broadcast_to(b_ref[h, pl.ds(r, 1), :], (n, dh))

    blk_i = lax.broadcasted_iota(I32, (c // HG_BASE, dh), 0)
    base_ok = []
    for h in hs:
        ends = b_ref[h, pl.ds(HG_BASE - 1, c // HG_BASE, stride=HG_BASE), :]
        drop = jnp.where(blk_i == 0, 0.0, pltpu.roll(ends, 1, axis=0)) - ends
        base_ok.append(jnp.max(drop) <= HG_BASE_MAX_DECAY)

    qs_l, ks_l, vs_l = [], [], []
    for lv in range(base_lv, nlev):
        m = 1 << lv
        n = 2 * m
        for h in hs:
            qs, ks, vs = [], [], []
            for a in range(c // n):
                mid = b_row(h, a * n + m - 1, m)
                qs.append(q[h][a * n + m:(a + 1) * n] * jnp.exp2(b[h][a * n + m:(a + 1) * n] - mid))
                ks.append(k[h][a * n:a * n + m] * jnp.exp2(mid - b[h][a * n:a * n + m]))
                vs.append(v[h][a * n:a * n + m])
            qs_l.append(jnp.concatenate(qs, axis=0).astype(BF16))
            ks_l.append(jnp.concatenate(ks, axis=0).astype(BF16))
            vs_l.append(jnp.concatenate(vs, axis=0))
    qb16, kb16, qe, kd, b_last, st = [], [], [], [], [], []
    for h in hs:
        start = jnp.concatenate(
            [jnp.zeros((HG_BASE, dh), F32)] + [b_row(h, j * HG_BASE - 1, HG_BASE) for j in range(1, c // HG_BASE)],
            axis=0)
        dlt = jnp.maximum(b[h] - start, -HG_BASE_MAX_DECAY)
        qb16.append((q[h] * jnp.exp2(dlt)).astype(BF16))
        kb16.append((k[h] * jnp.exp2(-dlt)).astype(BF16))
        b_last.append(b_row(h, c - 1, c))
        qe.append((q[h] * jnp.exp2(b[h])).astype(BF16))
        kd.append((k[h] * jnp.exp2(b_last[h] - b[h])).astype(BF16))
        st.append(st_ref[h])

    nprod = nbig * heads
    a_l = [_dot_nt(qs_l[i], ks_l[i]) for i in range(nprod)]
    a16 = [_dot_nt(qb16[h], kb16[h]) for h in hs]
    o_inter = [_dot_nt(qe[h], st[h].astype(BF16)) for h in hs]
    for h in hs:
        st_ref[h] = st[h] * jnp.exp2(b_last[h][0:1, :]) + _dot_tn(v[h], kd[h])

    a_l = [(a_l[i] * cmask_ref[i // heads] if i // heads < nbig - 1 else a_l[i]).astype(BF16) for i in range(nprod)]
    a16 = [(a16[h] * bmask_ref[...]).astype(BF16) for h in hs]
    o_l = [_dot(a_l[i], vs_l[i]) for i in range(nprod)]
    oi_ref[...] = jnp.concatenate([_dot(a16[h], v[h]) for h in hs], axis=1)

    rest = []
    for h in hs:
        pieces = [None] * (c // HG_BASE)
        for lvi in range(nbig):
            m = HG_BASE << lvi
            per = m // HG_BASE
            o_c = o_l[lvi * heads + h]
            for a in range(c // (2 * m)):
                for u in range(per):
                    dst = (a * 2 * m + m) // HG_BASE + u
                    src = o_c[(a * per + u) * HG_BASE:(a * per + u + 1) * HG_BASE]
                    pieces[dst] = src if pieces[dst] is None else pieces[dst] + src
        zero_slab = jnp.zeros((HG_BASE, dh), F32)
        rest.append(o_inter[h] + jnp.concatenate([zero_slab if p is None else p for p in pieces], axis=0))
    rest_ref[...] = jnp.concatenate(rest, axis=1)

    for h in hs:
        @pl.when(jnp.logical_not(base_ok[h]))
        def _(h=h):
            rowi = lax.broadcasted_iota(I32, (c, dh), 0)
            sub = lax.broadcasted_iota(I32, (SUBLANES, dh), 0)
            ntile = c // SUBLANES
            forget = lanes(forget_all, h)
            attn = _dot_nt(q[h].astype(BF16), k[h].astype(BF16)) * mask_ref[0]
            for lv in range(base_lv):
                m = 1 << lv
                isq = (rowi & m) != 0
                if m == 1:
                    e = jnp.where(isq, forget, 1.0)
                else:
                    if m >= SUBLANES:
                        tiles = [b_row(h, (j * SUBLANES // (2 * m)) * 2 * m + m - 1, SUBLANES) for j in range(ntile)]
                    elif m == 4:
                        tiles = [b_row(h, j * SUBLANES + 3, SUBLANES) for j in range(ntile)]
                    else:
                        tiles = [jnp.where(sub < 4, b_row(h, j * SUBLANES + 1, SUBLANES),
                                           b_row(h, j * SUBLANES + 5, SUBLANES)) for j in range(ntile)]
                    mid = jnp.concatenate(tiles, axis=0)
                    e = jnp.exp2(jnp.where(isq, b[h] - mid, mid - b[h]))
                qt = jnp.where(isq, q[h] * e, 0.0).astype(BF16)
                kt = jnp.where(isq, 0.0, k[h] * e).astype(BF16)
                attn = attn + _dot_nt(qt, kt) * mask_ref[lv + 1]
            oi_ref[:, h * dh:(h + 1) * dh] = _dot(attn.astype(BF16), v[h])

    o = oi_ref[...] + rest_ref[...]
    o = jnp.concatenate(
        [lanes(o, h) * lax.rsqrt(jnp.mean(lanes(o, h) * lanes(o, h), axis=-1, keepdims=True) + RMS_EPS) for h in hs],
        axis=1)
    o = o * gain_ref[...] * _silu(g_ref[...].astype(F32))
    o_ref[...] = o.astype(o_ref.dtype)


def _hgrn_mixer(proj, lb_logits, head_gain, layer, j, bsz, seq):
    t, d4 = proj.shape
    d = d4 // 4
    nh = d // HG_HEAD_DIM
    heads = 2 if nh % 2 == 0 else 1
    w = heads * HG_HEAD_DIM
    nhp = nh // heads
    chunk = 256 if seq % 256 == 0 else 128
    nc = seq // chunk
    nlev = int(math.log2(chunk))
    base_lv = int(math.log2(HG_BASE))

    def col(part):
        return lambda b, h, c: (b * nc + c, part * nhp + h)

    return pl.pallas_call(
        functools.partial(_hgrn_kernel, layer=layer, chunk=chunk, heads=heads),
        grid=(bsz, nhp, nc),
        in_specs=[
            pl.BlockSpec((chunk, w), col(0)),
            pl.BlockSpec((chunk, w), col(1)),
            pl.BlockSpec((chunk, w), col(2)),
            pl.BlockSpec((chunk, w), col(3)),
            pl.BlockSpec((lb_logits.shape[0], w), lambda b, h, c: (0, h)),
            pl.BlockSpec((None, 1, w), lambda b, h, c: (j, 0, h)),
        ],
        out_specs=pl.BlockSpec((chunk, w), lambda b, h, c: (b * nc + c, h)),
        out_shape=jax.ShapeDtypeStruct((t, d), BF16),
        scratch_shapes=[
            pltpu.VMEM((heads, HG_HEAD_DIM, HG_HEAD_DIM), F32),
            pltpu.VMEM((heads, chunk, HG_HEAD_DIM), F32),
            pltpu.VMEM((chunk, w), F32),
            pltpu.VMEM((chunk, w), F32),
            pltpu.VMEM((chunk, chunk), BF16),
            pltpu.VMEM((base_lv + 1, chunk, chunk), F32),
            pltpu.VMEM((chunk, chunk), F32),
            pltpu.VMEM((nlev - 1 - base_lv, chunk // 2, chunk // 2), F32),
        ],
        compiler_params=_cparams(3),
        name="hgrn2_mixer",
    )(proj, proj, proj, proj, lb_logits.astype(F32), head_gain.astype(F32).reshape(head_gain.shape[0], 1, d))


def _t5_bucket(dist):
    max_exact = N_BUCKETS // 2
    n = jnp.maximum(dist, 0)
    large = max_exact + (jnp.log(jnp.maximum(n, 1).astype(F32) / max_exact)
                         / math.log(MAX_DISTANCE / max_exact)
                         * (N_BUCKETS - max_exact)).astype(I32)
    large = jnp.minimum(large, N_BUCKETS - 1)
    return jnp.where(n < max_exact, n, large)


def _attn_kernel(rb_ref, sink_ref, q_ref, kp_ref, kc_ref, vp_ref, vc_ref, bucket_ref, o_ref,
                 bias_ref, *, group, layer_j):
    blk = ATT_BLOCK
    hd = ATT_HEAD_DIM
    pr = pl.program_id(1)
    i = pl.program_id(2)
    heads_per_step = 2 * group

    @pl.when(i == 0)
    def _():
        qi = lax.broadcasted_iota(I32, (blk, 2 * blk), 0)
        sj = lax.broadcasted_iota(I32, (blk, 2 * blk), 1)
        dist = qi + blk - sj
        band = (dist >= 0) & (dist < WINDOW)
        bucket = bucket_ref[...]

        def per_head(hh, carry):
            h = pr * heads_per_step + hh
            tbl = jnp.zeros((blk, 2 * blk), F32)
            for bk in range(N_BUCKETS):
                tbl = jnp.where(bucket == bk, rb_ref[bk, h], tbl)
            bias_ref[hh] = jnp.where(band, tbl, NEG_INF)
            bias_ref[heads_per_step + hh] = jnp.where(band & (sj >= blk), tbl, NEG_INF)
            return carry

        lax.fori_loop(0, heads_per_step, per_head, 0)

    lane = lax.broadcasted_iota(I32, (2 * blk, LANES), 1)
    table0 = jnp.where(i == 0, heads_per_step, 0)

    kk = jnp.concatenate([kp_ref[...], kc_ref[...]], axis=0).astype(F32)
    vv = jnp.concatenate([vp_ref[...], vc_ref[...]], axis=0).astype(F32)
    q_all = q_ref[...] * (hd ** -0.5)

    kbds, vbds = [], []
    for c in range(2):
        if c == 0:
            klo = jnp.where(lane < hd, kk, 0.0)
            khi = pltpu.roll(klo, hd, axis=1)
            vlo = jnp.where(lane < hd, vv, 0.0)
            vhi = pltpu.roll(vlo, hd, axis=1)
        else:
            khi = jnp.where(lane >= hd, kk, 0.0)
            klo = pltpu.roll(khi, hd, axis=1)
            vhi = jnp.where(lane >= hd, vv, 0.0)
            vlo = pltpu.roll(vhi, hd, axis=1)
        kbds.append(jnp.concatenate([klo, khi], axis=0).astype(BF16))
        vbds.append(jnp.concatenate([vlo, vhi], axis=0).astype(BF16))

    half = group // 2
    tiles = [(c, p) for c in range(2) for p in range(half)]
    lgs = [_dot_nt(q_all[:, (c * half + p) * LANES:(c * half + p + 1) * LANES], kbds[c]) for c, p in tiles]
    p2s, rinvs = [], []
    for (c, p), lg in zip(tiles, lgs):
        probs, rinv = [], []
        for hh in range(2):
            hl = c * group + 2 * p + hh
            sk = sink_ref[layer_j, pr * heads_per_step + hl]
            l = lg[:, hh * 2 * blk:(hh + 1) * 2 * blk] + bias_ref[table0 + hl]
            mx = jnp.maximum(jnp.max(l, axis=-1, keepdims=True), sk)
            pe = jnp.exp(l - mx)
            rinv.append(1.0 / (jnp.sum(pe, axis=-1, keepdims=True) + jnp.exp(sk - mx)))
            probs.append(pe.astype(BF16))
        p2s.append(jnp.concatenate(probs, axis=1))
        rinvs.append(rinv)
    lane_o = lax.broadcasted_iota(I32, (blk, LANES), 1)
    for (c, p), p2, rinv in zip(tiles, p2s, rinvs):
        tile = c * half + p
        o = _dot(p2, vbds[c]) * jnp.where(lane_o < hd, rinv[0], rinv[1])
        o_ref[:, tile * LANES:(tile + 1) * LANES] = o.astype(o_ref.dtype)


def _attn_mixer(proj, sinks, rel_bias, layer_j, bsz, seq, d):
    t, att_in = proj.shape
    kvw = (att_in - d) // 2
    n_heads = d // ATT_HEAD_DIM
    kvh = kvw // ATT_HEAD_DIM
    group = n_heads // kvh
    assert kvh % 2 == 0 and group % 2 == 0
    blk = ATT_BLOCK
    nb = seq // blk
    npair = kvh // 2
    qw = 2 * group * ATT_HEAD_DIM
    k0 = d // LANES
    v0 = (d + kvw) // LANES

    qi = jnp.arange(blk)[:, None]
    sj = jnp.arange(2 * blk)[None, :]
    bucket = _t5_bucket(qi + blk - sj).astype(I32)

    def prev(i):
        return jnp.maximum(i - 1, 0)

    grid_spec = pltpu.PrefetchScalarGridSpec(
        num_scalar_prefetch=2,
        grid=(bsz, npair, nb),
        in_specs=[
            pl.BlockSpec((blk, qw), lambda b, p, i, *_: (b * nb + i, p)),
            pl.BlockSpec((blk, LANES), lambda b, p, i, *_: (b * nb + prev(i), k0 + p)),
            pl.BlockSpec((blk, LANES), lambda b, p, i, *_: (b * nb + i, k0 + p)),
            pl.BlockSpec((blk, LANES), lambda b, p, i, *_: (b * nb + prev(i), v0 + p)),
            pl.BlockSpec((blk, LANES), lambda b, p, i, *_: (b * nb + i, v0 + p)),
            pl.BlockSpec((blk, 2 * blk), lambda b, p, i, *_: (0, 0)),
        ],
        out_specs=pl.BlockSpec((blk, qw), lambda b, p, i, *_: (b * nb + i, p)),
        scratch_shapes=[pltpu.VMEM((4 * group, blk, 2 * blk), F32)],
    )
    return pl.pallas_call(
        functools.partial(_attn_kernel, group=group, layer_j=layer_j),
        grid_spec=grid_spec,
        out_shape=jax.ShapeDtypeStruct((t, d), BF16),
        compiler_params=_cparams(3),
        name="swa_sink_mixer",
    )(rel_bias.astype(F32), sinks.astype(F32), proj, proj, proj, proj, proj, bucket)


def _layer_norm_rows(z, g, b):
    mu = jnp.mean(z, axis=-1, keepdims=True)
    zc = z - mu
    var = jnp.mean(zc * zc, axis=-1, keepdims=True)
    return zc * lax.rsqrt(var + LN_EPS) * g + b


def _ln_router_kernel(x_ref, y_ref, gate_ref, sc_ref, sh_ref, lng_ref, lnb_ref, wr_ref, br_ref,
                      xo_ref, hf_ref, meta_ref, cnt_ref, carry_ref, wsplit_ref, *, alpha, n_groups, n_experts):
    i = pl.program_id(0)
    bm = x_ref.shape[0]
    epg = n_experts // n_groups

    @pl.when(i == 0)
    def _():
        carry_ref[...] = jnp.zeros_like(carry_ref)
        wh, wl = _split_bf16(wr_ref[...], 2)
        wsplit_ref[:, 0:LANES] = wh
        wsplit_ref[:, LANES:2 * LANES] = wl

    z = alpha * x_ref[...] + (1.0 + gate_ref[...]) * y_ref[...]
    xn = _layer_norm_rows(z, lng_ref[...], lnb_ref[...])
    xo_ref[...] = xn
    hf = xn * (1.0 + sc_ref[...]) + sh_ref[...]
    hf_ref[...] = _pack_bf16_pairs(hf)

    xh, xl = _split_bf16(hf, 2)
    hh = _dot(xh, wsplit_ref[...])
    lg = hh[:, 0:LANES] + hh[:, LANES:2 * LANES] + _dot(xl, wsplit_ref[:, 0:LANES]) + br_ref[...]

    lane = lax.broadcasted_iota(I32, (bm, LANES), 1)
    lanef = lane.astype(F32)
    big = float(LANES)

    gl = jnp.where(lane < n_groups, lg, NEG_INF)
    gmax = jnp.max(gl, axis=-1, keepdims=True)
    gsel = jnp.min(jnp.where(gl == gmax, lanef, big), axis=-1, keepdims=True)
    p_group = 1.0 / jnp.sum(jnp.exp(gl - gmax), axis=-1, keepdims=True)

    lo = n_groups + gsel * epg
    el = jnp.where((lanef >= lo) & (lanef < lo + epg), lg, NEG_INF)
    m1 = jnp.max(el, axis=-1, keepdims=True)
    i1 = jnp.min(jnp.where(el == m1, lanef, big), axis=-1, keepdims=True)
    el2 = jnp.where(lanef == i1, NEG_INF, el)
    m2 = jnp.max(el2, axis=-1, keepdims=True)
    i2 = jnp.min(jnp.where(el2 == m2, lanef, big), axis=-1, keepdims=True)
    e21 = jnp.exp(m2 - m1)
    g0 = p_group / (1.0 + e21)
    g1 = g0 * e21

    oh0 = lanef == i1
    oh1 = lanef == i2
    cnt = jnp.where(oh0 | oh1, 1.0, 0.0)
    row = lax.broadcasted_iota(I32, (bm, bm), 0)
    col = lax.broadcasted_iota(I32, (bm, bm), 1)
    stril = jnp.where(row > col, 1.0, 0.0).astype(BF16)
    before = _dot(stril, cnt.astype(BF16)) + carry_ref[...]
    r0 = jnp.sum(jnp.where(oh0, before, 0.0), axis=-1, keepdims=True)
    r1 = jnp.sum(jnp.where(oh1, before, 0.0), axis=-1, keepdims=True)
    carry_ref[...] = carry_ref[...] + jnp.sum(cnt, axis=0, keepdims=True)
    cnt_ref[...] = carry_ref[...]

    meta = jnp.where(lane == 0, i1 - n_groups, 0.0)
    meta = jnp.where(lane == 1, i2 - n_groups, meta)
    meta = jnp.where(lane == 2, g0, meta)
    meta = jnp.where(lane == 3, g1, meta)
    meta = jnp.where(lane == 4, r0, meta)
    meta = jnp.where(lane == 5, r1, meta)
    meta_ref[...] = meta


def _ln_router(x2, y2, mod, row_of, ln_g, ln_b, w_router, b_router, alpha, n_groups, n_experts, seq):
    t, d = x2.shape
    bm = 128
    nbs = seq // bm

    def mrow(which):
        return lambda i: (row_of(i // nbs, which), 0, 0)

    return pl.pallas_call(
        functools.partial(_ln_router_kernel, alpha=alpha, n_groups=n_groups, n_experts=n_experts),
        grid=(t // bm,),
        in_specs=[
            pl.BlockSpec((bm, d), lambda i: (i, 0)),
            pl.BlockSpec((bm, d), lambda i: (i, 0)),
            pl.BlockSpec((None, 1, d), mrow(2)),
            pl.BlockSpec((None, 1, d), mrow(4)),
            pl.BlockSpec((None, 1, d), mrow(3)),
            pl.BlockSpec((1, d), lambda i: (0, 0)),
            pl.BlockSpec((1, d), lambda i: (0, 0)),
            pl.BlockSpec((d, LANES), lambda i: (0, 0)),
            pl.BlockSpec((1, LANES), lambda i: (0, 0)),
        ],
        out_specs=[
            pl.BlockSpec((bm, d), lambda i: (i, 0)),
            pl.BlockSpec((bm, d // 2), lambda i: (i, 0)),
            pl.BlockSpec((bm, LANES), lambda i: (i, 0)),
            pl.BlockSpec((1, LANES), lambda i: (0, 0)),
        ],
        out_shape=[
            jax.ShapeDtypeStruct((t, d), F32),
            jax.ShapeDtypeStruct((t, d // 2), jnp.uint32),
            jax.ShapeDtypeStruct((t, LANES), F32),
            jax.ShapeDtypeStruct((1, LANES), F32),
        ],
        scratch_shapes=[pltpu.VMEM((1, LANES), F32), pltpu.VMEM((d, 2 * LANES), BF16)],
        compiler_params=_cparams(1),
        name="ln_router",
    )(x2, y2, mod, mod, mod, ln_g, ln_b, w_router, b_router)


def _cast_rows(src_ref, dst_ref, rows=128):
    def body(r, carry):
        sl = pl.ds(pl.multiple_of(r * rows, rows), rows)
        dst_ref[sl, :] = src_ref[sl, :].astype(dst_ref.dtype)
        return carry
    lax.fori_loop(0, src_ref.shape[0] // rows, body, 0)


def _moe_kernel(blk_e_ref, nxt_e_ref, first_ref, nused_ref,
                tokc_ref, tokn_ref, hf_hbm, wgu_hbm, wdn_hbm, y_ref,
                xbuf, wgu_st, wdn_st, wgu_bf, wdn_bf, gsem, wsem, *, layer, d_expert):
    i = pl.program_id(0)
    bm = xbuf.shape[1]
    nused = nused_ref[0]
    slot = i % 2

    def row_copy(tok, r, s):
        return pltpu.make_async_copy(hf_hbm.at[pl.ds(tok, 1), :], xbuf.at[s, pl.ds(r, 1), :], gsem.at[s])

    def issue_rows(tok_ref, s):
        def body(r, carry):
            row_copy(tok_ref[0, r], r, s).start()
            return carry
        lax.fori_loop(0, bm, body, 0, unroll=8)

    def wait_rows(s):
        pltpu.make_async_copy(hf_hbm.at[pl.ds(0, bm), :], xbuf.at[s], gsem.at[s]).wait()

    def weight_copies(e):
        return (pltpu.make_async_copy(wgu_hbm.at[layer, e], wgu_st, wsem.at[0]),
                pltpu.make_async_copy(wdn_hbm.at[layer, e], wdn_st, wsem.at[1]))

    @pl.when(i == 0)
    def _():
        issue_rows(tokc_ref, 0)
        for cp in weight_copies(blk_e_ref[0]):
            cp.start(priority=1)

    @pl.when((i < nused) & (first_ref[i] == 1))
    def _():
        for cp in weight_copies(blk_e_ref[i]):
            cp.wait()
        _cast_rows(wgu_st, wgu_bf)
        _cast_rows(wdn_st, wdn_bf)

        @pl.when(nxt_e_ref[i] >= 0)
        def _():
            for cp in weight_copies(nxt_e_ref[i]):
                cp.start(priority=1)

    @pl.when(i + 1 < nused)
    def _():
        issue_rows(tokn_ref, 1 - slot)

    @pl.when(i < nused)
    def _():
        wait_rows(slot)
        xb = _unpack_bf16_pairs(xbuf[slot]).astype(BF16)
        a = _dot(xb, wgu_bf[...])
        h = _silu(a[:, :d_expert]) * a[:, d_expert:]
        y_ref[...] = _pack_bf16_pairs(_dot(h.astype(BF16), wdn_bf[...]))

    @pl.when(i >= nused)
    def _():
        y_ref[...] = jnp.zeros_like(y_ref)


def _moe_experts(hf, tok_pad, blk_e, nxt_e, first, nused, w_gate_up, w_down, layer, bm):
    d = w_down.shape[3]
    nb = blk_e.shape[0]
    d_expert = w_down.shape[2]
    tok3 = tok_pad.reshape(nb, 1, bm)
    grid_spec = pltpu.PrefetchScalarGridSpec(
        num_scalar_prefetch=4,
        grid=(nb,),
        in_specs=[
            pl.BlockSpec((None, 1, bm), lambda i, *_: (i, 0, 0), memory_space=pltpu.SMEM),
            pl.BlockSpec((None, 1, bm), lambda i, *_: (jnp.minimum(i + 1, nb - 1), 0, 0), memory_space=pltpu.SMEM),
            pl.BlockSpec(memory_space=pl.ANY),
            pl.BlockSpec(memory_space=pl.ANY),
            pl.BlockSpec(memory_space=pl.ANY),
        ],
        out_specs=pl.BlockSpec((bm, d // 2), lambda i, *_: (i, 0)),
        scratch_shapes=[
            pltpu.VMEM((2, bm, d // 2), jnp.uint32),
            pltpu.VMEM((d, 2 * d_expert), F32),
            pltpu.VMEM((d_expert, d), F32),
            pltpu.VMEM((d, 2 * d_expert), BF16),
            pltpu.VMEM((d_expert, d), BF16),
            pltpu.SemaphoreType.DMA((2,)),
            pltpu.SemaphoreType.DMA((2,)),
        ],
    )
    return pl.pallas_call(
        functools.partial(_moe_kernel, layer=layer, d_expert=d_expert),
        grid_spec=grid_spec,
        out_shape=jax.ShapeDtypeStruct((nb * bm, d // 2), jnp.uint32),
        compiler_params=_cparams(1),
        name="moe_experts",
    )(blk_e, nxt_e, first, nused, tok3, tok3, hf, w_gate_up, w_down)


def _moe_plan(meta, counts_row, n_groups, n_experts, bm):
    t = meta.shape[0]
    eid = meta[:, 0:2].astype(I32)
    rank = meta[:, 4:6].astype(I32)
    counts = counts_row[0, n_groups:n_groups + n_experts].astype(I32)
    padded = (counts + bm - 1) // bm * bm
    pad_end = jnp.cumsum(padded)
    pad_start = pad_end - padded
    dest = pad_start[eid] + rank
    nb = (2 * t) // bm + n_experts
    nused = pad_end[-1] // bm
    ids = jnp.arange(nb, dtype=I32)
    raw_e = jnp.minimum(jnp.sum((pad_end[None, :] <= (ids * bm)[:, None]).astype(I32), axis=1), n_experts - 1)
    used = ids < nused
    blk_e = jnp.where(used, raw_e, raw_e[nused - 1])
    prev_e = jnp.concatenate([jnp.full((1,), -1, I32), blk_e[:-1]])
    first = (used & (blk_e != prev_e)).astype(I32)
    key = jnp.where(used, blk_e, n_experts)
    nxt_idx = jnp.sum((key[None, :] <= blk_e[:, None]).astype(I32), axis=1)
    nxt_e = jnp.where(nxt_idx < nused, key[jnp.minimum(nxt_idx, nb - 1)], -1).astype(I32)
    tok = jnp.repeat(jnp.arange(t, dtype=I32), 2)
    tok_pad = jnp.zeros((nb * bm,), I32).at[dest.reshape(-1)].set(tok)
    return dest, tok_pad, blk_e, nxt_e, first, nused.reshape(1).astype(I32)


def _ln_combine_kernel(*refs, alpha, with_next):
    if with_next:
        (dc_ref, dn_ref, x_ref, meta_ref, gate_ref, lng_ref, lnb_ref, sc_ref, sh_ref, y_hbm,
         xo_ref, hm_ref, ybuf, sem) = refs
    else:
        (dc_ref, dn_ref, x_ref, meta_ref, gate_ref, lng_ref, lnb_ref, y_hbm,
         xo_ref, ybuf, sem) = refs
    i = pl.program_id(0)
    n = pl.num_programs(0)
    bm = x_ref.shape[0]
    slot = i % 2

    def row_copy(src, r, s):
        return pltpu.make_async_copy(y_hbm.at[pl.ds(src, 1), :], ybuf.at[s, pl.ds(r, 1), :], sem.at[s])

    def issue_rows(d_ref, s):
        def body(r, carry):
            row_copy(d_ref[0, r], r, s).start()
            return carry
        lax.fori_loop(0, 2 * bm, body, 0, unroll=8)

    def wait_rows(s):
        pltpu.make_async_copy(y_hbm.at[pl.ds(0, 2 * bm), :], ybuf.at[s], sem.at[s]).wait()

    @pl.when(i == 0)
    def _():
        issue_rows(dc_ref, 0)

    @pl.when(i + 1 < n)
    def _():
        issue_rows(dn_ref, 1 - slot)

    wait_rows(slot)
    meta = meta_ref[...]
    y = (_unpack_bf16_pairs(ybuf[slot, 0:bm, :]) * meta[:, 2:3]
         + _unpack_bf16_pairs(ybuf[slot, bm:2 * bm, :]) * meta[:, 3:4])
    z = alpha * x_ref[...] + (1.0 + gate_ref[...]) * y
    xn = _layer_norm_rows(z, lng_ref[...], lnb_ref[...])
    xo_ref[...] = xn
    if with_next:
        hm_ref[...] = (xn * (1.0 + sc_ref[...]) + sh_ref[...]).astype(hm_ref.dtype)


def _ln_combine(x2, meta, dest, ysort, mod, row_of, next_row_of, ln_g, ln_b, alpha, seq):
    t, d = x2.shape
    bm = 128
    nbs = seq // bm
    nblk = t // bm
    with_next = next_row_of is not None
    dest3 = dest.reshape(nblk, bm, 2).transpose(0, 2, 1).reshape(nblk, 1, 2 * bm)

    def mrow(fn, which):
        return lambda i: (fn(i // nbs, which), 0, 0)

    in_specs = [
        pl.BlockSpec((None, 1, 2 * bm), lambda i: (i, 0, 0), memory_space=pltpu.SMEM),
        pl.BlockSpec((None, 1, 2 * bm), lambda i: (jnp.minimum(i + 1, nblk - 1), 0, 0), memory_space=pltpu.SMEM),
        pl.BlockSpec((bm, d), lambda i: (i, 0)),
        pl.BlockSpec((bm, LANES), lambda i: (i, 0)),
        pl.BlockSpec((None, 1, d), mrow(row_of, 5)),
        pl.BlockSpec((1, d), lambda i: (0, 0)),
        pl.BlockSpec((1, d), lambda i: (0, 0)),
    ]
    args = [dest3, dest3, x2, meta, mod, ln_g, ln_b]
    out_specs = [pl.BlockSpec((bm, d), lambda i: (i, 0))]
    out_shape = [jax.ShapeDtypeStruct((t, d), F32)]
    if with_next:
        in_specs += [pl.BlockSpec((None, 1, d), mrow(next_row_of, 1)),
                     pl.BlockSpec((None, 1, d), mrow(next_row_of, 0))]
        args += [mod, mod]
        out_specs.append(pl.BlockSpec((bm, d), lambda i: (i, 0)))
        out_shape.append(jax.ShapeDtypeStruct((t, d), BF16))
    in_specs.append(pl.BlockSpec(memory_space=pl.ANY))
    args.append(ysort)
    return pl.pallas_call(
        functools.partial(_ln_combine_kernel, alpha=alpha, with_next=with_next),
        grid=(nblk,),
        in_specs=in_specs,
        out_specs=out_specs,
        out_shape=out_shape,
        scratch_shapes=[pltpu.VMEM((2, 2 * bm, d // 2), jnp.uint32), pltpu.SemaphoreType.DMA((2,))],
        compiler_params=_cparams(1),
        name="ln_moe_combine",
    )(*args)


def kernel(x, c, w_ada, b_ada, ln_g, ln_b, w_in_a, lb_logits, head_gain_a, w_out_a, w_in_b, attn_sinks, w_out_b, rel_bias, w_router_group, b_router_group, w_router_expert, b_router_expert, w_gate_up, w_down):
    bsz, seq, d = x.shape
    depth = w_ada.shape[0]
    n_groups = w_router_group.shape[2]
    n_experts = w_router_expert.shape[2]
    alpha = (2 * depth) ** 0.25
    t = bsz * seq
    moe_bm = 128

    mod = _ada_modulation(c, w_ada, b_ada)

    def row_of_layer(layer):
        return lambda b, which: (layer * bsz + b) * 6 + which

    x2 = x.reshape(t, d).astype(F32)
    row0 = row_of_layer(0)
    hm = _modulate(x2, mod, lambda b: row0(b, 1), lambda b: row0(b, 0), bsz, seq)

    for layer in range(depth):
        row_of = row_of_layer(layer)
        j = layer // 2
        if layer % 2 == 0:
            proj = _matmul(hm, w_in_a, j, BF16)
            o = _hgrn_mixer(proj, lb_logits, head_gain_a, layer, j, bsz, seq)
            y = _matmul(o, w_out_a, j, F32)
        else:
            proj = _matmul(hm, w_in_b, j, BF16)
            o = _attn_mixer(proj, attn_sinks, rel_bias, j, bsz, seq, d)
            y = _matmul(o, w_out_b, j, F32)

        n_pad = LANES - n_groups - n_experts
        w_router = jnp.concatenate(
            [w_router_group[layer].astype(F32), w_router_expert[layer].astype(F32), jnp.zeros((d, n_pad), F32)], axis=1)
        b_router = jnp.concatenate(
            [b_router_group[layer].astype(F32), b_router_expert[layer].astype(F32), jnp.zeros((n_pad,), F32)]
        ).reshape(1, LANES)

        x2, hf, meta, counts = _ln_router(
            x2, y, mod, row_of, ln_g[layer, 0:1].astype(F32), ln_b[layer, 0:1].astype(F32),
            w_router, b_router, alpha, n_groups, n_experts, seq)
        dest, tok_pad, blk_e, nxt_e, first, nused = _moe_plan(meta, counts, n_groups, n_experts, moe_bm)
        ysort = _moe_experts(hf, tok_pad, blk_e, nxt_e, first, nused, w_gate_up, w_down, layer, moe_bm)
        next_row_of = row_of_layer(layer + 1) if layer + 1 < depth else None
        outs = _ln_combine(x2, meta, dest, ysort, mod, row_of, next_row_of,
                           ln_g[layer, 1:2].astype(F32), ln_b[layer, 1:2].astype(F32), alpha, seq)
        x2 = outs[0]
        if next_row_of is not None:
            hm = outs[1]

    return x2.reshape(bsz, seq, d).astype(x.dtype)
```

```python
import functools
import math

import jax
import jax.numpy as jnp
from jax import lax
from jax.experimental import pallas as pl
from jax.experimental.pallas import tpu as pltpu

F32 = jnp.float32
BF16 = jnp.bfloat16
I32 = jnp.int32

LANES = 128
SUBLANES = 8
V7X_VMEM_LIMIT_BYTES = 56 * 1024 * 1024

HG_HEAD_DIM = 128
ATT_HEAD_DIM = 64
ATT_BLOCK = 128
WINDOW = 128
N_BUCKETS = 32
MAX_DISTANCE = 128
LN_EPS = 1e-5
RMS_EPS = 1e-6
NEG_INF = float("-inf")


def _cparams(n_axes):
    return pltpu.CompilerParams(
        dimension_semantics=("arbitrary",) * n_axes,
        vmem_limit_bytes=V7X_VMEM_LIMIT_BYTES,
    )


def _sigmoid(x):
    return 0.5 * jnp.tanh(0.5 * x) + 0.5


def _silu(x):
    return x * _sigmoid(x)


def _dot_nt(a, b):
    return lax.dot_general(a, b, (((1,), (1,)), ((), ())), preferred_element_type=F32)


def _dot_tn(a, b):
    return lax.dot_general(a, b, (((0,), (0,)), ((), ())), preferred_element_type=F32)


def _dot(a, b):
    return jnp.dot(a, b, preferred_element_type=F32)


def _pack_bf16_pairs(x):
    n = x.shape[1] // 2
    bits = pltpu.bitcast(x.astype(BF16).astype(F32), jnp.uint32)
    return (bits[:, :n] >> 16) | (bits[:, n:] & jnp.uint32(0xFFFF0000))


def _unpack_bf16_pairs(w):
    lo = pltpu.bitcast(w << 16, F32)
    hi = pltpu.bitcast(w & jnp.uint32(0xFFFF0000), F32)
    return jnp.concatenate([lo, hi], axis=1)


def _split_bf16(x, parts):
    out = []
    r = x
    for _ in range(parts):
        h = r.astype(BF16)
        out.append(h)
        r = r - h.astype(F32)
    return out


def _ada_kernel(c_ref, w_ref, b_ref, o_ref):
    ca = _silu(c_ref[...]).astype(BF16)
    o_ref[...] = _dot(ca, w_ref[...].astype(BF16)) + b_ref[...]


def _ada_modulation(c, w_ada, b_ada):
    nl, d, n6 = w_ada.shape
    bsz = c.shape[0]
    rows = -(-bsz // SUBLANES) * SUBLANES
    c8 = jnp.zeros((rows, d), F32).at[:bsz].set(c.astype(F32))
    tn = 512
    out = pl.pallas_call(
        _ada_kernel,
        grid=(nl, n6 // tn),
        in_specs=[
            pl.BlockSpec((rows, d), lambda l, j: (0, 0)),
            pl.BlockSpec((None, d, tn), lambda l, j: (l, 0, j)),
            pl.BlockSpec((None, 1, tn), lambda l, j: (l, 0, j)),
        ],
        out_specs=pl.BlockSpec((None, rows, tn), lambda l, j: (l, 0, j)),
        out_shape=jax.ShapeDtypeStruct((nl, rows, n6), F32),
        compiler_params=_cparams(2),
        name="ada_modulation",
    )(c8, w_ada, b_ada.reshape(nl, 1, n6))
    return out[:, :bsz].reshape(nl * bsz * 6, 1, d)


def _modulate_kernel(x_ref, sc_ref, sh_ref, o_ref):
    o_ref[...] = (x_ref[...] * (1.0 + sc_ref[...]) + sh_ref[...]).astype(o_ref.dtype)


def _modulate(x2, mod, sc_row, sh_row, bsz, seq):
    t, d = x2.shape
    bs = min(512, seq)
    nbs = seq // bs
    return pl.pallas_call(
        _modulate_kernel,
        grid=(t // bs,),
        in_specs=[
            pl.BlockSpec((bs, d), lambda i: (i, 0)),
            pl.BlockSpec((None, 1, d), lambda i: (sc_row(i // nbs), 0, 0)),
            pl.BlockSpec((None, 1, d), lambda i: (sh_row(i // nbs), 0, 0)),
        ],
        out_specs=pl.BlockSpec((bs, d), lambda i: (i, 0)),
        out_shape=jax.ShapeDtypeStruct((t, d), BF16),
        compiler_params=_cparams(1),
        name="modulate",
    )(x2, mod, mod)


def _matmul_kernel(x_ref, w_ref, o_ref, wbf_ref):
    @pl.when(pl.program_id(1) == 0)
    def _():
        wbf_ref[...] = w_ref[...].astype(BF16)

    o_ref[...] = _dot(x_ref[...], wbf_ref[...]).astype(o_ref.dtype)


def _matmul(x, w3, layer, out_dtype):
    m, k = x.shape
    n = w3.shape[2]
    bm = min(1024, m)
    bn = 512
    return pl.pallas_call(
        _matmul_kernel,
        grid=(n // bn, m // bm),
        in_specs=[
            pl.BlockSpec((bm, k), lambda j, i: (i, 0)),
            pl.BlockSpec((None, k, bn), lambda j, i: (layer, 0, j)),
        ],
        out_specs=pl.BlockSpec((bm, bn), lambda j, i: (i, j)),
        out_shape=jax.ShapeDtypeStruct((m, n), out_dtype),
        scratch_shapes=[pltpu.VMEM((k, bn), BF16)],
        compiler_params=_cparams(2),
        name="dense_projection",
    )(x, w3)


HG_BASE = 16
HG_BASE_MAX_DECAY = 86.0


def _hgrn_kernel(q_ref, f_ref, v_ref, g_ref, lbl_ref, gain_ref, o_ref,
                 st_ref, b_ref, oi_ref, rest_ref, tri_ref, mask_ref, bmask_ref, cmask_ref, *, layer, chunk, heads):
    c = chunk
    hc = c // 2
    dh = HG_HEAD_DIM
    nlev = int(math.log2(c))
    base_lv = int(math.log2(HG_BASE))
    nbig = nlev - base_lv
    hs = range(heads)
    first = (pl.program_id(0) == 0) & (pl.program_id(1) == 0) & (pl.program_id(2) == 0)

    @pl.when(first)
    def _():
        row = lax.broadcasted_iota(I32, (c, c), 0)
        col = lax.broadcasted_iota(I32, (c, c), 1)
        tri_ref[...] = jnp.where(row >= col, 1.0, 0.0).astype(BF16)
        x = row ^ col
        mask_ref[0] = jnp.where(x == 0, 1.0, 0.0).astype(F32)
        for lv in range(1, base_lv + 1):
            mask_ref[lv] = jnp.where(x < (1 << lv), 1.0, 0.0).astype(F32)
        bmask_ref[...] = jnp.where((x < HG_BASE) & (row >= col), 1.0, 0.0).astype(F32)
        xh = lax.broadcasted_iota(I32, (hc, hc), 0) ^ lax.broadcasted_iota(I32, (hc, hc), 1)
        for lv in range(base_lv, nlev - 1):
            cmask_ref[lv - base_lv] = jnp.where(xh < (1 << lv), 1.0, 0.0).astype(F32)

    @pl.when(pl.program_id(2) == 0)
    def _():
        st_ref[...] = jnp.zeros_like(st_ref)

    lbl = lbl_ref[...]
    rows = [lbl[i:i + 1, :] for i in range(lbl.shape[0])]
    mx = functools.reduce(jnp.maximum, rows)
    es = [jnp.exp(r - mx) for r in rows]
    lb = functools.reduce(lambda a, b: a + b, es[:layer + 1]) / functools.reduce(lambda a, b: a + b, es)

    q_all = _silu(q_ref[...].astype(F32))
    forget_all = lb + (1.0 - lb) * _sigmoid(f_ref[...].astype(F32))
    k_all = 1.0 - forget_all
    v_all = v_ref[...]

    w = heads * dh
    bb = _dot(tri_ref[...], jnp.concatenate(_split_bf16(jnp.log2(forget_all), 3), axis=1))
    b_all = bb[:, 0:w] + bb[:, w:2 * w] + bb[:, 2 * w:3 * w]

    def lanes(x, h):
        return x[:, h * dh:(h + 1) * dh]

    for h in hs:
        b_ref[h] = lanes(b_all, h)

    q = [lanes(q_all, h) for h in hs]
    k = [lanes(k_all, h) for h in hs]
    v = [lanes(v_all, h) for h in hs]
    b = [lanes(b_all, h) for h in hs]

    def b_row(h, r, n):
        return jnp.broadcast_to(b_ref[h, pl.ds(r, 1), :], (n, dh))

    blk_i = lax.broadcasted_iota(I32, (c // HG_BASE, dh), 0)
    base_ok = []
    for h in hs:
        ends = b_ref[h, pl.ds(HG_BASE - 1, c // HG_BASE, stride=HG_BASE), :]
        drop = jnp.where(blk_i == 0, 0.0, pltpu.roll(ends, 1, axis=0)) - ends
        base_ok.append(jnp.max(drop) <= HG_BASE_MAX_DECAY)

    qs_l, ks_l, vs_l = [], [], []
    for lv in range(base_lv, nlev):
        m = 1 << lv
        n = 2 * m
        for h in hs:
            qs, ks, vs = [], [], []
            for a in range(c // n):
                mid = b_row(h, a * n + m - 1, m)
                qs.append(q[h][a * n + m:(a + 1) * n] * jnp.exp2(b[h][a * n + m:(a + 1) * n] - mid))
                ks.append(k[h][a * n:a * n + m] * jnp.exp2(mid - b[h][a * n:a * n + m]))
                vs.append(v[h][a * n:a * n + m])
            qs_l.append(jnp.concatenate(qs, axis=0).astype(BF16))
            ks_l.append(jnp.concatenate(ks, axis=0).astype(BF16))
            vs_l.append(jnp.concatenate(vs, axis=0))
    qb16, kb16, qe, kd, b_last, st = [], [], [], [], [], []
    for h in hs:
        start = jnp.concatenate(
            [jnp.zeros((HG_BASE, dh), F32)] + [b_row(h, j * HG_BASE - 1, HG_BASE) for j in range(1, c // HG_BASE)],
            axis=0)
        dlt = jnp.maximum(b[h] - start, -HG_BASE_MAX_DECAY)
        qb16.append((q[h] * jnp.exp2(dlt)).astype(BF16))
        kb16.append((k[h] * jnp.exp2(-dlt)).astype(BF16))
        b_last.append(b_row(h, c - 1, c))
        qe.append((q[h] * jnp.exp2(b[h])).astype(BF16))
        kd.append((k[h] * jnp.exp2(b_last[h] - b[h])).astype(BF16))
        st.append(st_ref[h])

    nprod = nbig * heads
    a_l = [_dot_nt(qs_l[i], ks_l[i]) for i in range(nprod)]
    a16 = [_dot_nt(qb16[h], kb16[h]) for h in hs]
    o_inter = [_dot_nt(qe[h], st[h].astype(BF16)) for h in hs]
    for h in hs:
        st_ref[h] = st[h] * jnp.exp2(b_last[h][0:1, :]) + _dot_tn(v[h], kd[h])

    a_l = [(a_l[i] * cmask_ref[i // heads] if i // heads < nbig - 1 else a_l[i]).astype(BF16) for i in range(nprod)]
    a16 = [(a16[h] * bmask_ref[...]).astype(BF16) for h in hs]
    o_l = [_dot(a_l[i], vs_l[i]) for i in range(nprod)]
    oi_ref[...] = jnp.concatenate([_dot(a16[h], v[h]) for h in hs], axis=1)

    rest = []
    for h in hs:
        pieces = [None] * (c // HG_BASE)
        for lvi in range(nbig):
            m = HG_BASE << lvi
            per = m // HG_BASE
            o_c = o_l[lvi * heads + h]
            for a in range(c // (2 * m)):
                for u in range(per):
                    dst = (a * 2 * m + m) // HG_BASE + u
                    src = o_c[(a * per + u) * HG_BASE:(a * per + u + 1) * HG_BASE]
                    pieces[dst] = src if pieces[dst] is None else pieces[dst] + src
        zero_slab = jnp.zeros((HG_BASE, dh), F32)
        rest.append(o_inter[h] + jnp.concatenate([zero_slab if p is None else p for p in pieces], axis=0))
    rest_ref[...] = jnp.concatenate(rest, axis=1)

    for h in hs:
        @pl.when(jnp.logical_not(base_ok[h]))
        def _(h=h):
            rowi = lax.broadcasted_iota(I32, (c, dh), 0)
            sub = lax.broadcasted_iota(I32, (SUBLANES, dh), 0)
            ntile = c // SUBLANES
            forget = lanes(forget_all, h)
            attn = _dot_nt(q[h].astype(BF16), k[h].astype(BF16)) * mask_ref[0]
            for lv in range(base_lv):
                m = 1 << lv
                isq = (rowi & m) != 0
                if m == 1:
                    e = jnp.where(isq, forget, 1.0)
                else:
                    if m >= SUBLANES:
                        tiles = [b_row(h, (j * SUBLANES // (2 * m)) * 2 * m + m - 1, SUBLANES) for j in range(ntile)]
                    elif m == 4:
                        tiles = [b_row(h, j * SUBLANES + 3, SUBLANES) for j in range(ntile)]
                    else:
                        tiles = [jnp.where(sub < 4, b_row(h, j * SUBLANES + 1, SUBLANES),
                                           b_row(h, j * SUBLANES + 5, SUBLANES)) for j in range(ntile)]
                    mid = jnp.concatenate(tiles, axis=0)
                    e = jnp.exp2(jnp.where(isq, b[h] - mid, mid - b[h]))
                qt = jnp.where(isq, q[h] * e, 0.0).astype(BF16)
                kt = jnp.where(isq, 0.0, k[h] * e).astype(BF16)
                attn = attn + _dot_nt(qt, kt) * mask_ref[lv + 1]
            oi_ref[:, h * dh:(h + 1) * dh] = _dot(attn.astype(BF16), v[h])

    o = oi_ref[...] + rest_ref[...]
    o = jnp.concatenate(
        [lanes(o, h) * lax.rsqrt(jnp.mean(lanes(o, h) * lanes(o, h), axis=-1, keepdims=True) + RMS_EPS) for h in hs],
        axis=1)
    o = o * gain_ref[...] * _silu(g_ref[...].astype(F32))
    o_ref[...] = o.astype(o_ref.dtype)


def _hgrn_mixer(proj, lb_logits, head_gain, layer, j, bsz, seq):
    t, d4 = proj.shape
    d = d4 // 4
    nh = d // HG_HEAD_DIM
    heads = next(n for n in (4, 2, 1) if nh % n == 0)
    w = heads * HG_HEAD_DIM
    nhp = nh // heads
    chunk = 256 if seq % 256 == 0 else 128
    nc = seq // chunk
    nlev = int(math.log2(chunk))
    base_lv = int(math.log2(HG_BASE))

    def col(part):
        return lambda b, h, c: (b * nc + c, part * nhp + h)

    return pl.pallas_call(
        functools.partial(_hgrn_kernel, layer=layer, chunk=chunk, heads=heads),
        grid=(bsz, nhp, nc),
        in_specs=[
            pl.BlockSpec((chunk, w), col(0)),
            pl.BlockSpec((chunk, w), col(1)),
            pl.BlockSpec((chunk, w), col(2)),
            pl.BlockSpec((chunk, w), col(3)),
            pl.BlockSpec((lb_logits.shape[0], w), lambda b, h, c: (0, h)),
            pl.BlockSpec((None, 1, w), lambda b, h, c: (j, 0, h)),
        ],
        out_specs=pl.BlockSpec((chunk, w), lambda b, h, c: (b * nc + c, h)),
        out_shape=jax.ShapeDtypeStruct((t, d), BF16),
        scratch_shapes=[
            pltpu.VMEM((heads, HG_HEAD_DIM, HG_HEAD_DIM), F32),
            pltpu.VMEM((heads, chunk, HG_HEAD_DIM), F32),
            pltpu.VMEM((chunk, w), F32),
            pltpu.VMEM((chunk, w), F32),
            pltpu.VMEM((chunk, chunk), BF16),
            pltpu.VMEM((base_lv + 1, chunk, chunk), F32),
            pltpu.VMEM((chunk, chunk), F32),
            pltpu.VMEM((nlev - 1 - base_lv, chunk // 2, chunk // 2), F32),
        ],
        compiler_params=_cparams(3),
        name="hgrn2_mixer",
    )(proj, proj, proj, proj, lb_logits.astype(F32), head_gain.astype(F32).reshape(head_gain.shape[0], 1, d))


def _t5_bucket(dist):
    max_exact = N_BUCKETS // 2
    n = jnp.maximum(dist, 0)
    large = max_exact + (jnp.log(jnp.maximum(n, 1).astype(F32) / max_exact)
                         / math.log(MAX_DISTANCE / max_exact)
                         * (N_BUCKETS - max_exact)).astype(I32)
    large = jnp.minimum(large, N_BUCKETS - 1)
    return jnp.where(n < max_exact, n, large)


def _attn_kernel(rb_ref, sink_ref, q_ref, kp_ref, kc_ref, vp_ref, vc_ref, bucket_ref, o_ref,
                 bias_ref, *, group, layer_j):
    blk = ATT_BLOCK
    hd = ATT_HEAD_DIM
    pr = pl.program_id(1)
    i = pl.program_id(2)
    heads_per_step = 2 * group

    @pl.when(i == 0)
    def _():
        qi = lax.broadcasted_iota(I32, (blk, 2 * blk), 0)
        sj = lax.broadcasted_iota(I32, (blk, 2 * blk), 1)
        dist = qi + blk - sj
        band = (dist >= 0) & (dist < WINDOW)
        bucket = bucket_ref[...]

        def per_head(hh, carry):
            h = pr * heads_per_step + hh
            tbl = jnp.zeros((blk, 2 * blk), F32)
            for bk in range(N_BUCKETS):
                tbl = jnp.where(bucket == bk, rb_ref[bk, h], tbl)
            bias_ref[hh] = jnp.where(band, tbl, NEG_INF)
            bias_ref[heads_per_step + hh] = jnp.where(band & (sj >= blk), tbl, NEG_INF)
            return carry

        lax.fori_loop(0, heads_per_step, per_head, 0)

    lane = lax.broadcasted_iota(I32, (2 * blk, LANES), 1)
    table0 = jnp.where(i == 0, heads_per_step, 0)

    kk = jnp.concatenate([kp_ref[...], kc_ref[...]], axis=0).astype(F32)
    vv = jnp.concatenate([vp_ref[...], vc_ref[...]], axis=0).astype(F32)
    q_all = q_ref[...] * (hd ** -0.5)

    kbds, vbds = [], []
    for c in range(2):
        if c == 0:
            klo = jnp.where(lane < hd, kk, 0.0)
            khi = pltpu.roll(klo, hd, axis=1)
            vlo = jnp.where(lane < hd, vv, 0.0)
            vhi = pltpu.roll(vlo, hd, axis=1)
        else:
            khi = jnp.where(lane >= hd, kk, 0.0)
            klo = pltpu.roll(khi, hd, axis=1)
            vhi = jnp.where(lane >= hd, vv, 0.0)
            vlo = pltpu.roll(vhi, hd, axis=1)
        kbds.append(jnp.concatenate([klo, khi], axis=0).astype(BF16))
        vbds.append(jnp.concatenate([vlo, vhi], axis=0).astype(BF16))

    half = group // 2
    tiles = [(c, p) for c in range(2) for p in range(half)]
    lgs = [_dot_nt(q_all[:, (c * half + p) * LANES:(c * half + p + 1) * LANES], kbds[c]) for c, p in tiles]
    p2s, rinvs = [], []
    for (c, p), lg in zip(tiles, lgs):
        probs, rinv = [], []
        for hh in range(2):
            hl = c * group + 2 * p + hh
            sk = sink_ref[layer_j, pr * heads_per_step + hl]
            l = lg[:, hh * 2 * blk:(hh + 1) * 2 * blk] + bias_ref[table0 + hl]
            mx = jnp.maximum(jnp.max(l, axis=-1, keepdims=True), sk)
            pe = jnp.exp(l - mx)
            rinv.append(1.0 / (jnp.sum(pe, axis=-1, keepdims=True) + jnp.exp(sk - mx)))
            probs.append(pe.astype(BF16))
        p2s.append(jnp.concatenate(probs, axis=1))
        rinvs.append(rinv)
    lane_o = lax.broadcasted_iota(I32, (blk, LANES), 1)
    for (c, p), p2, rinv in zip(tiles, p2s, rinvs):
        tile = c * half + p
        o = _dot(p2, vbds[c]) * jnp.where(lane_o < hd, rinv[0], rinv[1])
        o_ref[:, tile * LANES:(tile + 1) * LANES] = o.astype(o_ref.dtype)


def _attn_mixer(proj, sinks, rel_bias, layer_j, bsz, seq, d):
    t, att_in = proj.shape
    kvw = (att_in - d) // 2
    n_heads = d // ATT_HEAD_DIM
    kvh = kvw // ATT_HEAD_DIM
    group = n_heads // kvh
    assert kvh % 2 == 0 and group % 2 == 0
    blk = ATT_BLOCK
    nb = seq // blk
    npair = kvh // 2
    qw = 2 * group * ATT_HEAD_DIM
    k0 = d // LANES
    v0 = (d + kvw) // LANES

    qi = jnp.arange(blk)[:, None]
    sj = jnp.arange(2 * blk)[None, :]
    bucket = _t5_bucket(qi + blk - sj).astype(I32)

    def prev(i):
        return jnp.maximum(i - 1, 0)

    grid_spec = pltpu.PrefetchScalarGridSpec(
        num_scalar_prefetch=2,
        grid=(bsz, npair, nb),
        in_specs=[
            pl.BlockSpec((blk, qw), lambda b, p, i, *_: (b * nb + i, p)),
            pl.BlockSpec((blk, LANES), lambda b, p, i, *_: (b * nb + prev(i), k0 + p)),
            pl.BlockSpec((blk, LANES), lambda b, p, i, *_: (b * nb + i, k0 + p)),
            pl.BlockSpec((blk, LANES), lambda b, p, i, *_: (b * nb + prev(i), v0 + p)),
            pl.BlockSpec((blk, LANES), lambda b, p, i, *_: (b * nb + i, v0 + p)),
            pl.BlockSpec((blk, 2 * blk), lambda b, p, i, *_: (0, 0)),
        ],
        out_specs=pl.BlockSpec((blk, qw), lambda b, p, i, *_: (b * nb + i, p)),
        scratch_shapes=[pltpu.VMEM((4 * group, blk, 2 * blk), F32)],
    )
    return pl.pallas_call(
        functools.partial(_attn_kernel, group=group, layer_j=layer_j),
        grid_spec=grid_spec,
        out_shape=jax.ShapeDtypeStruct((t, d), BF16),
        compiler_params=_cparams(3),
        name="swa_sink_mixer",
    )(rel_bias.astype(F32), sinks.astype(F32), proj, proj, proj, proj, proj, bucket)


def _layer_norm_rows(z, g, b):
    mu = jnp.mean(z, axis=-1, keepdims=True)
    zc = z - mu
    var = jnp.mean(zc * zc, axis=-1, keepdims=True)
    return zc * lax.rsqrt(var + LN_EPS) * g + b


def _ln_router_kernel(x_ref, y_ref, gate_ref, sc_ref, sh_ref, lng_ref, lnb_ref, wr_ref, br_ref,
                      xo_ref, hf_ref, meta_ref, cnt_ref, carry_ref, wsplit_ref, *, alpha, n_groups, n_experts):
    i = pl.program_id(0)
    bm = x_ref.shape[0]
    epg = n_experts // n_groups

    @pl.when(i == 0)
    def _():
        carry_ref[...] = jnp.zeros_like(carry_ref)
        wh, wl = _split_bf16(wr_ref[...], 2)
        wsplit_ref[:, 0:LANES] = wh
        wsplit_ref[:, LANES:2 * LANES] = wl

    z = alpha * x_ref[...] + (1.0 + gate_ref[...]) * y_ref[...]
    xn = _layer_norm_rows(z, lng_ref[...], lnb_ref[...])
    xo_ref[...] = xn
    hf = xn * (1.0 + sc_ref[...]) + sh_ref[...]
    hf_ref[...] = _pack_bf16_pairs(hf)

    xh, xl = _split_bf16(hf, 2)
    hh = _dot(xh, wsplit_ref[...])
    lg = hh[:, 0:LANES] + hh[:, LANES:2 * LANES] + _dot(xl, wsplit_ref[:, 0:LANES]) + br_ref[...]

    lane = lax.broadcasted_iota(I32, (bm, LANES), 1)
    lanef = lane.astype(F32)
    big = float(LANES)

    gl = jnp.where(lane < n_groups, lg, NEG_INF)
    gmax = jnp.max(gl, axis=-1, keepdims=True)
    gsel = jnp.min(jnp.where(gl == gmax, lanef, big), axis=-1, keepdims=True)
    p_group = 1.0 / jnp.sum(jnp.exp(gl - gmax), axis=-1, keepdims=True)

    lo = n_groups + gsel * epg
    el = jnp.where((lanef >= lo) & (lanef < lo + epg), lg, NEG_INF)
    m1 = jnp.max(el, axis=-1, keepdims=True)
    i1 = jnp.min(jnp.where(el == m1, lanef, big), axis=-1, keepdims=True)
    el2 = jnp.where(lanef == i1, NEG_INF, el)
    m2 = jnp.max(el2, axis=-1, keepdims=True)
    i2 = jnp.min(jnp.where(el2 == m2, lanef, big), axis=-1, keepdims=True)
    e21 = jnp.exp(m2 - m1)
    g0 = p_group / (1.0 + e21)
    g1 = g0 * e21

    oh0 = lanef == i1
    oh1 = lanef == i2
    cnt = jnp.where(oh0 | oh1, 1.0, 0.0)
    row = lax.broadcasted_iota(I32, (bm, bm), 0)
    col = lax.broadcasted_iota(I32, (bm, bm), 1)
    stril = jnp.where(row > col, 1.0, 0.0).astype(BF16)
    before = _dot(stril, cnt.astype(BF16)) + carry_ref[...]
    r0 = jnp.sum(jnp.where(oh0, before, 0.0), axis=-1, keepdims=True)
    r1 = jnp.sum(jnp.where(oh1, before, 0.0), axis=-1, keepdims=True)
    carry_ref[...] = carry_ref[...] + jnp.sum(cnt, axis=0, keepdims=True)
    cnt_ref[...] = carry_ref[...]

    meta = jnp.where(lane == 0, i1 - n_groups, 0.0)
    meta = jnp.where(lane == 1, i2 - n_groups, meta)
    meta = jnp.where(lane == 2, g0, meta)
    meta = jnp.where(lane == 3, g1, meta)
    meta = jnp.where(lane == 4, r0, meta)
    meta = jnp.where(lane == 5, r1, meta)
    meta_ref[...] = meta


def _ln_router(x2, y2, mod, row_of, ln_g, ln_b, w_router, b_router, alpha, n_groups, n_experts, seq):
    t, d = x2.shape
    bm = 128
    nbs = seq // bm

    def mrow(which):
        return lambda i: (row_of(i // nbs, which), 0, 0)

    return pl.pallas_call(
        functools.partial(_ln_router_kernel, alpha=alpha, n_groups=n_groups, n_experts=n_experts),
        grid=(t // bm,),
        in_specs=[
            pl.BlockSpec((bm, d), lambda i: (i, 0)),
            pl.BlockSpec((bm, d), lambda i: (i, 0)),
            pl.BlockSpec((None, 1, d), mrow(2)),
            pl.BlockSpec((None, 1, d), mrow(4)),
            pl.BlockSpec((None, 1, d), mrow(3)),
            pl.BlockSpec((1, d), lambda i: (0, 0)),
            pl.BlockSpec((1, d), lambda i: (0, 0)),
            pl.BlockSpec((d, LANES), lambda i: (0, 0)),
            pl.BlockSpec((1, LANES), lambda i: (0, 0)),
        ],
        out_specs=[
            pl.BlockSpec((bm, d), lambda i: (i, 0)),
            pl.BlockSpec((bm, d // 2), lambda i: (i, 0)),
            pl.BlockSpec((bm, LANES), lambda i: (i, 0)),
            pl.BlockSpec((1, LANES), lambda i: (0, 0)),
        ],
        out_shape=[
            jax.ShapeDtypeStruct((t, d), F32),
            jax.ShapeDtypeStruct((t, d // 2), jnp.uint32),
            jax.ShapeDtypeStruct((t, LANES), F32),
            jax.ShapeDtypeStruct((1, LANES), F32),
        ],
        scratch_shapes=[pltpu.VMEM((1, LANES), F32), pltpu.VMEM((d, 2 * LANES), BF16)],
        compiler_params=_cparams(1),
        name="ln_router",
    )(x2, y2, mod, mod, mod, ln_g, ln_b, w_router, b_router)


MOE_GATHER_SLOTS = 3


def _cast_rows(src_ref, dst_ref, rows=128):
    def body(r, carry):
        sl = pl.ds(pl.multiple_of(r * rows, rows), rows)
        dst_ref[sl, :] = src_ref[sl, :].astype(dst_ref.dtype)
        return carry
    lax.fori_loop(0, src_ref.shape[0] // rows, body, 0)


def _moe_kernel(blk_e_ref, nxt_e_ref, first_ref, nused_ref,
                tok0_ref, tok1_ref, tok2_ref, hf_hbm, wgu_hbm, wdn_hbm, y_ref,
                xbuf, wgu_st, wdn_st, wgu_bf, wdn_bf, gsem, wsem, *, layer, d_expert):
    i = pl.program_id(0)
    bm = xbuf.shape[1]
    nused = nused_ref[0]
    slot = i % MOE_GATHER_SLOTS

    def row_copy(tok, r, s):
        return pltpu.make_async_copy(hf_hbm.at[pl.ds(tok, 1), :], xbuf.at[s, pl.ds(r, 1), :], gsem.at[s])

    def issue_rows(tok_ref, s):
        for r in range(bm):
            row_copy(tok_ref[0, r], r, s).start()

    def wait_rows(s):
        pltpu.make_async_copy(hf_hbm.at[pl.ds(0, bm), :], xbuf.at[s], gsem.at[s]).wait()

    def weight_copies(e):
        return (pltpu.make_async_copy(wgu_hbm.at[layer, e], wgu_st, wsem.at[0]),
                pltpu.make_async_copy(wdn_hbm.at[layer, e], wdn_st, wsem.at[1]))

    @pl.when(i == 0)
    def _():
        issue_rows(tok0_ref, 0)
        for cp in weight_copies(blk_e_ref[0]):
            cp.start(priority=1)

    @pl.when((i == 0) & (nused > 1))
    def _():
        issue_rows(tok1_ref, 1)

    @pl.when((i < nused) & (first_ref[i] == 1))
    def _():
        for cp in weight_copies(blk_e_ref[i]):
            cp.wait()
        _cast_rows(wgu_st, wgu_bf)
        _cast_rows(wdn_st, wdn_bf)

        @pl.when(nxt_e_ref[i] >= 0)
        def _():
            for cp in weight_copies(nxt_e_ref[i]):
                cp.start(priority=1)

    @pl.when(i + 2 < nused)
    def _():
        issue_rows(tok2_ref, (i + 2) % MOE_GATHER_SLOTS)

    @pl.when(i < nused)
    def _():
        wait_rows(slot)
        xb = _unpack_bf16_pairs(xbuf[slot]).astype(BF16)
        a = _dot(xb, wgu_bf[...])
        h = _silu(a[:, :d_expert]) * a[:, d_expert:]
        y_ref[...] = _pack_bf16_pairs(_dot(h.astype(BF16), wdn_bf[...]))

    @pl.when(i >= nused)
    def _():
        y_ref[...] = jnp.zeros_like(y_ref)


def _moe_experts(hf, tok_pad, blk_e, nxt_e, first, nused, w_gate_up, w_down, layer, bm):
    d = w_down.shape[3]
    nb = blk_e.shape[0]
    d_expert = w_down.shape[2]
    tok3 = tok_pad.reshape(nb, 1, bm)
    grid_spec = pltpu.PrefetchScalarGridSpec(
        num_scalar_prefetch=4,
        grid=(nb,),
        in_specs=[
            pl.BlockSpec((None, 1, bm), lambda i, *_: (i, 0, 0), memory_space=pltpu.SMEM),
            pl.BlockSpec((None, 1, bm), lambda i, *_: (jnp.minimum(i + 1, nb - 1), 0, 0), memory_space=pltpu.SMEM),
            pl.BlockSpec((None, 1, bm), lambda i, *_: (jnp.minimum(i + 2, nb - 1), 0, 0), memory_space=pltpu.SMEM),
            pl.BlockSpec(memory_space=pl.ANY),
            pl.BlockSpec(memory_space=pl.ANY),
            pl.BlockSpec(memory_space=pl.ANY),
        ],
        out_specs=pl.BlockSpec((bm, d // 2), lambda i, *_: (i, 0)),
        scratch_shapes=[
            pltpu.VMEM((MOE_GATHER_SLOTS, bm, d // 2), jnp.uint32),
            pltpu.VMEM((d, 2 * d_expert), F32),
            pltpu.VMEM((d_expert, d), F32),
            pltpu.VMEM((d, 2 * d_expert), BF16),
            pltpu.VMEM((d_expert, d), BF16),
            pltpu.SemaphoreType.DMA((MOE_GATHER_SLOTS,)),
            pltpu.SemaphoreType.DMA((2,)),
        ],
    )
    return pl.pallas_call(
        functools.partial(_moe_kernel, layer=layer, d_expert=d_expert),
        grid_spec=grid_spec,
        out_shape=jax.ShapeDtypeStruct((nb * bm, d // 2), jnp.uint32),
        compiler_params=_cparams(1),
        name="moe_experts",
    )(blk_e, nxt_e, first, nused, tok3, tok3, tok3, hf, w_gate_up, w_down)


def _moe_plan(meta, counts_row, n_groups, n_experts, bm):
    t = meta.shape[0]
    eid = meta[:, 0:2].astype(I32)
    rank = meta[:, 4:6].astype(I32)
    counts = counts_row[0, n_groups:n_groups + n_experts].astype(I32)
    padded = (counts + bm - 1) // bm * bm
    pad_end = jnp.cumsum(padded)
    pad_start = pad_end - padded
    dest = pad_start[eid] + rank
    nb = (2 * t) // bm + n_experts
    nused = pad_end[-1] // bm
    ids = jnp.arange(nb, dtype=I32)
    raw_e = jnp.minimum(jnp.sum((pad_end[None, :] <= (ids * bm)[:, None]).astype(I32), axis=1), n_experts - 1)
    used = ids < nused
    blk_e = jnp.where(used, raw_e, raw_e[nused - 1])
    prev_e = jnp.concatenate([jnp.full((1,), -1, I32), blk_e[:-1]])
    first = (used & (blk_e != prev_e)).astype(I32)
    key = jnp.where(used, blk_e, n_experts)
    nxt_idx = jnp.sum((key[None, :] <= blk_e[:, None]).astype(I32), axis=1)
    nxt_e = jnp.where(nxt_idx < nused, key[jnp.minimum(nxt_idx, nb - 1)], -1).astype(I32)
    tok = jnp.repeat(jnp.arange(t, dtype=I32), 2)
    tok_pad = jnp.zeros((nb * bm,), I32).at[dest.reshape(-1)].set(tok)
    return dest, tok_pad, blk_e, nxt_e, first, nused.reshape(1).astype(I32)


def _ln_combine_kernel(*refs, alpha, with_next):
    if with_next:
        (dc_ref, dn_ref, x_ref, meta_ref, gate_ref, lng_ref, lnb_ref, sc_ref, sh_ref, y_hbm,
         xo_ref, hm_ref, ybuf, sem) = refs
    else:
        (dc_ref, dn_ref, x_ref, meta_ref, gate_ref, lng_ref, lnb_ref, y_hbm,
         xo_ref, ybuf, sem) = refs
    i = pl.program_id(0)
    n = pl.num_programs(0)
    bm = x_ref.shape[0]
    slot = i % 2

    def row_copy(src, r, s):
        return pltpu.make_async_copy(y_hbm.at[pl.ds(src, 1), :], ybuf.at[s, pl.ds(r, 1), :], sem.at[s])

    def issue_rows(d_ref, s):
        for r in range(2 * bm):
            row_copy(d_ref[0, r], r, s).start()

    def wait_rows(s):
        pltpu.make_async_copy(y_hbm.at[pl.ds(0, 2 * bm), :], ybuf.at[s], sem.at[s]).wait()

    @pl.when(i == 0)
    def _():
        issue_rows(dc_ref, 0)

    @pl.when(i + 1 < n)
    def _():
        issue_rows(dn_ref, 1 - slot)

    wait_rows(slot)
    meta = meta_ref[...]
    y = (_unpack_bf16_pairs(ybuf[slot, 0:bm, :]) * meta[:, 2:3]
         + _unpack_bf16_pairs(ybuf[slot, bm:2 * bm, :]) * meta[:, 3:4])
    z = alpha * x_ref[...] + (1.0 + gate_ref[...]) * y
    xn = _layer_norm_rows(z, lng_ref[...], lnb_ref[...])
    xo_ref[...] = xn
    if with_next:
        hm_ref[...] = (xn * (1.0 + sc_ref[...]) + sh_ref[...]).astype(hm_ref.dtype)


def _ln_combine(x2, meta, dest, ysort, mod, row_of, next_row_of, ln_g, ln_b, alpha, seq):
    t, d = x2.shape
    bm = 128
    nbs = seq // bm
    nblk = t // bm
    with_next = next_row_of is not None
    dest3 = dest.reshape(nblk, bm, 2).transpose(0, 2, 1).reshape(nblk, 1, 2 * bm)

    def mrow(fn, which):
        return lambda i: (fn(i // nbs, which), 0, 0)

    in_specs = [
        pl.BlockSpec((None, 1, 2 * bm), lambda i: (i, 0, 0), memory_space=pltpu.SMEM),
        pl.BlockSpec((None, 1, 2 * bm), lambda i: (jnp.minimum(i + 1, nblk - 1), 0, 0), memory_space=pltpu.SMEM),
        pl.BlockSpec((bm, d), lambda i: (i, 0)),
        pl.BlockSpec((bm, LANES), lambda i: (i, 0)),
        pl.BlockSpec((None, 1, d), mrow(row_of, 5)),
        pl.BlockSpec((1, d), lambda i: (0, 0)),
        pl.BlockSpec((1, d), lambda i: (0, 0)),
    ]
    args = [dest3, dest3, x2, meta, mod, ln_g, ln_b]
    out_specs = [pl.BlockSpec((bm, d), lambda i: (i, 0))]
    out_shape = [jax.ShapeDtypeStruct((t, d), F32)]
    if with_next:
        in_specs += [pl.BlockSpec((None, 1, d), mrow(next_row_of, 1)),
                     pl.BlockSpec((None, 1, d), mrow(next_row_of, 0))]
        args += [mod, mod]
        out_specs.append(pl.BlockSpec((bm, d), lambda i: (i, 0)))
        out_shape.append(jax.ShapeDtypeStruct((t, d), BF16))
    in_specs.append(pl.BlockSpec(memory_space=pl.ANY))
    args.append(ysort)
    return pl.pallas_call(
        functools.partial(_ln_combine_kernel, alpha=alpha, with_next=with_next),
        grid=(nblk,),
        in_specs=in_specs,
        out_specs=out_specs,
        out_shape=out_shape,
        scratch_shapes=[pltpu.VMEM((2, 2 * bm, d // 2), jnp.uint32), pltpu.SemaphoreType.DMA((2,))],
        compiler_params=_cparams(1),
        name="ln_moe_combine",
    )(*args)


def kernel(x, c, w_ada, b_ada, ln_g, ln_b, w_in_a, lb_logits, head_gain_a, w_out_a, w_in_b, attn_sinks, w_out_b, rel_bias, w_router_group, b_router_group, w_router_expert, b_router_expert, w_gate_up, w_down):
    bsz, seq, d = x.shape
    depth = w_ada.shape[0]
    n_groups = w_router_group.shape[2]
    n_experts = w_router_expert.shape[2]
    alpha = (2 * depth) ** 0.25
    t = bsz * seq
    moe_bm = 128

    mod = _ada_modulation(c, w_ada, b_ada)

    def row_of_layer(layer):
        return lambda b, which: (layer * bsz + b) * 6 + which

    x2 = x.reshape(t, d).astype(F32)
    row0 = row_of_layer(0)
    hm = _modulate(x2, mod, lambda b: row0(b, 1), lambda b: row0(b, 0), bsz, seq)

    for layer in range(depth):
        row_of = row_of_layer(layer)
        j = layer // 2
        if layer % 2 == 0:
            proj = _matmul(hm, w_in_a, j, BF16)
            o = _hgrn_mixer(proj, lb_logits, head_gain_a, layer, j, bsz, seq)
            y = _matmul(o, w_out_a, j, F32)
        else:
            proj = _matmul(hm, w_in_b, j, BF16)
            o = _attn_mixer(proj, attn_sinks, rel_bias, j, bsz, seq, d)
            y = _matmul(o, w_out_b, j, F32)

        n_pad = LANES - n_groups - n_experts
        w_router = jnp.concatenate(
            [w_router_group[layer].astype(F32), w_router_expert[layer].astype(F32), jnp.zeros((d, n_pad), F32)], axis=1)
        b_router = jnp.concatenate(
            [b_router_group[layer].astype(F32), b_router_expert[layer].astype(F32), jnp.zeros((n_pad,), F32)]
        ).reshape(1, LANES)

        x2, hf, meta, counts = _ln_router(
            x2, y, mod, row_of, ln_g[layer, 0:1].astype(F32), ln_b[layer, 0:1].astype(F32),
            w_router, b_router, alpha, n_groups, n_experts, seq)
        dest, tok_pad, blk_e, nxt_e, first, nused = _moe_plan(meta, counts, n_groups, n_experts, moe_bm)
        ysort = _moe_experts(hf, tok_pad, blk_e, nxt_e, first, nused, w_gate_up, w_down, layer, moe_bm)
        next_row_of = row_of_layer(layer + 1) if layer + 1 < depth else None
        outs = _ln_combine(x2, meta, dest, ysort, mod, row_of, next_row_of,
                           ln_g[layer, 1:2].astype(F32), ln_b[layer, 1:2].astype(F32), alpha, seq)
        x2 = outs[0]
        if next_row_of is not None:
            hm = outs[1]

    return x2.reshape(bsz, seq, d).astype(x.dtype)
```

```python
import functools
import math

import jax
import jax.numpy as jnp
from jax import lax
from jax.experimental import pallas as pl
from jax.experimental.pallas import tpu as pltpu

F32 = jnp.float32
BF16 = jnp.bfloat16
I32 = jnp.int32

LANES = 128
SUBLANES = 8
V7X_VMEM_LIMIT_BYTES = 56 * 1024 * 1024

HG_HEAD_DIM = 128
ATT_HEAD_DIM = 64
ATT_BLOCK = 128
WINDOW = 128
N_BUCKETS = 32
MAX_DISTANCE = 128
LN_EPS = 1e-5
RMS_EPS = 1e-6
NEG_INF = float("-inf")


def _cparams(n_axes):
    return pltpu.CompilerParams(
        dimension_semantics=("arbitrary",) * n_axes,
        vmem_limit_bytes=V7X_VMEM_LIMIT_BYTES,
    )


def _sigmoid(x):
    return 0.5 * jnp.tanh(0.5 * x) + 0.5


def _silu(x):
    return x * _sigmoid(x)


def _dot_nt(a, b):
    return lax.dot_general(a, b, (((1,), (1,)), ((), ())), preferred_element_type=F32)


def _dot_tn(a, b):
    return lax.dot_general(a, b, (((0,), (0,)), ((), ())), preferred_element_type=F32)


def _dot(a, b):
    return jnp.dot(a, b, preferred_element_type=F32)


def _pack_bf16_pairs(x):
    n = x.shape[1] // 2
    bits = pltpu.bitcast(x.astype(BF16).astype(F32), jnp.uint32)
    return (bits[:, :n] >> 16) | (bits[:, n:] & jnp.uint32(0xFFFF0000))


def _unpack_bf16_pairs(w):
    lo = pltpu.bitcast(w << 16, F32)
    hi = pltpu.bitcast(w & jnp.uint32(0xFFFF0000), F32)
    return jnp.concatenate([lo, hi], axis=1)


def _split_bf16(x, parts):
    out = []
    r = x
    for _ in range(parts):
        h = r.astype(BF16)
        out.append(h)
        r = r - h.astype(F32)
    return out


def _ada_kernel(c_ref, w_ref, b_ref, o_ref):
    ca = _silu(c_ref[...]).astype(BF16)
    o_ref[...] = _dot(ca, w_ref[...].astype(BF16)) + b_ref[...]


def _ada_modulation(c, w_ada, b_ada):
    nl, d, n6 = w_ada.shape
    bsz = c.shape[0]
    rows = -(-bsz // SUBLANES) * SUBLANES
    c8 = jnp.zeros((rows, d), F32).at[:bsz].set(c.astype(F32))
    tn = 512
    out = pl.pallas_call(
        _ada_kernel,
        grid=(nl, n6 // tn),
        in_specs=[
            pl.BlockSpec((rows, d), lambda l, j: (0, 0)),
            pl.BlockSpec((None, d, tn), lambda l, j: (l, 0, j)),
            pl.BlockSpec((None, 1, tn), lambda l, j: (l, 0, j)),
        ],
        out_specs=pl.BlockSpec((None, rows, tn), lambda l, j: (l, 0, j)),
        out_shape=jax.ShapeDtypeStruct((nl, rows, n6), F32),
        compiler_params=_cparams(2),
        name="ada_modulation",
    )(c8, w_ada, b_ada.reshape(nl, 1, n6))
    return out[:, :bsz].reshape(nl * bsz * 6, 1, d)


def _modulate_kernel(x_ref, sc_ref, sh_ref, o_ref):
    o_ref[...] = (x_ref[...] * (1.0 + sc_ref[...]) + sh_ref[...]).astype(o_ref.dtype)


def _modulate(x2, mod, sc_row, sh_row, bsz, seq):
    t, d = x2.shape
    bs = min(512, seq)
    nbs = seq // bs
    return pl.pallas_call(
        _modulate_kernel,
        grid=(t // bs,),
        in_specs=[
            pl.BlockSpec((bs, d), lambda i: (i, 0)),
            pl.BlockSpec((None, 1, d), lambda i: (sc_row(i // nbs), 0, 0)),
            pl.BlockSpec((None, 1, d), lambda i: (sh_row(i // nbs), 0, 0)),
        ],
        out_specs=pl.BlockSpec((bs, d), lambda i: (i, 0)),
        out_shape=jax.ShapeDtypeStruct((t, d), BF16),
        compiler_params=_cparams(1),
        name="modulate",
    )(x2, mod, mod)


def _matmul_kernel(x_ref, w_hbm, o_ref, wbf_ref, stage_ref, sem, *, layer, bn, kc):
    j = pl.program_id(0)
    i = pl.program_id(1)
    nj = pl.num_programs(0)
    cur = j % 2

    def slab_copy(jb, c):
        return pltpu.make_async_copy(
            w_hbm.at[layer, pl.ds(c * kc, kc), pl.ds(jb * bn, bn)], stage_ref, sem.at[0])

    def cast_slab(c, buf):
        wbf_ref[buf, pl.ds(c * kc, kc), :] = stage_ref[...].astype(BF16)

    @pl.when((j == 0) & (i == 0))
    def _():
        def first_block(c, carry):
            cp = slab_copy(0, c)
            cp.start()
            cp.wait()
            cast_slab(pl.multiple_of(c, 1), 0)
            return carry
        lax.fori_loop(0, pl.num_programs(1), first_block, 0)

    @pl.when(j + 1 < nj)
    def _():
        slab_copy(j + 1, i).start()

    o_ref[...] = _dot(x_ref[...], wbf_ref[cur]).astype(o_ref.dtype)

    @pl.when(j + 1 < nj)
    def _():
        slab_copy(j + 1, i).wait()
        cast_slab(i, 1 - cur)


def _matmul(x, w3, layer, out_dtype):
    m, k = x.shape
    n = w3.shape[2]
    bm = min(1024, m)
    bn = 1024 if n % 1024 == 0 else 512
    steps = m // bm
    kc = k // steps
    assert k % steps == 0 and kc % SUBLANES == 0
    return pl.pallas_call(
        functools.partial(_matmul_kernel, layer=layer, bn=bn, kc=kc),
        grid=(n // bn, steps),
        in_specs=[
            pl.BlockSpec((bm, k), lambda j, i: (i, 0)),
            pl.BlockSpec(memory_space=pl.ANY),
        ],
        out_specs=pl.BlockSpec((bm, bn), lambda j, i: (i, j)),
        out_shape=jax.ShapeDtypeStruct((m, n), out_dtype),
        scratch_shapes=[
            pltpu.VMEM((2, k, bn), BF16),
            pltpu.VMEM((kc, bn), F32),
            pltpu.SemaphoreType.DMA((1,)),
        ],
        compiler_params=_cparams(2),
        name="dense_projection",
    )(x, w3)


HG_BASE = 16
HG_BASE_MAX_DECAY = 86.0


def _hgrn_kernel(q_ref, f_ref, v_ref, g_ref, lbl_ref, gain_ref, o_ref,
                 st_ref, b_ref, oi_ref, rest_ref, tri_ref, mask_ref, bmask_ref, cmask_ref, *, layer, chunk, heads):
    c = chunk
    hc = c // 2
    dh = HG_HEAD_DIM
    nlev = int(math.log2(c))
    base_lv = int(math.log2(HG_BASE))
    nbig = nlev - base_lv
    hs = range(heads)
    first = (pl.program_id(0) == 0) & (pl.program_id(1) == 0) & (pl.program_id(2) == 0)

    @pl.when(first)
    def _():
        row = lax.broadcasted_iota(I32, (c, c), 0)
        col = lax.broadcasted_iota(I32, (c, c), 1)
        tri_ref[...] = jnp.where(row >= col, 1.0, 0.0).astype(BF16)
        x = row ^ col
        mask_ref[0] = jnp.where(x == 0, 1.0, 0.0).astype(F32)
        for lv in range(1, base_lv + 1):
            mask_ref[lv] = jnp.where(x < (1 << lv), 1.0, 0.0).astype(F32)
        bmask_ref[...] = jnp.where((x < HG_BASE) & (row >= col), 1.0, 0.0).astype(F32)
        xh = lax.broadcasted_iota(I32, (hc, hc), 0) ^ lax.broadcasted_iota(I32, (hc, hc), 1)
        for lv in range(base_lv, nlev - 1):
            cmask_ref[lv - base_lv] = jnp.where(xh < (1 << lv), 1.0, 0.0).astype(F32)

    @pl.when(pl.program_id(2) == 0)
    def _():
        st_ref[...] = jnp.zeros_like(st_ref)

    lbl = lbl_ref[...]
    rows = [lbl[i:i + 1, :] for i in range(lbl.shape[0])]
    mx = functools.reduce(jnp.maximum, rows)
    es = [jnp.exp(r - mx) for r in rows]
    lb = functools.reduce(lambda a, b: a + b, es[:layer + 1]) / functools.reduce(lambda a, b: a + b, es)

    q_all = _silu(q_ref[...].astype(F32))
    forget_all = lb + (1.0 - lb) * _sigmoid(f_ref[...].astype(F32))
    k_all = 1.0 - forget_all
    v_all = v_ref[...]

    w = heads * dh
    bb = _dot(tri_ref[...], jnp.concatenate(_split_bf16(jnp.log2(forget_all), 3), axis=1))
    b_all = bb[:, 0:w] + bb[:, w:2 * w] + bb[:, 2 * w:3 * w]

    def lanes(x, h):
        return x[:, h * dh:(h + 1) * dh]

    for h in hs:
        b_ref[h] = lanes(b_all, h)

    q = [lanes(q_all, h) for h in hs]
    k = [lanes(k_all, h) for h in hs]
    v = [lanes(v_all, h) for h in hs]
    b = [lanes(b_all, h) for h in hs]

    def b_row(h, r, n):
        return jnp.broadcast_to(b_ref[h, pl.ds(r, 1), :], (n, dh))

    blk_i = lax.broadcasted_iota(I32, (c // HG_BASE, dh), 0)
    base_ok = []
    for h in hs:
        ends = b_ref[h, pl.ds(HG_BASE - 1, c // HG_BASE, stride=HG_BASE), :]
        drop = jnp.where(blk_i == 0, 0.0, pltpu.roll(ends, 1, axis=0)) - ends
        base_ok.append(jnp.max(drop) <= HG_BASE_MAX_DECAY)

    qs_l, ks_l, vs_l = [], [], []
    for lv in range(base_lv, nlev):
        m = 1 << lv
        n = 2 * m
        for h in hs:
            qs, ks, vs = [], [], []
            for a in range(c // n):
                mid = b_row(h, a * n + m - 1, m)
                qs.append(q[h][a * n + m:(a + 1) * n] * jnp.exp2(b[h][a * n + m:(a + 1) * n] - mid))
                ks.append(k[h][a * n:a * n + m] * jnp.exp2(mid - b[h][a * n:a * n + m]))
                vs.append(v[h][a * n:a * n + m])
            qs_l.append(jnp.concatenate(qs, axis=0).astype(BF16))
            ks_l.append(jnp.concatenate(ks, axis=0).astype(BF16))
            vs_l.append(jnp.concatenate(vs, axis=0))
    qb16, kb16, qe, kd, b_last, st = [], [], [], [], [], []
    for h in hs:
        start = jnp.concatenate(
            [jnp.zeros((HG_BASE, dh), F32)] + [b_row(h, j * HG_BASE - 1, HG_BASE) for j in range(1, c // HG_BASE)],
            axis=0)
        dlt = jnp.maximum(b[h] - start, -HG_BASE_MAX_DECAY)
        qb16.append((q[h] * jnp.exp2(dlt)).astype(BF16))
        kb16.append((k[h] * jnp.exp2(-dlt)).astype(BF16))
        b_last.append(b_row(h, c - 1, c))
        qe.append((q[h] * jnp.exp2(b[h])).astype(BF16))
        kd.append((k[h] * jnp.exp2(b_last[h] - b[h])).astype(BF16))
        st.append(st_ref[h])

    nprod = nbig * heads
    a_l = [_dot_nt(qs_l[i], ks_l[i]) for i in range(nprod)]
    a16 = [_dot_nt(qb16[h], kb16[h]) for h in hs]
    o_inter = [_dot_nt(qe[h], st[h].astype(BF16)) for h in hs]
    for h in hs:
        st_ref[h] = st[h] * jnp.exp2(b_last[h][0:1, :]) + _dot_tn(v[h], kd[h])

    a_l = [(a_l[i] * cmask_ref[i // heads] if i // heads < nbig - 1 else a_l[i]).astype(BF16) for i in range(nprod)]
    a16 = [(a16[h] * bmask_ref[...]).astype(BF16) for h in hs]
    o_l = [_dot(a_l[i], vs_l[i]) for i in range(nprod)]
    oi_ref[...] = jnp.concatenate([_dot(a16[h], v[h]) for h in hs], axis=1)

    rest = []
    for h in hs:
        pieces = [None] * (c // HG_BASE)
        for lvi in range(nbig):
            m = HG_BASE << lvi
            per = m // HG_BASE
            o_c = o_l[lvi * heads + h]
            for a in range(c // (2 * m)):
                for u in range(per):
                    dst = (a * 2 * m + m) // HG_BASE + u
                    src = o_c[(a * per + u) * HG_BASE:(a * per + u + 1) * HG_BASE]
                    pieces[dst] = src if pieces[dst] is None else pieces[dst] + src
        zero_slab = jnp.zeros((HG_BASE, dh), F32)
        rest.append(o_inter[h] + jnp.concatenate([zero_slab if p is None else p for p in pieces], axis=0))
    rest_ref[...] = jnp.concatenate(rest, axis=1)

    for h in hs:
        @pl.when(jnp.logical_not(base_ok[h]))
        def _(h=h):
            rowi = lax.broadcasted_iota(I32, (c, dh), 0)
            sub = lax.broadcasted_iota(I32, (SUBLANES, dh), 0)
            ntile = c // SUBLANES
            forget = lanes(forget_all, h)
            attn = _dot_nt(q[h].astype(BF16), k[h].astype(BF16)) * mask_ref[0]
            for lv in range(base_lv):
                m = 1 << lv
                isq = (rowi & m) != 0
                if m == 1:
                    e = jnp.where(isq, forget, 1.0)
                else:
                    if m >= SUBLANES:
                        tiles = [b_row(h, (j * SUBLANES // (2 * m)) * 2 * m + m - 1, SUBLANES) for j in range(ntile)]
                    elif m == 4:
                        tiles = [b_row(h, j * SUBLANES + 3, SUBLANES) for j in range(ntile)]
                    else:
                        tiles = [jnp.where(sub < 4, b_row(h, j * SUBLANES + 1, SUBLANES),
                                           b_row(h, j * SUBLANES + 5, SUBLANES)) for j in range(ntile)]
                    mid = jnp.concatenate(tiles, axis=0)
                    e = jnp.exp2(jnp.where(isq, b[h] - mid, mid - b[h]))
                qt = jnp.where(isq, q[h] * e, 0.0).astype(BF16)
                kt = jnp.where(isq, 0.0, k[h] * e).astype(BF16)
                attn = attn + _dot_nt(qt, kt) * mask_ref[lv + 1]
            oi_ref[:, h * dh:(h + 1) * dh] = _dot(attn.astype(BF16), v[h])

    o = oi_ref[...] + rest_ref[...]
    o = jnp.concatenate(
        [lanes(o, h) * lax.rsqrt(jnp.mean(lanes(o, h) * lanes(o, h), axis=-1, keepdims=True) + RMS_EPS) for h in hs],
        axis=1)
    o = o * gain_ref[...] * _silu(g_ref[...].astype(F32))
    o_ref[...] = o.astype(o_ref.dtype)


def _hgrn_mixer(proj, lb_logits, head_gain, layer, j, bsz, seq):
    t, d4 = proj.shape
    d = d4 // 4
    nh = d // HG_HEAD_DIM
    heads = next(n for n in (8, 4, 2, 1) if nh % n == 0)
    w = heads * HG_HEAD_DIM
    nhp = nh // heads
    chunk = 256 if seq % 256 == 0 else 128
    nc = seq // chunk
    nlev = int(math.log2(chunk))
    base_lv = int(math.log2(HG_BASE))

    def col(part):
        return lambda b, h, c: (b * nc + c, part * nhp + h)

    return pl.pallas_call(
        functools.partial(_hgrn_kernel, layer=layer, chunk=chunk, heads=heads),
        grid=(bsz, nhp, nc),
        in_specs=[
            pl.BlockSpec((chunk, w), col(0)),
            pl.BlockSpec((chunk, w), col(1)),
            pl.BlockSpec((chunk, w), col(2)),
            pl.BlockSpec((chunk, w), col(3)),
            pl.BlockSpec((lb_logits.shape[0], w), lambda b, h, c: (0, h)),
            pl.BlockSpec((None, 1, w), lambda b, h, c: (j, 0, h)),
        ],
        out_specs=pl.BlockSpec((chunk, w), lambda b, h, c: (b * nc + c, h)),
        out_shape=jax.ShapeDtypeStruct((t, d), BF16),
        scratch_shapes=[
            pltpu.VMEM((heads, HG_HEAD_DIM, HG_HEAD_DIM), F32),
            pltpu.VMEM((heads, chunk, HG_HEAD_DIM), F32),
            pltpu.VMEM((chunk, w), F32),
            pltpu.VMEM((chunk, w), F32),
            pltpu.VMEM((chunk, chunk), BF16),
            pltpu.VMEM((base_lv + 1, chunk, chunk), F32),
            pltpu.VMEM((chunk, chunk), F32),
            pltpu.VMEM((nlev - 1 - base_lv, chunk // 2, chunk // 2), F32),
        ],
        compiler_params=_cparams(3),
        name="hgrn2_mixer",
    )(proj, proj, proj, proj, lb_logits.astype(F32), head_gain.astype(F32).reshape(head_gain.shape[0], 1, d))


def _t5_bucket(dist):
    max_exact = N_BUCKETS // 2
    n = jnp.maximum(dist, 0)
    large = max_exact + (jnp.log(jnp.maximum(n, 1).astype(F32) / max_exact)
                         / math.log(MAX_DISTANCE / max_exact)
                         * (N_BUCKETS - max_exact)).astype(I32)
    large = jnp.minimum(large, N_BUCKETS - 1)
    return jnp.where(n < max_exact, n, large)


def _attn_kernel(rb_ref, sink_ref, q_ref, kp_ref, kc_ref, vp_ref, vc_ref, bucket_ref, o_ref,
                 bias_ref, *, group, pairs, layer_j):
    blk = ATT_BLOCK
    hd = ATT_HEAD_DIM
    pr = pl.program_id(0)
    i = pl.program_id(2)
    heads_per_step = 2 * group * pairs

    @pl.when((pl.program_id(1) == 0) & (i == 0))
    def _():
        qi = lax.broadcasted_iota(I32, (blk, 2 * blk), 0)
        sj = lax.broadcasted_iota(I32, (blk, 2 * blk), 1)
        dist = qi + blk - sj
        band = (dist >= 0) & (dist < WINDOW)
        bucket = bucket_ref[...]

        def per_head(hh, carry):
            h = pr * heads_per_step + hh
            tbl = jnp.zeros((blk, 2 * blk), F32)
            for bk in range(N_BUCKETS):
                tbl = jnp.where(bucket == bk, rb_ref[bk, h], tbl)
            bias_ref[hh] = jnp.where(band, tbl, NEG_INF)
            bias_ref[heads_per_step + hh] = jnp.where(band & (sj >= blk), tbl, NEG_INF)
            return carry

        lax.fori_loop(0, heads_per_step, per_head, 0)

    lane = lax.broadcasted_iota(I32, (2 * blk, LANES), 1)
    table0 = jnp.where(i == 0, heads_per_step, 0)

    q_all = q_ref[...] * (hd ** -0.5)

    kbds, vbds = [], []
    for pp in range(pairs):
        kk = jnp.concatenate([kp_ref[:, pp * LANES:(pp + 1) * LANES], kc_ref[:, pp * LANES:(pp + 1) * LANES]],
                             axis=0).astype(F32)
        vv = jnp.concatenate([vp_ref[:, pp * LANES:(pp + 1) * LANES], vc_ref[:, pp * LANES:(pp + 1) * LANES]],
                             axis=0).astype(F32)
        for c in range(2):
            if c == 0:
                klo = jnp.where(lane < hd, kk, 0.0)
                khi = pltpu.roll(klo, hd, axis=1)
                vlo = jnp.where(lane < hd, vv, 0.0)
                vhi = pltpu.roll(vlo, hd, axis=1)
            else:
                khi = jnp.where(lane >= hd, kk, 0.0)
                klo = pltpu.roll(khi, hd, axis=1)
                vhi = jnp.where(lane >= hd, vv, 0.0)
                vlo = pltpu.roll(vhi, hd, axis=1)
            kbds.append(jnp.concatenate([klo, khi], axis=0).astype(BF16))
            vbds.append(jnp.concatenate([vlo, vhi], axis=0).astype(BF16))

    half = group // 2
    tiles = [(c, p) for c in range(2 * pairs) for p in range(half)]
    lgs = [_dot_nt(q_all[:, (c * half + p) * LANES:(c * half + p + 1) * LANES], kbds[c]) for c, p in tiles]
    p2s, rinvs = [], []
    for (c, p), lg in zip(tiles, lgs):
        probs, rinv = [], []
        for hh in range(2):
            hl = c * group + 2 * p + hh
            sk = sink_ref[layer_j, pr * heads_per_step + hl]
            l = lg[:, hh * 2 * blk:(hh + 1) * 2 * blk] + bias_ref[table0 + hl]
            mx = jnp.maximum(jnp.max(l, axis=-1, keepdims=True), sk)
            pe = jnp.exp(l - mx)
            rinv.append(1.0 / (jnp.sum(pe, axis=-1, keepdims=True) + jnp.exp(sk - mx)))
            probs.append(pe.astype(BF16))
        p2s.append(jnp.concatenate(probs, axis=1))
        rinvs.append(rinv)
    lane_o = lax.broadcasted_iota(I32, (blk, LANES), 1)
    for (c, p), p2, rinv in zip(tiles, p2s, rinvs):
        tile = c * half + p
        o = _dot(p2, vbds[c]) * jnp.where(lane_o < hd, rinv[0], rinv[1])
        o_ref[:, tile * LANES:(tile + 1) * LANES] = o.astype(o_ref.dtype)


def _attn_mixer(proj, sinks, rel_bias, layer_j, bsz, seq, d):
    t, att_in = proj.shape
    kvw = (att_in - d) // 2
    n_heads = d // ATT_HEAD_DIM
    kvh = kvw // ATT_HEAD_DIM
    group = n_heads // kvh
    assert kvh % 2 == 0 and group % 2 == 0
    blk = ATT_BLOCK
    nb = seq // blk
    npair = kvh // 2
    pairs = 1
    ngrp = npair // pairs
    qw = 2 * group * ATT_HEAD_DIM * pairs
    kw = LANES * pairs
    k0 = d // kw
    v0 = (d + kvw) // kw
    assert d % kw == 0 and (d + kvw) % kw == 0

    qi = jnp.arange(blk)[:, None]
    sj = jnp.arange(2 * blk)[None, :]
    bucket = _t5_bucket(qi + blk - sj).astype(I32)

    def prev(i):
        return jnp.maximum(i - 1, 0)

    grid_spec = pltpu.PrefetchScalarGridSpec(
        num_scalar_prefetch=2,
        grid=(ngrp, bsz, nb),
        in_specs=[
            pl.BlockSpec((blk, qw), lambda p, b, i, *_: (b * nb + i, p)),
            pl.BlockSpec((blk, kw), lambda p, b, i, *_: (b * nb + prev(i), k0 + p)),
            pl.BlockSpec((blk, kw), lambda p, b, i, *_: (b * nb + i, k0 + p)),
            pl.BlockSpec((blk, kw), lambda p, b, i, *_: (b * nb + prev(i), v0 + p)),
            pl.BlockSpec((blk, kw), lambda p, b, i, *_: (b * nb + i, v0 + p)),
            pl.BlockSpec((blk, 2 * blk), lambda p, b, i, *_: (0, 0)),
        ],
        out_specs=pl.BlockSpec((blk, qw), lambda p, b, i, *_: (b * nb + i, p)),
        scratch_shapes=[pltpu.VMEM((4 * group * pairs, blk, 2 * blk), F32)],
    )
    return pl.pallas_call(
        functools.partial(_attn_kernel, group=group, pairs=pairs, layer_j=layer_j),
        grid_spec=grid_spec,
        out_shape=jax.ShapeDtypeStruct((t, d), BF16),
        compiler_params=_cparams(3),
        name="swa_sink_mixer",
    )(rel_bias.astype(F32), sinks.astype(F32), proj, proj, proj, proj, proj, bucket)


def _layer_norm_rows(z, g, b):
    mu = jnp.mean(z, axis=-1, keepdims=True)
    zc = z - mu
    var = jnp.mean(zc * zc, axis=-1, keepdims=True)
    return zc * lax.rsqrt(var + LN_EPS) * g + b


def _ln_router_kernel(x_ref, y_ref, gate_ref, sc_ref, sh_ref, lng_ref, lnb_ref, wr_ref, br_ref,
                      xo_ref, hf_ref, meta_ref, cnt_ref, carry_ref, wsplit_ref, *, alpha, n_groups, n_experts):
    i = pl.program_id(0)
    bm = x_ref.shape[0]
    epg = n_experts // n_groups

    @pl.when(i == 0)
    def _():
        carry_ref[...] = jnp.zeros_like(carry_ref)
        wh, wl = _split_bf16(wr_ref[...], 2)
        wsplit_ref[:, 0:LANES] = wh
        wsplit_ref[:, LANES:2 * LANES] = wl

    z = alpha * x_ref[...] + (1.0 + gate_ref[...]) * y_ref[...]
    xn = _layer_norm_rows(z, lng_ref[...], lnb_ref[...])
    xo_ref[...] = xn
    hf = xn * (1.0 + sc_ref[...]) + sh_ref[...]
    hf_ref[...] = _pack_bf16_pairs(hf)

    xh, xl = _split_bf16(hf, 2)
    hh = _dot(xh, wsplit_ref[...])
    lg = hh[:, 0:LANES] + hh[:, LANES:2 * LANES] + _dot(xl, wsplit_ref[:, 0:LANES]) + br_ref[...]

    lane = lax.broadcasted_iota(I32, (bm, LANES), 1)
    lanef = lane.astype(F32)
    big = float(LANES)

    gl = jnp.where(lane < n_groups, lg, NEG_INF)
    gmax = jnp.max(gl, axis=-1, keepdims=True)
    gsel = jnp.min(jnp.where(gl == gmax, lanef, big), axis=-1, keepdims=True)
    p_group = 1.0 / jnp.sum(jnp.exp(gl - gmax), axis=-1, keepdims=True)

    lo = n_groups + gsel * epg
    el = jnp.where((lanef >= lo) & (lanef < lo + epg), lg, NEG_INF)
    m1 = jnp.max(el, axis=-1, keepdims=True)
    i1 = jnp.min(jnp.where(el == m1, lanef, big), axis=-1, keepdims=True)
    el2 = jnp.where(lanef == i1, NEG_INF, el)
    m2 = jnp.max(el2, axis=-1, keepdims=True)
    i2 = jnp.min(jnp.where(el2 == m2, lanef, big), axis=-1, keepdims=True)
    e21 = jnp.exp(m2 - m1)
    g0 = p_group / (1.0 + e21)
    g1 = g0 * e21

    oh0 = lanef == i1
    oh1 = lanef == i2
    cnt = jnp.where(oh0 | oh1, 1.0, 0.0)
    row = lax.broadcasted_iota(I32, (bm, bm), 0)
    col = lax.broadcasted_iota(I32, (bm, bm), 1)
    stril = jnp.where(row > col, 1.0, 0.0).astype(BF16)
    before = _dot(stril, cnt.astype(BF16)) + carry_ref[...]
    r0 = jnp.sum(jnp.where(oh0, before, 0.0), axis=-1, keepdims=True)
    r1 = jnp.sum(jnp.where(oh1, before, 0.0), axis=-1, keepdims=True)
    carry_ref[...] = carry_ref[...] + jnp.sum(cnt, axis=0, keepdims=True)
    cnt_ref[...] = carry_ref[...]

    meta = jnp.where(lane == 0, i1 - n_groups, 0.0)
    meta = jnp.where(lane == 1, i2 - n_groups, meta)
    meta = jnp.where(lane == 2, g0, meta)
    meta = jnp.where(lane == 3, g1, meta)
    meta = jnp.where(lane == 4, r0, meta)
    meta = jnp.where(lane == 5, r1, meta)
    meta_ref[...] = meta


def _ln_router(x2, y2, mod, row_of, ln_g, ln_b, w_router, b_router, alpha, n_groups, n_experts, seq):
    t, d = x2.shape
    bm = 128
    nbs = seq // bm

    def mrow(which):
        return lambda i: (row_of(i // nbs, which), 0, 0)

    return pl.pallas_call(
        functools.partial(_ln_router_kernel, alpha=alpha, n_groups=n_groups, n_experts=n_experts),
        grid=(t // bm,),
        in_specs=[
            pl.BlockSpec((bm, d), lambda i: (i, 0)),
            pl.BlockSpec((bm, d), lambda i: (i, 0)),
            pl.BlockSpec((None, 1, d), mrow(2)),
            pl.BlockSpec((None, 1, d), mrow(4)),
            pl.BlockSpec((None, 1, d), mrow(3)),
            pl.BlockSpec((1, d), lambda i: (0, 0)),
            pl.BlockSpec((1, d), lambda i: (0, 0)),
            pl.BlockSpec((d, LANES), lambda i: (0, 0)),
            pl.BlockSpec((1, LANES), lambda i: (0, 0)),
        ],
        out_specs=[
            pl.BlockSpec((bm, d), lambda i: (i, 0)),
            pl.BlockSpec((bm, d // 2), lambda i: (i, 0)),
            pl.BlockSpec((bm, LANES), lambda i: (i, 0)),
            pl.BlockSpec((1, LANES), lambda i: (0, 0)),
        ],
        out_shape=[
            jax.ShapeDtypeStruct((t, d), F32),
            jax.ShapeDtypeStruct((t, d // 2), jnp.uint32),
            jax.ShapeDtypeStruct((t, LANES), F32),
            jax.ShapeDtypeStruct((1, LANES), F32),
        ],
        scratch_shapes=[pltpu.VMEM((1, LANES), F32), pltpu.VMEM((d, 2 * LANES), BF16)],
        compiler_params=_cparams(1),
        name="ln_router",
    )(x2, y2, mod, mod, mod, ln_g, ln_b, w_router, b_router)


MOE_GATHER_SLOTS = 3


def _cast_rows(src_ref, dst_ref, rows=128):
    def body(r, carry):
        sl = pl.ds(pl.multiple_of(r * rows, rows), rows)
        dst_ref[sl, :] = src_ref[sl, :].astype(dst_ref.dtype)
        return carry
    lax.fori_loop(0, src_ref.shape[0] // rows, body, 0)


def _moe_kernel(blk_e_ref, nxt_e_ref, first_ref, nused_ref,
                tok0_ref, tok1_ref, tok2_ref, hf_hbm, wgu_hbm, wdn_hbm, y_ref,
                xbuf, wgu_st, wdn_st, wgu_bf, wdn_bf, gsem, wsem, *, layer, d_expert):
    i = pl.program_id(0)
    bm = xbuf.shape[1]
    nused = nused_ref[0]
    slot = i % MOE_GATHER_SLOTS

    def row_copy(tok, r, s):
        return pltpu.make_async_copy(hf_hbm.at[pl.ds(tok, 1), :], xbuf.at[s, pl.ds(r, 1), :], gsem.at[s])

    def issue_rows(tok_ref, s):
        for r in range(bm):
            row_copy(tok_ref[0, r], r, s).start()

    def wait_rows(s):
        pltpu.make_async_copy(hf_hbm.at[pl.ds(0, bm), :], xbuf.at[s], gsem.at[s]).wait()

    def weight_copies(e):
        return (pltpu.make_async_copy(wgu_hbm.at[layer, e], wgu_st, wsem.at[0]),
                pltpu.make_async_copy(wdn_hbm.at[layer, e], wdn_st, wsem.at[1]))

    @pl.when(i == 0)
    def _():
        issue_rows(tok0_ref, 0)
        for cp in weight_copies(blk_e_ref[0]):
            cp.start(priority=1)

    @pl.when((i == 0) & (nused > 1))
    def _():
        issue_rows(tok1_ref, 1)

    @pl.when((i < nused) & (first_ref[i] == 1))
    def _():
        for cp in weight_copies(blk_e_ref[i]):
            cp.wait()
        _cast_rows(wgu_st, wgu_bf)
        _cast_rows(wdn_st, wdn_bf)

        @pl.when(nxt_e_ref[i] >= 0)
        def _():
            for cp in weight_copies(nxt_e_ref[i]):
                cp.start(priority=1)

    @pl.when(i + 2 < nused)
    def _():
        issue_rows(tok2_ref, (i + 2) % MOE_GATHER_SLOTS)

    @pl.when(i < nused)
    def _():
        wait_rows(slot)
        xb = _unpack_bf16_pairs(xbuf[slot]).astype(BF16)
        a = _dot(xb, wgu_bf[...])
        h = _silu(a[:, :d_expert]) * a[:, d_expert:]
        y_ref[...] = _pack_bf16_pairs(_dot(h.astype(BF16), wdn_bf[...]))

    @pl.when(i >= nused)
    def _():
        y_ref[...] = jnp.zeros_like(y_ref)


def _moe_experts(hf, tok_pad, blk_e, nxt_e, first, nused, w_gate_up, w_down, layer, bm):
    d = w_down.shape[3]
    nb = blk_e.shape[0]
    d_expert = w_down.shape[2]
    tok3 = tok_pad.reshape(nb, 1, bm)
    grid_spec = pltpu.PrefetchScalarGridSpec(
        num_scalar_prefetch=4,
        grid=(nb,),
        in_specs=[
            pl.BlockSpec((None, 1, bm), lambda i, *_: (i, 0, 0), memory_space=pltpu.SMEM),
            pl.BlockSpec((None, 1, bm), lambda i, *_: (jnp.minimum(i + 1, nb - 1), 0, 0), memory_space=pltpu.SMEM),
            pl.BlockSpec((None, 1, bm), lambda i, *_: (jnp.minimum(i + 2, nb - 1), 0, 0), memory_space=pltpu.SMEM),
            pl.BlockSpec(memory_space=pl.ANY),
            pl.BlockSpec(memory_space=pl.ANY),
            pl.BlockSpec(memory_space=pl.ANY),
        ],
        out_specs=pl.BlockSpec((bm, d // 2), lambda i, *_: (i, 0)),
        scratch_shapes=[
            pltpu.VMEM((MOE_GATHER_SLOTS, bm, d // 2), jnp.uint32),
            pltpu.VMEM((d, 2 * d_expert), F32),
            pltpu.VMEM((d_expert, d), F32),
            pltpu.VMEM((d, 2 * d_expert), BF16),
            pltpu.VMEM((d_expert, d), BF16),
            pltpu.SemaphoreType.DMA((MOE_GATHER_SLOTS,)),
            pltpu.SemaphoreType.DMA((2,)),
        ],
    )
    return pl.pallas_call(
        functools.partial(_moe_kernel, layer=layer, d_expert=d_expert),
        grid_spec=grid_spec,
        out_shape=jax.ShapeDtypeStruct((nb * bm, d // 2), jnp.uint32),
        compiler_params=_cparams(1),
        name="moe_experts",
    )(blk_e, nxt_e, first, nused, tok3, tok3, tok3, hf, w_gate_up, w_down)


def _moe_plan(meta, counts_row, n_groups, n_experts, bm):
    t = meta.shape[0]
    eid = meta[:, 0:2].astype(I32)
    rank = meta[:, 4:6].astype(I32)
    counts = counts_row[0, n_groups:n_groups + n_experts].astype(I32)
    padded = (counts + bm - 1) // bm * bm
    pad_end = jnp.cumsum(padded)
    pad_start = pad_end - padded
    dest = pad_start[eid] + rank
    nb = (2 * t) // bm + n_experts
    nused = pad_end[-1] // bm
    ids = jnp.arange(nb, dtype=I32)
    raw_e = jnp.minimum(jnp.sum((pad_end[None, :] <= (ids * bm)[:, None]).astype(I32), axis=1), n_experts - 1)
    used = ids < nused
    blk_e = jnp.where(used, raw_e, raw_e[nused - 1])
    prev_e = jnp.concatenate([jnp.full((1,), -1, I32), blk_e[:-1]])
    first = (used & (blk_e != prev_e)).astype(I32)
    key = jnp.where(used, blk_e, n_experts)
    nxt_idx = jnp.sum((key[None, :] <= blk_e[:, None]).astype(I32), axis=1)
    nxt_e = jnp.where(nxt_idx < nused, key[jnp.minimum(nxt_idx, nb - 1)], -1).astype(I32)
    tok = jnp.repeat(jnp.arange(t, dtype=I32), 2)
    tok_pad = jnp.zeros((nb * bm,), I32).at[dest.reshape(-1)].set(tok)
    return dest, tok_pad, blk_e, nxt_e, first, nused.reshape(1).astype(I32)


def _ln_combine_kernel(*refs, alpha, with_next):
    if with_next:
        (dc_ref, dn_ref, x_ref, meta_ref, gate_ref, lng_ref, lnb_ref, sc_ref, sh_ref, y_hbm,
         xo_ref, hm_ref, ybuf, sem) = refs
    else:
        (dc_ref, dn_ref, x_ref, meta_ref, gate_ref, lng_ref, lnb_ref, y_hbm,
         xo_ref, ybuf, sem) = refs
    i = pl.program_id(0)
    n = pl.num_programs(0)
    bm = x_ref.shape[0]
    slot = i % 2

    def row_copy(src, r, s):
        return pltpu.make_async_copy(y_hbm.at[pl.ds(src, 1), :], ybuf.at[s, pl.ds(r, 1), :], sem.at[s])

    def issue_rows(d_ref, s):
        for r in range(2 * bm):
            row_copy(d_ref[0, r], r, s).start()

    def wait_rows(s):
        pltpu.make_async_copy(y_hbm.at[pl.ds(0, 2 * bm), :], ybuf.at[s], sem.at[s]).wait()

    @pl.when(i == 0)
    def _():
        issue_rows(dc_ref, 0)

    @pl.when(i + 1 < n)
    def _():
        issue_rows(dn_ref, 1 - slot)

    wait_rows(slot)
    meta = meta_ref[...]
    y = (_unpack_bf16_pairs(ybuf[slot, 0:bm, :]) * meta[:, 2:3]
         + _unpack_bf16_pairs(ybuf[slot, bm:2 * bm, :]) * meta[:, 3:4])
    z = alpha * x_ref[...] + (1.0 + gate_ref[...]) * y
    xn = _layer_norm_rows(z, lng_ref[...], lnb_ref[...])
    xo_ref[...] = xn
    if with_next:
        hm_ref[...] = (xn * (1.0 + sc_ref[...]) + sh_ref[...]).astype(hm_ref.dtype)


def _ln_combine(x2, meta, dest, ysort, mod, row_of, next_row_of, ln_g, ln_b, alpha, seq):
    t, d = x2.shape
    bm = 128
    nbs = seq // bm
    nblk = t // bm
    with_next = next_row_of is not None
    dest3 = dest.reshape(nblk, bm, 2).transpose(0, 2, 1).reshape(nblk, 1, 2 * bm)

    def mrow(fn, which):
        return lambda i: (fn(i // nbs, which), 0, 0)

    in_specs = [
        pl.BlockSpec((None, 1, 2 * bm), lambda i: (i, 0, 0), memory_space=pltpu.SMEM),
        pl.BlockSpec((None, 1, 2 * bm), lambda i: (jnp.minimum(i + 1, nblk - 1), 0, 0), memory_space=pltpu.SMEM),
        pl.BlockSpec((bm, d), lambda i: (i, 0)),
        pl.BlockSpec((bm, LANES), lambda i: (i, 0)),
        pl.BlockSpec((None, 1, d), mrow(row_of, 5)),
        pl.BlockSpec((1, d), lambda i: (0, 0)),
        pl.BlockSpec((1, d), lambda i: (0, 0)),
    ]
    args = [dest3, dest3, x2, meta, mod, ln_g, ln_b]
    out_specs = [pl.BlockSpec((bm, d), lambda i: (i, 0))]
    out_shape = [jax.ShapeDtypeStruct((t, d), F32)]
    if with_next:
        in_specs += [pl.BlockSpec((None, 1, d), mrow(next_row_of, 1)),
                     pl.BlockSpec((None, 1, d), mrow(next_row_of, 0))]
        args += [mod, mod]
        out_specs.append(pl.BlockSpec((bm, d), lambda i: (i, 0)))
        out_shape.append(jax.ShapeDtypeStruct((t, d), BF16))
    in_specs.append(pl.BlockSpec(memory_space=pl.ANY))
    args.append(ysort)
    return pl.pallas_call(
        functools.partial(_ln_combine_kernel, alpha=alpha, with_next=with_next),
        grid=(nblk,),
        in_specs=in_specs,
        out_specs=out_specs,
        out_shape=out_shape,
        scratch_shapes=[pltpu.VMEM((2, 2 * bm, d // 2), jnp.uint32), pltpu.SemaphoreType.DMA((2,))],
        compiler_params=_cparams(1),
        name="ln_moe_combine",
    )(*args)


def kernel(x, c, w_ada, b_ada, ln_g, ln_b, w_in_a, lb_logits, head_gain_a, w_out_a, w_in_b, attn_sinks, w_out_b, rel_bias, w_router_group, b_router_group, w_router_expert, b_router_expert, w_gate_up, w_down):
    bsz, seq, d = x.shape
    depth = w_ada.shape[0]
    n_groups = w_router_group.shape[2]
    n_experts = w_router_expert.shape[2]
    alpha = (2 * depth) ** 0.25
    t = bsz * seq
    moe_bm = 128

    mod = _ada_modulation(c, w_ada, b_ada)

    def row_of_layer(layer):
        return lambda b, which: (layer * bsz + b) * 6 + which

    x2 = x.reshape(t, d).astype(F32)
    row0 = row_of_layer(0)
    hm = _modulate(x2, mod, lambda b: row0(b, 1), lambda b: row0(b, 0), bsz, seq)

    for layer in range(depth):
        row_of = row_of_layer(layer)
        j = layer // 2
        if layer % 2 == 0:
            proj = _matmul(hm, w_in_a, j, BF16)
            o = _hgrn_mixer(proj, lb_logits, head_gain_a, layer, j, bsz, seq)
            y = _matmul(o, w_out_a, j, F32)
        else:
            proj = _matmul(hm, w_in_b, j, BF16)
            o = _attn_mixer(proj, attn_sinks, rel_bias, j, bsz, seq, d)
            y = _matmul(o, w_out_b, j, F32)

        n_pad = LANES - n_groups - n_experts
        w_router = jnp.concatenate(
            [w_router_group[layer].astype(F32), w_router_expert[layer].astype(F32), jnp.zeros((d, n_pad), F32)], axis=1)
        b_router = jnp.concatenate(
            [b_router_group[layer].astype(F32), b_router_expert[layer].astype(F32), jnp.zeros((n_pad,), F32)]
        ).reshape(1, LANES)

        x2, hf, meta, counts = _ln_router(
            x2, y, mod, row_of, ln_g[layer, 0:1].astype(F32), ln_b[layer, 0:1].astype(F32),
            w_router, b_router, alpha, n_groups, n_experts, seq)
        dest, tok_pad, blk_e, nxt_e, first, nused = _moe_plan(meta, counts, n_groups, n_experts, moe_bm)
        ysort = _moe_experts(hf, tok_pad, blk_e, nxt_e, first, nused, w_gate_up, w_down, layer, moe_bm)
        next_row_of = row_of_layer(layer + 1) if layer + 1 < depth else None
        outs = _ln_combine(x2, meta, dest, ysort, mod, row_of, next_row_of,
                           ln_g[layer, 1:2].astype(F32), ln_b[layer, 1:2].astype(F32), alpha, seq)
        x2 = outs[0]
        if next_row_of is not None:
            hm = outs[1]

    return x2.reshape(bsz, seq, d).astype(x.dtype)
```

```python
import functools
import math

import jax
import jax.numpy as jnp
from jax import lax
from jax.experimental import pallas as pl
from jax.experimental.pallas import tpu as pltpu

F32 = jnp.float32
BF16 = jnp.bfloat16
I32 = jnp.int32

LANES = 128
SUBLANES = 8
V7X_VMEM_LIMIT_BYTES = 56 * 1024 * 1024

HG_HEAD_DIM = 128
ATT_HEAD_DIM = 64
ATT_BLOCK = 128
WINDOW = 128
N_BUCKETS = 32
MAX_DISTANCE = 128
LN_EPS = 1e-5
RMS_EPS = 1e-6
NEG_INF = float("-inf")


def _cparams(n_axes):
    return pltpu.CompilerParams(
        dimension_semantics=("arbitrary",) * n_axes,
        vmem_limit_bytes=V7X_VMEM_LIMIT_BYTES,
    )


def _sigmoid(x):
    return 0.5 * jnp.tanh(0.5 * x) + 0.5


def _silu(x):
    return x * _sigmoid(x)


def _dot_nt(a, b):
    return lax.dot_general(a, b, (((1,), (1,)), ((), ())), preferred_element_type=F32)


def _dot_tn(a, b):
    return lax.dot_general(a, b, (((0,), (0,)), ((), ())), preferred_element_type=F32)


def _dot(a, b):
    return jnp.dot(a, b, preferred_element_type=F32)


def _pack_bf16_pairs(x, rounded=False):
    n = x.shape[1] // 2
    bits = pltpu.bitcast(x if rounded else x.astype(BF16).astype(F32), jnp.uint32)
    return (bits[:, :n] >> 16) | (bits[:, n:] & jnp.uint32(0xFFFF0000))


def _unpack_bf16_pairs(w):
    lo = pltpu.bitcast(w << 16, F32)
    hi = pltpu.bitcast(w & jnp.uint32(0xFFFF0000), F32)
    return jnp.concatenate([lo, hi], axis=1)


def _split_bf16(x, parts):
    out = []
    r = x
    for _ in range(parts):
        h = r.astype(BF16)
        out.append(h)
        r = r - h.astype(F32)
    return out


def _ada_kernel(c_ref, w_ref, b_ref, o_ref):
    ca = _silu(c_ref[...]).astype(BF16)
    o_ref[...] = _dot(ca, w_ref[...].astype(BF16)) + b_ref[...]


def _ada_modulation(c, w_ada, b_ada):
    nl, d, n6 = w_ada.shape
    bsz = c.shape[0]
    rows = -(-bsz // SUBLANES) * SUBLANES
    c8 = jnp.zeros((rows, d), F32).at[:bsz].set(c.astype(F32))
    tn = 512
    out = pl.pallas_call(
        _ada_kernel,
        grid=(nl, n6 // tn),
        in_specs=[
            pl.BlockSpec((rows, d), lambda l, j: (0, 0)),
            pl.BlockSpec((None, d, tn), lambda l, j: (l, 0, j)),
            pl.BlockSpec((None, 1, tn), lambda l, j: (l, 0, j)),
        ],
        out_specs=pl.BlockSpec((None, rows, tn), lambda l, j: (l, 0, j)),
        out_shape=jax.ShapeDtypeStruct((nl, rows, n6), F32),
        compiler_params=_cparams(2),
        name="ada_modulation",
    )(c8, w_ada, b_ada.reshape(nl, 1, n6))
    return out[:, :bsz].reshape(nl * bsz * 6, 1, d)


def _modulate_kernel(x_ref, sc_ref, sh_ref, o_ref):
    o_ref[...] = (x_ref[...] * (1.0 + sc_ref[...]) + sh_ref[...]).astype(o_ref.dtype)


def _modulate(x2, mod, sc_row, sh_row, bsz, seq):
    t, d = x2.shape
    bs = min(512, seq)
    nbs = seq // bs
    return pl.pallas_call(
        _modulate_kernel,
        grid=(t // bs,),
        in_specs=[
            pl.BlockSpec((bs, d), lambda i: (i, 0)),
            pl.BlockSpec((None, 1, d), lambda i: (sc_row(i // nbs), 0, 0)),
            pl.BlockSpec((None, 1, d), lambda i: (sh_row(i // nbs), 0, 0)),
        ],
        out_specs=pl.BlockSpec((bs, d), lambda i: (i, 0)),
        out_shape=jax.ShapeDtypeStruct((t, d), BF16),
        compiler_params=_cparams(1),
        name="modulate",
    )(x2, mod, mod)


def _matmul_kernel(x_ref, w_hbm, o_ref, wbf_ref, stage_ref, sem, *, layer, bn, kc):
    j = pl.program_id(0)
    i = pl.program_id(1)
    nj = pl.num_programs(0)
    cur = j % 2

    def slab_copy(jb, c):
        return pltpu.make_async_copy(
            w_hbm.at[layer, pl.ds(c * kc, kc), pl.ds(jb * bn, bn)], stage_ref, sem.at[0])

    def cast_slab(c, buf):
        wbf_ref[buf, pl.ds(c * kc, kc), :] = stage_ref[...].astype(BF16)

    @pl.when((j == 0) & (i == 0))
    def _():
        def first_block(c, carry):
            cp = slab_copy(0, c)
            cp.start()
            cp.wait()
            cast_slab(pl.multiple_of(c, 1), 0)
            return carry
        lax.fori_loop(0, pl.num_programs(1), first_block, 0)

    @pl.when(j + 1 < nj)
    def _():
        slab_copy(j + 1, i).start()

    o_ref[...] = _dot(x_ref[...], wbf_ref[cur]).astype(o_ref.dtype)

    @pl.when(j + 1 < nj)
    def _():
        slab_copy(j + 1, i).wait()
        cast_slab(i, 1 - cur)


def _matmul(x, w3, layer, out_dtype):
    m, k = x.shape
    n = w3.shape[2]
    bm = min(1024, m)
    bn = 1024 if n % 1024 == 0 else 512
    steps = m // bm
    kc = k // steps
    assert k % steps == 0 and kc % SUBLANES == 0
    return pl.pallas_call(
        functools.partial(_matmul_kernel, layer=layer, bn=bn, kc=kc),
        grid=(n // bn, steps),
        in_specs=[
            pl.BlockSpec((bm, k), lambda j, i: (i, 0)),
            pl.BlockSpec(memory_space=pl.ANY),
        ],
        out_specs=pl.BlockSpec((bm, bn), lambda j, i: (i, j)),
        out_shape=jax.ShapeDtypeStruct((m, n), out_dtype),
        scratch_shapes=[
            pltpu.VMEM((2, k, bn), BF16),
            pltpu.VMEM((kc, bn), F32),
            pltpu.SemaphoreType.DMA((1,)),
        ],
        compiler_params=_cparams(2),
        name="dense_projection",
    )(x, w3)


HG_BASE = 16
HG_BASE_MAX_DECAY = 86.0


def _hgrn_kernel(q_ref, f_ref, v_ref, g_ref, lbl_ref, gain_ref, o_ref,
                 st_ref, b_ref, oi_ref, rest_ref, tri_ref, mask_ref, bmask_ref, cmask_ref, *, layer, chunk, heads):
    c = chunk
    hc = c // 2
    dh = HG_HEAD_DIM
    nlev = int(math.log2(c))
    base_lv = int(math.log2(HG_BASE))
    nbig = nlev - base_lv
    hs = range(heads)
    first = (pl.program_id(0) == 0) & (pl.program_id(1) == 0) & (pl.program_id(2) == 0)

    @pl.when(first)
    def _():
        row = lax.broadcasted_iota(I32, (c, c), 0)
        col = lax.broadcasted_iota(I32, (c, c), 1)
        tri_ref[...] = jnp.where(row >= col, 1.0, 0.0).astype(BF16)
        x = row ^ col
        mask_ref[0] = jnp.where(x == 0, 1.0, 0.0).astype(F32)
        for lv in range(1, base_lv + 1):
            mask_ref[lv] = jnp.where(x < (1 << lv), 1.0, 0.0).astype(F32)
        bmask_ref[...] = jnp.where((x < HG_BASE) & (row >= col), 1.0, 0.0).astype(F32)
        xh = lax.broadcasted_iota(I32, (hc, hc), 0) ^ lax.broadcasted_iota(I32, (hc, hc), 1)
        for lv in range(base_lv, nlev - 1):
            cmask_ref[lv - base_lv] = jnp.where(xh < (1 << lv), 1.0, 0.0).astype(F32)

    @pl.when(pl.program_id(2) == 0)
    def _():
        st_ref[...] = jnp.zeros_like(st_ref)

    lbl = lbl_ref[...]
    rows = [lbl[i:i + 1, :] for i in range(lbl.shape[0])]
    mx = functools.reduce(jnp.maximum, rows)
    es = [jnp.exp(r - mx) for r in rows]
    lb = functools.reduce(lambda a, b: a + b, es[:layer + 1]) / functools.reduce(lambda a, b: a + b, es)

    q_all = _silu(q_ref[...].astype(F32))
    forget_all = lb + (1.0 - lb) * _sigmoid(f_ref[...].astype(F32))
    k_all = 1.0 - forget_all
    v_all = v_ref[...]

    w = heads * dh
    bb = _dot(tri_ref[...], jnp.concatenate(_split_bf16(jnp.log2(forget_all), 3), axis=1))
    b_all = bb[:, 0:w] + bb[:, w:2 * w] + bb[:, 2 * w:3 * w]

    def lanes(x, h):
        return x[:, h * dh:(h + 1) * dh]

    for h in hs:
        b_ref[h] = lanes(b_all, h)

    q = [lanes(q_all, h) for h in hs]
    k = [lanes(k_all, h) for h in hs]
    v = [lanes(v_all, h) for h in hs]
    b = [lanes(b_all, h) for h in hs]

    def b_row(h, r, n):
        return jnp.broadcast_to(b_ref[h, pl.ds(r, 1), :], (n, dh))

    blk_i = lax.broadcasted_iota(I32, (c // HG_BASE, dh), 0)
    base_ok = []
    for h in hs:
        ends = b_ref[h, pl.ds(HG_BASE - 1, c // HG_BASE, stride=HG_BASE), :]
        drop = jnp.where(blk_i == 0, 0.0, pltpu.roll(ends, 1, axis=0)) - ends
        base_ok.append(jnp.max(drop) <= HG_BASE_MAX_DECAY)

    qs_l, ks_l, vs_l = [], [], []
    for lv in range(base_lv, nlev):
        m = 1 << lv
        n = 2 * m
        for h in hs:
            qs, ks, vs = [], [], []
            for a in range(c // n):
                mid = b_row(h, a * n + m - 1, m)
                qs.append(q[h][a * n + m:(a + 1) * n] * jnp.exp2(b[h][a * n + m:(a + 1) * n] - mid))
                ks.append(k[h][a * n:a * n + m] * jnp.exp2(mid - b[h][a * n:a * n + m]))
                vs.append(v[h][a * n:a * n + m])
            qs_l.append(jnp.concatenate(qs, axis=0).astype(BF16))
            ks_l.append(jnp.concatenate(ks, axis=0).astype(BF16))
            vs_l.append(jnp.concatenate(vs, axis=0))
    qb16, kb16, qe, kd, b_last, st = [], [], [], [], [], []
    for h in hs:
        start = jnp.concatenate(
            [jnp.zeros((HG_BASE, dh), F32)] + [b_row(h, j * HG_BASE - 1, HG_BASE) for j in range(1, c // HG_BASE)],
            axis=0)
        dlt = jnp.maximum(b[h] - start, -HG_BASE_MAX_DECAY)
        qb16.append((q[h] * jnp.exp2(dlt)).astype(BF16))
        kb16.append((k[h] * jnp.exp2(-dlt)).astype(BF16))
        b_last.append(b_row(h, c - 1, c))
        qe.append((q[h] * jnp.exp2(b[h])).astype(BF16))
        kd.append((k[h] * jnp.exp2(b_last[h] - b[h])).astype(BF16))
        st.append(st_ref[h])

    nprod = nbig * heads
    a_l = [_dot_nt(qs_l[i], ks_l[i]) for i in range(nprod)]
    a16 = [_dot_nt(qb16[h], kb16[h]) for h in hs]
    o_inter = [_dot_nt(qe[h], st[h].astype(BF16)) for h in hs]
    for h in hs:
        st_ref[h] = st[h] * jnp.exp2(b_last[h][0:1, :]) + _dot_tn(v[h], kd[h])

    a_l = [(a_l[i] * cmask_ref[i // heads] if i // heads < nbig - 1 else a_l[i]).astype(BF16) for i in range(nprod)]
    a16 = [(a16[h] * bmask_ref[...]).astype(BF16) for h in hs]
    o_l = [_dot(a_l[i], vs_l[i]) for i in range(nprod)]
    oi_ref[...] = jnp.concatenate([_dot(a16[h], v[h]) for h in hs], axis=1)

    rest = []
    for h in hs:
        pieces = [None] * (c // HG_BASE)
        for lvi in range(nbig):
            m = HG_BASE << lvi
            per = m // HG_BASE
            o_c = o_l[lvi * heads + h]
            for a in range(c // (2 * m)):
                for u in range(per):
                    dst = (a * 2 * m + m) // HG_BASE + u
                    src = o_c[(a * per + u) * HG_BASE:(a * per + u + 1) * HG_BASE]
                    pieces[dst] = src if pieces[dst] is None else pieces[dst] + src
        zero_slab = jnp.zeros((HG_BASE, dh), F32)
        rest.append(o_inter[h] + jnp.concatenate([zero_slab if p is None else p for p in pieces], axis=0))
    rest_ref[...] = jnp.concatenate(rest, axis=1)

    for h in hs:
        @pl.when(jnp.logical_not(base_ok[h]))
        def _(h=h):
            rowi = lax.broadcasted_iota(I32, (c, dh), 0)
            sub = lax.broadcasted_iota(I32, (SUBLANES, dh), 0)
            ntile = c // SUBLANES
            forget = lanes(forget_all, h)
            attn = _dot_nt(q[h].astype(BF16), k[h].astype(BF16)) * mask_ref[0]
            for lv in range(base_lv):
                m = 1 << lv
                isq = (rowi & m) != 0
                if m == 1:
                    e = jnp.where(isq, forget, 1.0)
                else:
                    if m >= SUBLANES:
                        tiles = [b_row(h, (j * SUBLANES // (2 * m)) * 2 * m + m - 1, SUBLANES) for j in range(ntile)]
                    elif m == 4:
                        tiles = [b_row(h, j * SUBLANES + 3, SUBLANES) for j in range(ntile)]
                    else:
                        tiles = [jnp.where(sub < 4, b_row(h, j * SUBLANES + 1, SUBLANES),
                                           b_row(h, j * SUBLANES + 5, SUBLANES)) for j in range(ntile)]
                    mid = jnp.concatenate(tiles, axis=0)
                    e = jnp.exp2(jnp.where(isq, b[h] - mid, mid - b[h]))
                qt = jnp.where(isq, q[h] * e, 0.0).astype(BF16)
                kt = jnp.where(isq, 0.0, k[h] * e).astype(BF16)
                attn = attn + _dot_nt(qt, kt) * mask_ref[lv + 1]
            oi_ref[:, h * dh:(h + 1) * dh] = _dot(attn.astype(BF16), v[h])

    o = oi_ref[...] + rest_ref[...]
    o = jnp.concatenate(
        [lanes(o, h) * lax.rsqrt(jnp.mean(lanes(o, h) * lanes(o, h), axis=-1, keepdims=True) + RMS_EPS) for h in hs],
        axis=1)
    o = o * gain_ref[...] * _silu(g_ref[...].astype(F32))
    o_ref[...] = o.astype(o_ref.dtype)


def _hgrn_mixer(proj, lb_logits, head_gain, layer, j, bsz, seq):
    t, d4 = proj.shape
    d = d4 // 4
    nh = d // HG_HEAD_DIM
    heads = next(n for n in (8, 4, 2, 1) if nh % n == 0)
    w = heads * HG_HEAD_DIM
    nhp = nh // heads
    chunk = 256 if seq % 256 == 0 else 128
    nc = seq // chunk
    nlev = int(math.log2(chunk))
    base_lv = int(math.log2(HG_BASE))

    def col(part):
        return lambda b, h, c: (b * nc + c, part * nhp + h)

    return pl.pallas_call(
        functools.partial(_hgrn_kernel, layer=layer, chunk=chunk, heads=heads),
        grid=(bsz, nhp, nc),
        in_specs=[
            pl.BlockSpec((chunk, w), col(0)),
            pl.BlockSpec((chunk, w), col(1)),
            pl.BlockSpec((chunk, w), col(2)),
            pl.BlockSpec((chunk, w), col(3)),
            pl.BlockSpec((lb_logits.shape[0], w), lambda b, h, c: (0, h)),
            pl.BlockSpec((None, 1, w), lambda b, h, c: (j, 0, h)),
        ],
        out_specs=pl.BlockSpec((chunk, w), lambda b, h, c: (b * nc + c, h)),
        out_shape=jax.ShapeDtypeStruct((t, d), BF16),
        scratch_shapes=[
            pltpu.VMEM((heads, HG_HEAD_DIM, HG_HEAD_DIM), F32),
            pltpu.VMEM((heads, chunk, HG_HEAD_DIM), F32),
            pltpu.VMEM((chunk, w), F32),
            pltpu.VMEM((chunk, w), F32),
            pltpu.VMEM((chunk, chunk), BF16),
            pltpu.VMEM((base_lv + 1, chunk, chunk), F32),
            pltpu.VMEM((chunk, chunk), F32),
            pltpu.VMEM((nlev - 1 - base_lv, chunk // 2, chunk // 2), F32),
        ],
        compiler_params=_cparams(3),
        name="hgrn2_mixer",
    )(proj, proj, proj, proj, lb_logits.astype(F32), head_gain.astype(F32).reshape(head_gain.shape[0], 1, d))


def _t5_bucket(dist):
    max_exact = N_BUCKETS // 2
    n = jnp.maximum(dist, 0)
    large = max_exact + (jnp.log(jnp.maximum(n, 1).astype(F32) / max_exact)
                         / math.log(MAX_DISTANCE / max_exact)
                         * (N_BUCKETS - max_exact)).astype(I32)
    large = jnp.minimum(large, N_BUCKETS - 1)
    return jnp.where(n < max_exact, n, large)


def _attn_kernel(rb_ref, sink_ref, q_ref, kp_ref, kc_ref, vp_ref, vc_ref, bucket_ref, o_ref,
                 bias_ref, *, group, pairs, layer_j):
    blk = ATT_BLOCK
    hd = ATT_HEAD_DIM
    pr = pl.program_id(0)
    i = pl.program_id(2)
    heads_per_step = 2 * group * pairs
    upper = lax.broadcasted_iota(I32, (blk, blk), 1) > lax.broadcasted_iota(I32, (blk, blk), 0)

    @pl.when((pl.program_id(1) == 0) & (i == 0))
    def _():
        bucket = bucket_ref[...]
        bucket_m = jnp.where(upper, bucket[:, 0:blk], bucket[:, blk:2 * blk])

        def per_head(hh, carry):
            h = pr * heads_per_step + hh
            tbl = jnp.zeros((blk, blk), F32)
            for bk in range(N_BUCKETS):
                tbl = jnp.where(bucket_m == bk, rb_ref[bk, h], tbl)
            bias_ref[hh] = tbl
            bias_ref[heads_per_step + hh] = jnp.where(upper, NEG_INF, tbl)
            return carry

        lax.fori_loop(0, heads_per_step, per_head, 0)

    lane = lax.broadcasted_iota(I32, (2 * blk, LANES), 1)
    table0 = jnp.where(i == 0, heads_per_step, 0)

    q_all = q_ref[...] * (hd ** -0.5)

    kbds, vbds = [], []
    for pp in range(pairs):
        kk = jnp.concatenate([kp_ref[:, pp * LANES:(pp + 1) * LANES], kc_ref[:, pp * LANES:(pp + 1) * LANES]],
                             axis=0).astype(F32)
        vv = jnp.concatenate([vp_ref[:, pp * LANES:(pp + 1) * LANES], vc_ref[:, pp * LANES:(pp + 1) * LANES]],
                             axis=0).astype(F32)
        for c in range(2):
            if c == 0:
                klo = jnp.where(lane < hd, kk, 0.0)
                khi = pltpu.roll(klo, hd, axis=1)
                vlo = jnp.where(lane < hd, vv, 0.0)
                vhi = pltpu.roll(vlo, hd, axis=1)
            else:
                khi = jnp.where(lane >= hd, kk, 0.0)
                klo = pltpu.roll(khi, hd, axis=1)
                vhi = jnp.where(lane >= hd, vv, 0.0)
                vlo = pltpu.roll(vhi, hd, axis=1)
            kbds.append(jnp.concatenate([klo, khi], axis=0).astype(BF16))
            vbds.append(jnp.concatenate([vlo, vhi], axis=0).astype(BF16))

    half = group // 2
    tiles = [(c, p) for c in range(2 * pairs) for p in range(half)]
    lgs = [_dot_nt(q_all[:, (c * half + p) * LANES:(c * half + p + 1) * LANES], kbds[c]) for c, p in tiles]
    p2s, rinvs = [], []
    for (c, p), lg in zip(tiles, lgs):
        probs, rinv = [], []
        for hh in range(2):
            hl = c * group + 2 * p + hh
            sk = sink_ref[layer_j, pr * heads_per_step + hl]
            c0 = hh * 2 * blk
            l = jnp.where(upper, lg[:, c0:c0 + blk], lg[:, c0 + blk:c0 + 2 * blk]) + bias_ref[table0 + hl]
            mx = jnp.maximum(jnp.max(l, axis=-1, keepdims=True), sk)
            pe = jnp.exp(l - mx)
            rinv.append(1.0 / (jnp.sum(pe, axis=-1, keepdims=True) + jnp.exp(sk - mx)))
            pe = pe.astype(BF16)
            zero = jnp.zeros_like(pe)
            probs += [jnp.where(upper, pe, zero), jnp.where(upper, zero, pe)]
        p2s.append(jnp.concatenate(probs, axis=1))
        rinvs.append(rinv)
    lane_o = lax.broadcasted_iota(I32, (blk, LANES), 1)
    for (c, p), p2, rinv in zip(tiles, p2s, rinvs):
        tile = c * half + p
        o = _dot(p2, vbds[c]) * jnp.where(lane_o < hd, rinv[0], rinv[1])
        o_ref[:, tile * LANES:(tile + 1) * LANES] = o.astype(o_ref.dtype)


def _attn_mixer(proj, sinks, rel_bias, layer_j, bsz, seq, d):
    t, att_in = proj.shape
    kvw = (att_in - d) // 2
    n_heads = d // ATT_HEAD_DIM
    kvh = kvw // ATT_HEAD_DIM
    group = n_heads // kvh
    assert kvh % 2 == 0 and group % 2 == 0
    assert WINDOW == ATT_BLOCK
    blk = ATT_BLOCK
    nb = seq // blk
    npair = kvh // 2
    pairs = 1
    ngrp = npair // pairs
    qw = 2 * group * ATT_HEAD_DIM * pairs
    kw = LANES * pairs
    k0 = d // kw
    v0 = (d + kvw) // kw
    assert d % kw == 0 and (d + kvw) % kw == 0

    qi = jnp.arange(blk)[:, None]
    sj = jnp.arange(2 * blk)[None, :]
    bucket = _t5_bucket(qi + blk - sj).astype(I32)

    def prev(i):
        return jnp.maximum(i - 1, 0)

    grid_spec = pltpu.PrefetchScalarGridSpec(
        num_scalar_prefetch=2,
        grid=(ngrp, bsz, nb),
        in_specs=[
            pl.BlockSpec((blk, qw), lambda p, b, i, *_: (b * nb + i, p)),
            pl.BlockSpec((blk, kw), lambda p, b, i, *_: (b * nb + prev(i), k0 + p)),
            pl.BlockSpec((blk, kw), lambda p, b, i, *_: (b * nb + i, k0 + p)),
            pl.BlockSpec((blk, kw), lambda p, b, i, *_: (b * nb + prev(i), v0 + p)),
            pl.BlockSpec((blk, kw), lambda p, b, i, *_: (b * nb + i, v0 + p)),
            pl.BlockSpec((blk, 2 * blk), lambda p, b, i, *_: (0, 0)),
        ],
        out_specs=pl.BlockSpec((blk, qw), lambda p, b, i, *_: (b * nb + i, p)),
        scratch_shapes=[pltpu.VMEM((4 * group * pairs, blk, blk), F32)],
    )
    return pl.pallas_call(
        functools.partial(_attn_kernel, group=group, pairs=pairs, layer_j=layer_j),
        grid_spec=grid_spec,
        out_shape=jax.ShapeDtypeStruct((t, d), BF16),
        compiler_params=_cparams(3),
        name="swa_sink_mixer",
    )(rel_bias.astype(F32), sinks.astype(F32), proj, proj, proj, proj, proj, bucket)


def _layer_norm_rows(z, g, b):
    mu = jnp.mean(z, axis=-1, keepdims=True)
    zc = z - mu
    var = jnp.mean(zc * zc, axis=-1, keepdims=True)
    return zc * lax.rsqrt(var + LN_EPS) * g + b


def _ln_router_kernel(x_ref, y_ref, gate_ref, sc_ref, sh_ref, lng_ref, lnb_ref, wr_ref, br_ref,
                      xo_ref, hf_ref, meta_ref, cnt_ref, carry_ref, wsplit_ref, *, alpha, n_groups, n_experts):
    i = pl.program_id(0)
    bm = x_ref.shape[0]
    epg = n_experts // n_groups

    @pl.when(i == 0)
    def _():
        carry_ref[...] = jnp.zeros_like(carry_ref)
        wh, wl = _split_bf16(wr_ref[...], 2)
        wsplit_ref[:, 0:LANES] = wh
        wsplit_ref[:, LANES:2 * LANES] = wl

    z = alpha * x_ref[...] + (1.0 + gate_ref[...]) * y_ref[...]
    xn = _layer_norm_rows(z, lng_ref[...], lnb_ref[...])
    xo_ref[...] = xn
    hf = xn * (1.0 + sc_ref[...]) + sh_ref[...]
    xh = hf.astype(BF16)
    xh32 = xh.astype(F32)
    hf_ref[...] = _pack_bf16_pairs(xh32, rounded=True)

    xl = (hf - xh32).astype(BF16)
    hh = _dot(xh, wsplit_ref[...])
    lg = hh[:, 0:LANES] + hh[:, LANES:2 * LANES] + _dot(xl, wsplit_ref[:, 0:LANES]) + br_ref[...]

    lane = lax.broadcasted_iota(I32, (bm, LANES), 1)
    lanef = lane.astype(F32)
    big = float(LANES)

    gl = jnp.where(lane < n_groups, lg, NEG_INF)
    gmax = jnp.max(gl, axis=-1, keepdims=True)
    gsel = jnp.min(jnp.where(gl == gmax, lanef, big), axis=-1, keepdims=True)
    p_group = 1.0 / jnp.sum(jnp.exp(gl - gmax), axis=-1, keepdims=True)

    lo = n_groups + gsel * epg
    el = jnp.where((lanef >= lo) & (lanef < lo + epg), lg, NEG_INF)
    m1 = jnp.max(el, axis=-1, keepdims=True)
    i1 = jnp.min(jnp.where(el == m1, lanef, big), axis=-1, keepdims=True)
    el2 = jnp.where(lanef == i1, NEG_INF, el)
    m2 = jnp.max(el2, axis=-1, keepdims=True)
    i2 = jnp.min(jnp.where(el2 == m2, lanef, big), axis=-1, keepdims=True)
    e21 = jnp.exp(m2 - m1)
    g0 = p_group / (1.0 + e21)
    g1 = g0 * e21

    oh0 = lanef == i1
    oh1 = lanef == i2
    cnt = jnp.where(oh0 | oh1, 1.0, 0.0)
    row = lax.broadcasted_iota(I32, (bm, bm), 0)
    col = lax.broadcasted_iota(I32, (bm, bm), 1)
    stril = jnp.where(row > col, 1.0, 0.0).astype(BF16)
    before = _dot(stril, cnt.astype(BF16)) + carry_ref[...]
    r0 = jnp.sum(jnp.where(oh0, before, 0.0), axis=-1, keepdims=True)
    r1 = jnp.sum(jnp.where(oh1, before, 0.0), axis=-1, keepdims=True)
    carry_ref[...] = carry_ref[...] + jnp.sum(cnt, axis=0, keepdims=True)
    cnt_ref[...] = carry_ref[...]

    meta = jnp.where(lane == 0, i1 - n_groups, 0.0)
    meta = jnp.where(lane == 1, i2 - n_groups, meta)
    meta = jnp.where(lane == 2, g0, meta)
    meta = jnp.where(lane == 3, g1, meta)
    meta = jnp.where(lane == 4, r0, meta)
    meta = jnp.where(lane == 5, r1, meta)
    meta_ref[...] = meta


def _ln_router(x2, y2, mod, row_of, ln_g, ln_b, w_router, b_router, alpha, n_groups, n_experts, seq):
    t, d = x2.shape
    bm = 128
    nbs = seq // bm

    def mrow(which):
        return lambda i: (row_of(i // nbs, which), 0, 0)

    return pl.pallas_call(
        functools.partial(_ln_router_kernel, alpha=alpha, n_groups=n_groups, n_experts=n_experts),
        grid=(t // bm,),
        in_specs=[
            pl.BlockSpec((bm, d), lambda i: (i, 0)),
            pl.BlockSpec((bm, d), lambda i: (i, 0)),
            pl.BlockSpec((None, 1, d), mrow(2)),
            pl.BlockSpec((None, 1, d), mrow(4)),
            pl.BlockSpec((None, 1, d), mrow(3)),
            pl.BlockSpec((1, d), lambda i: (0, 0)),
            pl.BlockSpec((1, d), lambda i: (0, 0)),
            pl.BlockSpec((d, LANES), lambda i: (0, 0)),
            pl.BlockSpec((1, LANES), lambda i: (0, 0)),
        ],
        out_specs=[
            pl.BlockSpec((bm, d), lambda i: (i, 0)),
            pl.BlockSpec((bm, d // 2), lambda i: (i, 0)),
            pl.BlockSpec((bm, LANES), lambda i: (i, 0)),
            pl.BlockSpec((1, LANES), lambda i: (0, 0)),
        ],
        out_shape=[
            jax.ShapeDtypeStruct((t, d), F32),
            jax.ShapeDtypeStruct((t, d // 2), jnp.uint32),
            jax.ShapeDtypeStruct((t, LANES), F32),
            jax.ShapeDtypeStruct((1, LANES), F32),
        ],
        scratch_shapes=[pltpu.VMEM((1, LANES), F32), pltpu.VMEM((d, 2 * LANES), BF16)],
        compiler_params=_cparams(1),
        name="ln_router",
    )(x2, y2, mod, mod, mod, ln_g, ln_b, w_router, b_router)


MOE_GATHER_SLOTS = 3


def _cast_rows(src_ref, dst_ref, rows=128):
    def body(r, carry):
        sl = pl.ds(pl.multiple_of(r * rows, rows), rows)
        dst_ref[sl, :] = src_ref[sl, :].astype(dst_ref.dtype)
        return carry
    lax.fori_loop(0, src_ref.shape[0] // rows, body, 0)


def _moe_kernel(blk_e_ref, nxt_e_ref, first_ref, grp_ref, nused_ref,
                tok0_ref, tok1_ref, tok2_ref, hf_hbm, wgu_hbm, wdn_hbm, y_ref,
                xbuf, wgu_st, wdn_st, wgu_bf, wdn_bf, gsem, wsem, *, layer, d_expert):
    i = pl.program_id(0)
    bm = xbuf.shape[1]
    nused = nused_ref[0]
    slot = i % MOE_GATHER_SLOTS

    def row_copy(tok, r, s):
        return pltpu.make_async_copy(hf_hbm.at[pl.ds(tok, 1), :], xbuf.at[s, pl.ds(r, 1), :], gsem.at[s])

    def issue_rows(tok_ref, s):
        for r in range(bm):
            row_copy(tok_ref[0, r], r, s).start()

    def wait_rows(s):
        pltpu.make_async_copy(hf_hbm.at[pl.ds(0, bm), :], xbuf.at[s], gsem.at[s]).wait()

    def wgu_copy(e):
        return pltpu.make_async_copy(wgu_hbm.at[layer, e], wgu_st, wsem.at[0])

    def wdn_copy(e, s):
        return pltpu.make_async_copy(wdn_hbm.at[layer, e], wdn_st.at[s], wsem.at[1 + s])

    @pl.when(i == 0)
    def _():
        issue_rows(tok0_ref, 0)
        wgu_copy(blk_e_ref[0]).start(priority=1)
        wdn_copy(blk_e_ref[0], 0).start(priority=1)

    @pl.when((i == 0) & (nused > 1))
    def _():
        issue_rows(tok1_ref, 1)

    @pl.when((i < nused) & (first_ref[i] == 1))
    def _():
        par = grp_ref[i] % 2
        has_next = nxt_e_ref[i] >= 0

        @pl.when(has_next)
        def _():
            wdn_copy(nxt_e_ref[i], 1 - par).start(priority=1)

        wgu_copy(blk_e_ref[i]).wait()
        _cast_rows(wgu_st, wgu_bf)

        @pl.when(has_next)
        def _():
            wgu_copy(nxt_e_ref[i]).start(priority=1)

        wdn_copy(blk_e_ref[i], par).wait()
        _cast_rows(wdn_st.at[par], wdn_bf)

    @pl.when(i + 2 < nused)
    def _():
        issue_rows(tok2_ref, (i + 2) % MOE_GATHER_SLOTS)

    @pl.when(i < nused)
    def _():
        wait_rows(slot)
        xb = _unpack_bf16_pairs(xbuf[slot]).astype(BF16)
        a = _dot(xb, wgu_bf[...])
        h = _silu(a[:, :d_expert]) * a[:, d_expert:]
        y_ref[...] = _pack_bf16_pairs(_dot(h.astype(BF16), wdn_bf[...]))

    @pl.when(i >= nused)
    def _():
        y_ref[...] = jnp.zeros_like(y_ref)


def _moe_experts(hf, tok_pad, blk_e, nxt_e, first, grp, nused, w_gate_up, w_down, layer, bm):
    d = w_down.shape[3]
    nb = blk_e.shape[0]
    d_expert = w_down.shape[2]
    tok3 = tok_pad.reshape(nb, 1, bm)
    grid_spec = pltpu.PrefetchScalarGridSpec(
        num_scalar_prefetch=5,
        grid=(nb,),
        in_specs=[
            pl.BlockSpec((None, 1, bm), lambda i, *_: (i, 0, 0), memory_space=pltpu.SMEM),
            pl.BlockSpec((None, 1, bm), lambda i, *_: (jnp.minimum(i + 1, nb - 1), 0, 0), memory_space=pltpu.SMEM),
            pl.BlockSpec((None, 1, bm), lambda i, *_: (jnp.minimum(i + 2, nb - 1), 0, 0), memory_space=pltpu.SMEM),
            pl.BlockSpec(memory_space=pl.ANY),
            pl.BlockSpec(memory_space=pl.ANY),
            pl.BlockSpec(memory_space=pl.ANY),
        ],
        out_specs=pl.BlockSpec((bm, d // 2), lambda i, *_: (i, 0)),
        scratch_shapes=[
            pltpu.VMEM((MOE_GATHER_SLOTS, bm, d // 2), jnp.uint32),
            pltpu.VMEM((d, 2 * d_expert), F32),
            pltpu.VMEM((2, d_expert, d), F32),
            pltpu.VMEM((d, 2 * d_expert), BF16),
            pltpu.VMEM((d_expert, d), BF16),
            pltpu.SemaphoreType.DMA((MOE_GATHER_SLOTS,)),
            pltpu.SemaphoreType.DMA((3,)),
        ],
    )
    return pl.pallas_call(
        functools.partial(_moe_kernel, layer=layer, d_expert=d_expert),
        grid_spec=grid_spec,
        out_shape=jax.ShapeDtypeStruct((nb * bm, d // 2), jnp.uint32),
        compiler_params=_cparams(1),
        name="moe_experts",
    )(blk_e, nxt_e, first, grp, nused, tok3, tok3, tok3, hf, w_gate_up, w_down)


def _moe_plan(meta, counts_row, n_groups, n_experts, bm):
    t = meta.shape[0]
    eid = meta[:, 0:2].astype(I32)
    rank = meta[:, 4:6].astype(I32)
    counts = counts_row[0, n_groups:n_groups + n_experts].astype(I32)
    padded = (counts + bm - 1) // bm * bm
    pad_end = jnp.cumsum(padded)
    pad_start = pad_end - padded
    dest = pad_start[eid] + rank
    nb = (2 * t) // bm + n_experts
    nused = pad_end[-1] // bm
    ids = jnp.arange(nb, dtype=I32)
    raw_e = jnp.minimum(jnp.sum((pad_end[None, :] <= (ids * bm)[:, None]).astype(I32), axis=1), n_experts - 1)
    used = ids < nused
    blk_e = jnp.where(used, raw_e, raw_e[nused - 1])
    prev_e = jnp.concatenate([jnp.full((1,), -1, I32), blk_e[:-1]])
    first = (used & (blk_e != prev_e)).astype(I32)
    grp = jnp.cumsum(first) - 1
    key = jnp.where(used, blk_e, n_experts)
    nxt_idx = jnp.sum((key[None, :] <= blk_e[:, None]).astype(I32), axis=1)
    nxt_e = jnp.where(nxt_idx < nused, key[jnp.minimum(nxt_idx, nb - 1)], -1).astype(I32)
    tok = jnp.repeat(jnp.arange(t, dtype=I32), 2)
    tok_pad = jnp.zeros((nb * bm,), I32).at[dest.reshape(-1)].set(tok)
    return dest, tok_pad, blk_e, nxt_e, first, grp.astype(I32), nused.reshape(1).astype(I32)


def _ln_combine_kernel(*refs, alpha, with_next):
    if with_next:
        (dc_ref, dn_ref, x_ref, meta_ref, gate_ref, lng_ref, lnb_ref, sc_ref, sh_ref, y_hbm,
         xo_ref, hm_ref, ybuf, sem) = refs
    else:
        (dc_ref, dn_ref, x_ref, meta_ref, gate_ref, lng_ref, lnb_ref, y_hbm,
         xo_ref, ybuf, sem) = refs
    i = pl.program_id(0)
    n = pl.num_programs(0)
    bm = x_ref.shape[0]
    slot = i % 2

    def row_copy(src, r, s):
        return pltpu.make_async_copy(y_hbm.at[pl.ds(src, 1), :], ybuf.at[s, pl.ds(r, 1), :], sem.at[s])

    def issue_rows(d_ref, s):
        for r in range(2 * bm):
            row_copy(d_ref[0, r], r, s).start()

    def wait_rows(s):
        pltpu.make_async_copy(y_hbm.at[pl.ds(0, 2 * bm), :], ybuf.at[s], sem.at[s]).wait()

    @pl.when(i == 0)
    def _():
        issue_rows(dc_ref, 0)

    @pl.when(i + 1 < n)
    def _():
        issue_rows(dn_ref, 1 - slot)

    wait_rows(slot)
    meta = meta_ref[...]
    y = (_unpack_bf16_pairs(ybuf[slot, 0:bm, :]) * meta[:, 2:3]
         + _unpack_bf16_pairs(ybuf[slot, bm:2 * bm, :]) * meta[:, 3:4])
    z = alpha * x_ref[...] + (1.0 + gate_ref[...]) * y
    xn = _layer_norm_rows(z, lng_ref[...], lnb_ref[...])
    xo_ref[...] = xn
    if with_next:
        hm_ref[...] = (xn * (1.0 + sc_ref[...]) + sh_ref[...]).astype(hm_ref.dtype)


def _ln_combine(x2, meta, dest, ysort, mod, row_of, next_row_of, ln_g, ln_b, alpha, seq):
    t, d = x2.shape
    bm = 128
    nbs = seq // bm
    nblk = t // bm
    with_next = next_row_of is not None
    dest3 = dest.reshape(nblk, bm, 2).transpose(0, 2, 1).reshape(nblk, 1, 2 * bm)

    def mrow(fn, which):
        return lambda i: (fn(i // nbs, which), 0, 0)

    in_specs = [
        pl.BlockSpec((None, 1, 2 * bm), lambda i: (i, 0, 0), memory_space=pltpu.SMEM),
        pl.BlockSpec((None, 1, 2 * bm), lambda i: (jnp.minimum(i + 1, nblk - 1), 0, 0), memory_space=pltpu.SMEM),
        pl.BlockSpec((bm, d), lambda i: (i, 0)),
        pl.BlockSpec((bm, LANES), lambda i: (i, 0)),
        pl.BlockSpec((None, 1, d), mrow(row_of, 5)),
        pl.BlockSpec((1, d), lambda i: (0, 0)),
        pl.BlockSpec((1, d), lambda i: (0, 0)),
    ]
    args = [dest3, dest3, x2, meta, mod, ln_g, ln_b]
    out_specs = [pl.BlockSpec((bm, d), lambda i: (i, 0))]
    out_shape = [jax.ShapeDtypeStruct((t, d), F32)]
    if with_next:
        in_specs += [pl.BlockSpec((None, 1, d), mrow(next_row_of, 1)),
                     pl.BlockSpec((None, 1, d), mrow(next_row_of, 0))]
        args += [mod, mod]
        out_specs.append(pl.BlockSpec((bm, d), lambda i: (i, 0)))
        out_shape.append(jax.ShapeDtypeStruct((t, d), BF16))
    in_specs.append(pl.BlockSpec(memory_space=pl.ANY))
    args.append(ysort)
    return pl.pallas_call(
        functools.partial(_ln_combine_kernel, alpha=alpha, with_next=with_next),
        grid=(nblk,),
        in_specs=in_specs,
        out_specs=out_specs,
        out_shape=out_shape,
        scratch_shapes=[pltpu.VMEM((2, 2 * bm, d // 2), jnp.uint32), pltpu.SemaphoreType.DMA((2,))],
        compiler_params=_cparams(1),
        name="ln_moe_combine",
    )(*args)


def kernel(x, c, w_ada, b_ada, ln_g, ln_b, w_in_a, lb_logits, head_gain_a, w_out_a, w_in_b, attn_sinks, w_out_b, rel_bias, w_router_group, b_router_group, w_router_expert, b_router_expert, w_gate_up, w_down):
    bsz, seq, d = x.shape
    depth = w_ada.shape[0]
    n_groups = w_router_group.shape[2]
    n_experts = w_router_expert.shape[2]
    alpha = (2 * depth) ** 0.25
    t = bsz * seq
    moe_bm = 128

    mod = _ada_modulation(c, w_ada, b_ada)

    def row_of_layer(layer):
        return lambda b, which: (layer * bsz + b) * 6 + which

    x2 = x.reshape(t, d).astype(F32)
    row0 = row_of_layer(0)
    hm = _modulate(x2, mod, lambda b: row0(b, 1), lambda b: row0(b, 0), bsz, seq)

    for layer in range(depth):
        row_of = row_of_layer(layer)
        j = layer // 2
        if layer % 2 == 0:
            proj = _matmul(hm, w_in_a, j, BF16)
            o = _hgrn_mixer(proj, lb_logits, head_gain_a, layer, j, bsz, seq)
            y = _matmul(o, w_out_a, j, F32)
        else:
            proj = _matmul(hm, w_in_b, j, BF16)
            o = _attn_mixer(proj, attn_sinks, rel_bias, j, bsz, seq, d)
            y = _matmul(o, w_out_b, j, F32)

        n_pad = LANES - n_groups - n_experts
        w_router = jnp.concatenate(
            [w_router_group[layer].astype(F32), w_router_expert[layer].astype(F32), jnp.zeros((d, n_pad), F32)], axis=1)
        b_router = jnp.concatenate(
            [b_router_group[layer].astype(F32), b_router_expert[layer].astype(F32), jnp.zeros((n_pad,), F32)]
        ).reshape(1, LANES)

        x2, hf, meta, counts = _ln_router(
            x2, y, mod, row_of, ln_g[layer, 0:1].astype(F32), ln_b[layer, 0:1].astype(F32),
            w_router, b_router, alpha, n_groups, n_experts, seq)
        dest, tok_pad, blk_e, nxt_e, first, grp, nused = _moe_plan(meta, counts, n_groups, n_experts, moe_bm)
        ysort = _moe_experts(hf, tok_pad, blk_e, nxt_e, first, grp, nused, w_gate_up, w_down, layer, moe_bm)
        next_row_of = row_of_layer(layer + 1) if layer + 1 < depth else None
        outs = _ln_combine(x2, meta, dest, ysort, mod, row_of, next_row_of,
                           ln_g[layer, 1:2].astype(F32), ln_b[layer, 1:2].astype(F32), alpha, seq)
        x2 = outs[0]
        if next_row_of is not None:
            hm = outs[1]

    return x2.reshape(bsz, seq, d).astype(x.dtype)
```

```python
import functools
import math

import jax
import jax.numpy as jnp
from jax import lax
from jax.experimental import pallas as pl
from jax.experimental.pallas import tpu as pltpu

F32 = jnp.float32
BF16 = jnp.bfloat16
I32 = jnp.int32

LANES = 128
SUBLANES = 8
V7X_VMEM_LIMIT_BYTES = 56 * 1024 * 1024

HG_HEAD_DIM = 128
ATT_HEAD_DIM = 64
ATT_BLOCK = 128
WINDOW = 128
N_BUCKETS = 32
MAX_DISTANCE = 128
LN_EPS = 1e-5
RMS_EPS = 1e-6
NEG_INF = float("-inf")


def _cparams(n_axes):
    return pltpu.CompilerParams(
        dimension_semantics=("arbitrary",) * n_axes,
        vmem_limit_bytes=V7X_VMEM_LIMIT_BYTES,
    )


def _sigmoid(x):
    return 0.5 * jnp.tanh(0.5 * x) + 0.5


def _silu(x):
    return x * _sigmoid(x)


def _dot_nt(a, b):
    return lax.dot_general(a, b, (((1,), (1,)), ((), ())), preferred_element_type=F32)


def _dot_tn(a, b):
    return lax.dot_general(a, b, (((0,), (0,)), ((), ())), preferred_element_type=F32)


def _dot(a, b):
    return jnp.dot(a, b, preferred_element_type=F32)


def _pack_bf16_pairs(x, rounded=False):
    n = x.shape[1] // 2
    bits = pltpu.bitcast(x if rounded else x.astype(BF16).astype(F32), jnp.uint32)
    return (bits[:, :n] >> 16) | (bits[:, n:] & jnp.uint32(0xFFFF0000))


def _unpack_bf16_pairs(w):
    lo = pltpu.bitcast(w << 16, F32)
    hi = pltpu.bitcast(w & jnp.uint32(0xFFFF0000), F32)
    return jnp.concatenate([lo, hi], axis=1)


def _split_bf16(x, parts):
    out = []
    r = x
    for _ in range(parts):
        h = r.astype(BF16)
        out.append(h)
        r = r - h.astype(F32)
    return out


def _ada_kernel(c_ref, w_ref, b_ref, o_ref):
    ca = _silu(c_ref[...]).astype(BF16)
    o_ref[...] = _dot(ca, w_ref[...].astype(BF16)) + b_ref[...]


def _ada_modulation(c, w_ada, b_ada):
    nl, d, n6 = w_ada.shape
    bsz = c.shape[0]
    rows = -(-bsz // SUBLANES) * SUBLANES
    c8 = jnp.zeros((rows, d), F32).at[:bsz].set(c.astype(F32))
    tn = 512
    out = pl.pallas_call(
        _ada_kernel,
        grid=(nl, n6 // tn),
        in_specs=[
            pl.BlockSpec((rows, d), lambda l, j: (0, 0)),
            pl.BlockSpec((None, d, tn), lambda l, j: (l, 0, j)),
            pl.BlockSpec((None, 1, tn), lambda l, j: (l, 0, j)),
        ],
        out_specs=pl.BlockSpec((None, rows, tn), lambda l, j: (l, 0, j)),
        out_shape=jax.ShapeDtypeStruct((nl, rows, n6), F32),
        compiler_params=_cparams(2),
        name="ada_modulation",
    )(c8, w_ada, b_ada.reshape(nl, 1, n6))
    return out[:, :bsz].reshape(nl * bsz * 6, 1, d)


def _modulate_kernel(x_ref, sc_ref, sh_ref, o_ref):
    o_ref[...] = (x_ref[...] * (1.0 + sc_ref[...]) + sh_ref[...]).astype(o_ref.dtype)


def _modulate(x2, mod, sc_row, sh_row, bsz, seq):
    t, d = x2.shape
    bs = min(512, seq)
    nbs = seq // bs
    return pl.pallas_call(
        _modulate_kernel,
        grid=(t // bs,),
        in_specs=[
            pl.BlockSpec((bs, d), lambda i: (i, 0)),
            pl.BlockSpec((None, 1, d), lambda i: (sc_row(i // nbs), 0, 0)),
            pl.BlockSpec((None, 1, d), lambda i: (sh_row(i // nbs), 0, 0)),
        ],
        out_specs=pl.BlockSpec((bs, d), lambda i: (i, 0)),
        out_shape=jax.ShapeDtypeStruct((t, d), BF16),
        compiler_params=_cparams(1),
        name="modulate",
    )(x2, mod, mod)


def _matmul_kernel(x_ref, w_hbm, o_ref, wbf_ref, stage_ref, sem, *, layer, bn, kc):
    j = pl.program_id(0)
    i = pl.program_id(1)
    nj = pl.num_programs(0)
    cur = j % 2

    def slab_copy(jb, c):
        return pltpu.make_async_copy(
            w_hbm.at[layer, pl.ds(c * kc, kc), pl.ds(jb * bn, bn)], stage_ref, sem.at[0])

    def cast_slab(c, buf):
        wbf_ref[buf, pl.ds(c * kc, kc), :] = stage_ref[...].astype(BF16)

    @pl.when((j == 0) & (i == 0))
    def _():
        def first_block(c, carry):
            cp = slab_copy(0, c)
            cp.start()
            cp.wait()
            cast_slab(pl.multiple_of(c, 1), 0)
            return carry
        lax.fori_loop(0, pl.num_programs(1), first_block, 0)

    @pl.when(j + 1 < nj)
    def _():
        slab_copy(j + 1, i).start()

    o_ref[...] = _dot(x_ref[...], wbf_ref[cur]).astype(o_ref.dtype)

    @pl.when(j + 1 < nj)
    def _():
        slab_copy(j + 1, i).wait()
        cast_slab(i, 1 - cur)


def _matmul(x, w3, layer, out_dtype):
    m, k = x.shape
    n = w3.shape[2]
    bm = min(1024, m)
    bn = 1024 if n % 1024 == 0 else 512
    steps = m // bm
    kc = k // steps
    assert k % steps == 0 and kc % SUBLANES == 0
    return pl.pallas_call(
        functools.partial(_matmul_kernel, layer=layer, bn=bn, kc=kc),
        grid=(n // bn, steps),
        in_specs=[
            pl.BlockSpec((bm, k), lambda j, i: (i, 0)),
            pl.BlockSpec(memory_space=pl.ANY),
        ],
        out_specs=pl.BlockSpec((bm, bn), lambda j, i: (i, j)),
        out_shape=jax.ShapeDtypeStruct((m, n), out_dtype),
        scratch_shapes=[
            pltpu.VMEM((2, k, bn), BF16),
            pltpu.VMEM((kc, bn), F32),
            pltpu.SemaphoreType.DMA((1,)),
        ],
        compiler_params=_cparams(2),
        name="dense_projection",
    )(x, w3)


HG_BASE = 16
HG_BASE_MAX_DECAY = 86.0


def _hgrn_kernel(q_ref, f_ref, v_ref, g_ref, lbl_ref, gain_ref, o_ref,
                 st_ref, b_ref, oi_ref, rest_ref, tri_ref, mask_ref, bmask_ref, cmask_ref, *, layer, chunk, heads):
    c = chunk
    hc = c // 2
    dh = HG_HEAD_DIM
    nlev = int(math.log2(c))
    base_lv = int(math.log2(HG_BASE))
    nbig = nlev - base_lv
    hs = range(heads)
    first = (pl.program_id(0) == 0) & (pl.program_id(1) == 0) & (pl.program_id(2) == 0)

    @pl.when(first)
    def _():
        row = lax.broadcasted_iota(I32, (c, c), 0)
        col = lax.broadcasted_iota(I32, (c, c), 1)
        tri_ref[...] = jnp.where(row >= col, 1.0, 0.0).astype(BF16)
        x = row ^ col
        mask_ref[0] = jnp.where(x == 0, 1.0, 0.0).astype(F32)
        for lv in range(1, base_lv + 1):
            mask_ref[lv] = jnp.where(x < (1 << lv), 1.0, 0.0).astype(F32)
        bmask_ref[...] = jnp.where((x < HG_BASE) & (row >= col), 1.0, 0.0).astype(F32)
        xh = lax.broadcasted_iota(I32, (hc, hc), 0) ^ lax.broadcasted_iota(I32, (hc, hc), 1)
        for lv in range(base_lv, nlev - 1):
            cmask_ref[lv - base_lv] = jnp.where(xh < (1 << lv), 1.0, 0.0).astype(F32)

    @pl.when(pl.program_id(2) == 0)
    def _():
        st_ref[...] = jnp.zeros_like(st_ref)

    lbl = lbl_ref[...]
    rows = [lbl[i:i + 1, :] for i in range(lbl.shape[0])]
    mx = functools.reduce(jnp.maximum, rows)
    es = [jnp.exp(r - mx) for r in rows]
    lb = functools.reduce(lambda a, b: a + b, es[:layer + 1]) / functools.reduce(lambda a, b: a + b, es)

    q_all = _silu(q_ref[...].astype(F32))
    forget_all = lb + (1.0 - lb) * _sigmoid(f_ref[...].astype(F32))
    k_all = 1.0 - forget_all
    v_all = v_ref[...]

    w = heads * dh
    bb = _dot(tri_ref[...], jnp.concatenate(_split_bf16(jnp.log2(forget_all), 3), axis=1))
    b_all = bb[:, 0:w] + bb[:, w:2 * w] + bb[:, 2 * w:3 * w]

    def lanes(x, h):
        return x[:, h * dh:(h + 1) * dh]

    for h in hs:
        b_ref[h] = lanes(b_all, h)

    q = [lanes(q_all, h) for h in hs]
    k = [lanes(k_all, h) for h in hs]
    v = [lanes(v_all, h) for h in hs]
    b = [lanes(b_all, h) for h in hs]

    def b_row(h, r, n):
        return jnp.broadcast_to(b_ref[h, pl.ds(r, 1), :], (n, dh))

    blk_i = lax.broadcasted_iota(I32, (c // HG_BASE, dh), 0)
    base_ok = []
    for h in hs:
        ends = b_ref[h, pl.ds(HG_BASE - 1, c // HG_BASE, stride=HG_BASE), :]
        drop = jnp.where(blk_i == 0, 0.0, pltpu.roll(ends, 1, axis=0)) - ends
        base_ok.append(jnp.max(drop) <= HG_BASE_MAX_DECAY)

    qs_l, ks_l, vs_l = [], [], []
    for lv in range(base_lv, nlev):
        m = 1 << lv
        n = 2 * m
        for h in hs:
            qs, ks, vs = [], [], []
            for a in range(c // n):
                mid = b_row(h, a * n + m - 1, m)
                qs.append(q[h][a * n + m:(a + 1) * n] * jnp.exp2(b[h][a * n + m:(a + 1) * n] - mid))
                ks.append(k[h][a * n:a * n + m] * jnp.exp2(mid - b[h][a * n:a * n + m]))
                vs.append(v[h][a * n:a * n + m])
            qs_l.append(jnp.concatenate(qs, axis=0).astype(BF16))
            ks_l.append(jnp.concatenate(ks, axis=0).astype(BF16))
            vs_l.append(jnp.concatenate(vs, axis=0))
    qb16, kb16, qe, kd, b_last, st = [], [], [], [], [], []
    for h in hs:
        start = jnp.concatenate(
            [jnp.zeros((HG_BASE, dh), F32)] + [b_row(h, j * HG_BASE - 1, HG_BASE) for j in range(1, c // HG_BASE)],
            axis=0)
        dlt = jnp.maximum(b[h] - start, -HG_BASE_MAX_DECAY)
        qb16.append((q[h] * jnp.exp2(dlt)).astype(BF16))
        kb16.append((k[h] * jnp.exp2(-dlt)).astype(BF16))
        b_last.append(b_row(h, c - 1, c))
        qe.append((q[h] * jnp.exp2(b[h])).astype(BF16))
        kd.append((k[h] * jnp.exp2(b_last[h] - b[h])).astype(BF16))
        st.append(st_ref[h])

    nprod = nbig * heads
    a_l = [_dot_nt(qs_l[i], ks_l[i]) for i in range(nprod)]
    a16 = [_dot_nt(qb16[h], kb16[h]) for h in hs]
    o_inter = [_dot_nt(qe[h], st[h].astype(BF16)) for h in hs]
    for h in hs:
        st_ref[h] = st[h] * jnp.exp2(b_last[h][0:1, :]) + _dot_tn(v[h], kd[h])

    a_l = [(a_l[i] * cmask_ref[i // heads] if i // heads < nbig - 1 else a_l[i]).astype(BF16) for i in range(nprod)]
    a16 = [(a16[h] * bmask_ref[...]).astype(BF16) for h in hs]
    o_l = [_dot(a_l[i], vs_l[i]) for i in range(nprod)]
    oi_ref[...] = jnp.concatenate([_dot(a16[h], v[h]) for h in hs], axis=1)

    rest = []
    for h in hs:
        pieces = [None] * (c // HG_BASE)
        for lvi in range(nbig):
            m = HG_BASE << lvi
            per = m // HG_BASE
            o_c = o_l[lvi * heads + h]
            for a in range(c // (2 * m)):
                for u in range(per):
                    dst = (a * 2 * m + m) // HG_BASE + u
                    src = o_c[(a * per + u) * HG_BASE:(a * per + u + 1) * HG_BASE]
                    pieces[dst] = src if pieces[dst] is None else pieces[dst] + src
        zero_slab = jnp.zeros((HG_BASE, dh), F32)
        rest.append(o_inter[h] + jnp.concatenate([zero_slab if p is None else p for p in pieces], axis=0))
    rest_ref[...] = jnp.concatenate(rest, axis=1)

    for h in hs:
        @pl.when(jnp.logical_not(base_ok[h]))
        def _(h=h):
            rowi = lax.broadcasted_iota(I32, (c, dh), 0)
            sub = lax.broadcasted_iota(I32, (SUBLANES, dh), 0)
            ntile = c // SUBLANES
            forget = lanes(forget_all, h)
            attn = _dot_nt(q[h].astype(BF16), k[h].astype(BF16)) * mask_ref[0]
            for lv in range(base_lv):
                m = 1 << lv
                isq = (rowi & m) != 0
                if m == 1:
                    e = jnp.where(isq, forget, 1.0)
                else:
                    if m >= SUBLANES:
                        tiles = [b_row(h, (j * SUBLANES // (2 * m)) * 2 * m + m - 1, SUBLANES) for j in range(ntile)]
                    elif m == 4:
                        tiles = [b_row(h, j * SUBLANES + 3, SUBLANES) for j in range(ntile)]
                    else:
                        tiles = [jnp.where(sub < 4, b_row(h, j * SUBLANES + 1, SUBLANES),
                                           b_row(h, j * SUBLANES + 5, SUBLANES)) for j in range(ntile)]
                    mid = jnp.concatenate(tiles, axis=0)
                    e = jnp.exp2(jnp.where(isq, b[h] - mid, mid - b[h]))
                qt = jnp.where(isq, q[h] * e, 0.0).astype(BF16)
                kt = jnp.where(isq, 0.0, k[h] * e).astype(BF16)
                attn = attn + _dot_nt(qt, kt) * mask_ref[lv + 1]
            oi_ref[:, h * dh:(h + 1) * dh] = _dot(attn.astype(BF16), v[h])

    o = oi_ref[...] + rest_ref[...]
    o = jnp.concatenate(
        [lanes(o, h) * lax.rsqrt(jnp.mean(lanes(o, h) * lanes(o, h), axis=-1, keepdims=True) + RMS_EPS) for h in hs],
        axis=1)
    o = o * gain_ref[...] * _silu(g_ref[...].astype(F32))
    o_ref[...] = o.astype(o_ref.dtype)


def _hgrn_mixer(proj, lb_logits, head_gain, layer, j, bsz, seq):
    t, d4 = proj.shape
    d = d4 // 4
    nh = d // HG_HEAD_DIM
    heads = next(n for n in (16, 8, 4, 2, 1) if nh % n == 0)
    w = heads * HG_HEAD_DIM
    nhp = nh // heads
    chunk = 256 if seq % 256 == 0 else 128
    nc = seq // chunk
    nlev = int(math.log2(chunk))
    base_lv = int(math.log2(HG_BASE))

    def col(part):
        return lambda b, h, c: (b * nc + c, part * nhp + h)

    return pl.pallas_call(
        functools.partial(_hgrn_kernel, layer=layer, chunk=chunk, heads=heads),
        grid=(bsz, nhp, nc),
        in_specs=[
            pl.BlockSpec((chunk, w), col(0)),
            pl.BlockSpec((chunk, w), col(1)),
            pl.BlockSpec((chunk, w), col(2)),
            pl.BlockSpec((chunk, w), col(3)),
            pl.BlockSpec((lb_logits.shape[0], w), lambda b, h, c: (0, h)),
            pl.BlockSpec((None, 1, w), lambda b, h, c: (j, 0, h)),
        ],
        out_specs=pl.BlockSpec((chunk, w), lambda b, h, c: (b * nc + c, h)),
        out_shape=jax.ShapeDtypeStruct((t, d), BF16),
        scratch_shapes=[
            pltpu.VMEM((heads, HG_HEAD_DIM, HG_HEAD_DIM), F32),
            pltpu.VMEM((heads, chunk, HG_HEAD_DIM), F32),
            pltpu.VMEM((chunk, w), F32),
            pltpu.VMEM((chunk, w), F32),
            pltpu.VMEM((chunk, chunk), BF16),
            pltpu.VMEM((base_lv + 1, chunk, chunk), F32),
            pltpu.VMEM((chunk, chunk), F32),
            pltpu.VMEM((nlev - 1 - base_lv, chunk // 2, chunk // 2), F32),
        ],
        compiler_params=_cparams(3),
        name="hgrn2_mixer",
    )(proj, proj, proj, proj, lb_logits.astype(F32), head_gain.astype(F32).reshape(head_gain.shape[0], 1, d))


def _t5_bucket(dist):
    max_exact = N_BUCKETS // 2
    n = jnp.maximum(dist, 0)
    large = max_exact + (jnp.log(jnp.maximum(n, 1).astype(F32) / max_exact)
                         / math.log(MAX_DISTANCE / max_exact)
                         * (N_BUCKETS - max_exact)).astype(I32)
    large = jnp.minimum(large, N_BUCKETS - 1)
    return jnp.where(n < max_exact, n, large)


def _attn_kernel(rb_ref, sink_ref, q_ref, kp_ref, kc_ref, vp_ref, vc_ref, bucket_ref, o_ref,
                 bias_ref, *, group, pairs, layer_j):
    blk = ATT_BLOCK
    hd = ATT_HEAD_DIM
    pr = pl.program_id(0)
    i = pl.program_id(2)
    heads_per_step = 2 * group * pairs
    upper = lax.broadcasted_iota(I32, (blk, blk), 1) > lax.broadcasted_iota(I32, (blk, blk), 0)

    @pl.when((pl.program_id(1) == 0) & (i == 0))
    def _():
        bucket = bucket_ref[...]
        bucket_m = jnp.where(upper, bucket[:, 0:blk], bucket[:, blk:2 * blk])

        def per_head(hh, carry):
            h = pr * heads_per_step + hh
            tbl = jnp.zeros((blk, blk), F32)
            for bk in range(N_BUCKETS):
                tbl = jnp.where(bucket_m == bk, rb_ref[bk, h], tbl)
            bias_ref[hh] = tbl
            bias_ref[heads_per_step + hh] = jnp.where(upper, NEG_INF, tbl)
            return carry

        lax.fori_loop(0, heads_per_step, per_head, 0)

    lane = lax.broadcasted_iota(I32, (2 * blk, LANES), 1)
    table0 = jnp.where(i == 0, heads_per_step, 0)

    q_all = q_ref[...] * (hd ** -0.5)

    kbds, vbds = [], []
    for pp in range(pairs):
        kk = jnp.concatenate([kp_ref[:, pp * LANES:(pp + 1) * LANES], kc_ref[:, pp * LANES:(pp + 1) * LANES]],
                             axis=0).astype(F32)
        vv = jnp.concatenate([vp_ref[:, pp * LANES:(pp + 1) * LANES], vc_ref[:, pp * LANES:(pp + 1) * LANES]],
                             axis=0).astype(F32)
        for c in range(2):
            if c == 0:
                klo = jnp.where(lane < hd, kk, 0.0)
                khi = pltpu.roll(klo, hd, axis=1)
                vlo = jnp.where(lane < hd, vv, 0.0)
                vhi = pltpu.roll(vlo, hd, axis=1)
            else:
                khi = jnp.where(lane >= hd, kk, 0.0)
                klo = pltpu.roll(khi, hd, axis=1)
                vhi = jnp.where(lane >= hd, vv, 0.0)
                vlo = pltpu.roll(vhi, hd, axis=1)
            kbds.append(jnp.concatenate([klo, khi], axis=0).astype(BF16))
            vbds.append(jnp.concatenate([vlo, vhi], axis=0).astype(BF16))

    half = group // 2
    tiles = [(c, p) for c in range(2 * pairs) for p in range(half)]
    lgs = [_dot_nt(q_all[:, (c * half + p) * LANES:(c * half + p + 1) * LANES], kbds[c]) for c, p in tiles]
    p2s, rinvs = [], []
    for (c, p), lg in zip(tiles, lgs):
        probs, rinv = [], []
        for hh in range(2):
            hl = c * group + 2 * p + hh
            sk = sink_ref[layer_j, pr * heads_per_step + hl]
            c0 = hh * 2 * blk
            l = jnp.where(upper, lg[:, c0:c0 + blk], lg[:, c0 + blk:c0 + 2 * blk]) + bias_ref[table0 + hl]
            mx = jnp.maximum(jnp.max(l, axis=-1, keepdims=True), sk)
            pe = jnp.exp(l - mx)
            rinv.append(1.0 / (jnp.sum(pe, axis=-1, keepdims=True) + jnp.exp(sk - mx)))
            pe = pe.astype(BF16)
            zero = jnp.zeros_like(pe)
            probs += [jnp.where(upper, pe, zero), jnp.where(upper, zero, pe)]
        p2s.append(jnp.concatenate(probs, axis=1))
        rinvs.append(rinv)
    lane_o = lax.broadcasted_iota(I32, (blk, LANES), 1)
    for (c, p), p2, rinv in zip(tiles, p2s, rinvs):
        tile = c * half + p
        o = _dot(p2, vbds[c]) * jnp.where(lane_o < hd, rinv[0], rinv[1])
        o_ref[:, tile * LANES:(tile + 1) * LANES] = o.astype(o_ref.dtype)


def _attn_mixer(proj, sinks, rel_bias, layer_j, bsz, seq, d):
    t, att_in = proj.shape
    kvw = (att_in - d) // 2
    n_heads = d // ATT_HEAD_DIM
    kvh = kvw // ATT_HEAD_DIM
    group = n_heads // kvh
    assert kvh % 2 == 0 and group % 2 == 0
    assert WINDOW == ATT_BLOCK
    blk = ATT_BLOCK
    nb = seq // blk
    npair = kvh // 2
    pairs = 1
    ngrp = npair // pairs
    qw = 2 * group * ATT_HEAD_DIM * pairs
    kw = LANES * pairs
    k0 = d // kw
    v0 = (d + kvw) // kw
    assert d % kw == 0 and (d + kvw) % kw == 0

    qi = jnp.arange(blk)[:, None]
    sj = jnp.arange(2 * blk)[None, :]
    bucket = _t5_bucket(qi + blk - sj).astype(I32)

    def prev(i):
        return jnp.maximum(i - 1, 0)

    grid_spec = pltpu.PrefetchScalarGridSpec(
        num_scalar_prefetch=2,
        grid=(ngrp, bsz, nb),
        in_specs=[
            pl.BlockSpec((blk, qw), lambda p, b, i, *_: (b * nb + i, p)),
            pl.BlockSpec((blk, kw), lambda p, b, i, *_: (b * nb + prev(i), k0 + p)),
            pl.BlockSpec((blk, kw), lambda p, b, i, *_: (b * nb + i, k0 + p)),
            pl.BlockSpec((blk, kw), lambda p, b, i, *_: (b * nb + prev(i), v0 + p)),
            pl.BlockSpec((blk, kw), lambda p, b, i, *_: (b * nb + i, v0 + p)),
            pl.BlockSpec((blk, 2 * blk), lambda p, b, i, *_: (0, 0)),
        ],
        out_specs=pl.BlockSpec((blk, qw), lambda p, b, i, *_: (b * nb + i, p)),
        scratch_shapes=[pltpu.VMEM((4 * group * pairs, blk, blk), F32)],
    )
    return pl.pallas_call(
        functools.partial(_attn_kernel, group=group, pairs=pairs, layer_j=layer_j),
        grid_spec=grid_spec,
        out_shape=jax.ShapeDtypeStruct((t, d), BF16),
        compiler_params=_cparams(3),
        name="swa_sink_mixer",
    )(rel_bias.astype(F32), sinks.astype(F32), proj, proj, proj, proj, proj, bucket)


def _layer_norm_rows(z, g, b):
    mu = jnp.mean(z, axis=-1, keepdims=True)
    zc = z - mu
    var = jnp.mean(zc * zc, axis=-1, keepdims=True)
    return zc * lax.rsqrt(var + LN_EPS) * g + b


def _ln_router_kernel(x_ref, y_ref, gate_ref, sc_ref, sh_ref, lng_ref, lnb_ref, wr_ref, br_ref,
                      xo_ref, hf_ref, meta_ref, cnt_ref, carry_ref, wsplit_ref, *, alpha, n_groups, n_experts):
    i = pl.program_id(0)
    bm = x_ref.shape[0]
    epg = n_experts // n_groups

    @pl.when(i == 0)
    def _():
        carry_ref[...] = jnp.zeros_like(carry_ref)
        wh, wl = _split_bf16(wr_ref[...], 2)
        wsplit_ref[:, 0:LANES] = wh
        wsplit_ref[:, LANES:2 * LANES] = wl

    z = alpha * x_ref[...] + (1.0 + gate_ref[...]) * y_ref[...]
    xn = _layer_norm_rows(z, lng_ref[...], lnb_ref[...])
    xo_ref[...] = xn
    hf = xn * (1.0 + sc_ref[...]) + sh_ref[...]
    xh = hf.astype(BF16)
    xh32 = xh.astype(F32)
    hf_ref[...] = _pack_bf16_pairs(xh32, rounded=True)

    xl = (hf - xh32).astype(BF16)
    hh = _dot(xh, wsplit_ref[...])
    lg = hh[:, 0:LANES] + hh[:, LANES:2 * LANES] + _dot(xl, wsplit_ref[:, 0:LANES]) + br_ref[...]

    lane = lax.broadcasted_iota(I32, (bm, LANES), 1)
    lanef = lane.astype(F32)
    big = float(LANES)

    gl = jnp.where(lane < n_groups, lg, NEG_INF)
    gmax = jnp.max(gl, axis=-1, keepdims=True)
    gsel = jnp.min(jnp.where(gl == gmax, lanef, big), axis=-1, keepdims=True)
    p_group = 1.0 / jnp.sum(jnp.exp(gl - gmax), axis=-1, keepdims=True)

    lo = n_groups + gsel * epg
    el = jnp.where((lanef >= lo) & (lanef < lo + epg), lg, NEG_INF)
    m1 = jnp.max(el, axis=-1, keepdims=True)
    i1 = jnp.min(jnp.where(el == m1, lanef, big), axis=-1, keepdims=True)
    el2 = jnp.where(lanef == i1, NEG_INF, el)
    m2 = jnp.max(el2, axis=-1, keepdims=True)
    i2 = jnp.min(jnp.where(el2 == m2, lanef, big), axis=-1, keepdims=True)
    e21 = jnp.exp(m2 - m1)
    g0 = p_group / (1.0 + e21)
    g1 = g0 * e21

    oh0 = lanef == i1
    oh1 = lanef == i2
    cnt = jnp.where(oh0 | oh1, 1.0, 0.0)
    row = lax.broadcasted_iota(I32, (bm, bm), 0)
    col = lax.broadcasted_iota(I32, (bm, bm), 1)
    stril = jnp.where(row > col, 1.0, 0.0).astype(BF16)
    before = _dot(stril, cnt.astype(BF16)) + carry_ref[...]
    r0 = jnp.sum(jnp.where(oh0, before, 0.0), axis=-1, keepdims=True)
    r1 = jnp.sum(jnp.where(oh1, before, 0.0), axis=-1, keepdims=True)
    carry_ref[...] = carry_ref[...] + jnp.sum(cnt, axis=0, keepdims=True)
    cnt_ref[...] = carry_ref[...]

    meta = jnp.where(lane == 0, i1 - n_groups, 0.0)
    meta = jnp.where(lane == 1, i2 - n_groups, meta)
    meta = jnp.where(lane == 2, g0, meta)
    meta = jnp.where(lane == 3, g1, meta)
    meta = jnp.where(lane == 4, r0, meta)
    meta = jnp.where(lane == 5, r1, meta)
    meta_ref[...] = meta


def _ln_router(x2, y2, mod, row_of, ln_g, ln_b, w_router, b_router, alpha, n_groups, n_experts, seq):
    t, d = x2.shape
    bm = 128
    nbs = seq // bm

    def mrow(which):
        return lambda i: (row_of(i // nbs, which), 0, 0)

    return pl.pallas_call(
        functools.partial(_ln_router_kernel, alpha=alpha, n_groups=n_groups, n_experts=n_experts),
        grid=(t // bm,),
        in_specs=[
            pl.BlockSpec((bm, d), lambda i: (i, 0)),
            pl.BlockSpec((bm, d), lambda i: (i, 0)),
            pl.BlockSpec((None, 1, d), mrow(2)),
            pl.BlockSpec((None, 1, d), mrow(4)),
            pl.BlockSpec((None, 1, d), mrow(3)),
            pl.BlockSpec((1, d), lambda i: (0, 0)),
            pl.BlockSpec((1, d), lambda i: (0, 0)),
            pl.BlockSpec((d, LANES), lambda i: (0, 0)),
            pl.BlockSpec((1, LANES), lambda i: (0, 0)),
        ],
        out_specs=[
            pl.BlockSpec((bm, d), lambda i: (i, 0)),
            pl.BlockSpec((bm, d // 2), lambda i: (i, 0)),
            pl.BlockSpec((bm, LANES), lambda i: (i, 0)),
            pl.BlockSpec((1, LANES), lambda i: (0, 0)),
        ],
        out_shape=[
            jax.ShapeDtypeStruct((t, d), F32),
            jax.ShapeDtypeStruct((t, d // 2), jnp.uint32),
            jax.ShapeDtypeStruct((t, LANES), F32),
            jax.ShapeDtypeStruct((1, LANES), F32),
        ],
        scratch_shapes=[pltpu.VMEM((1, LANES), F32), pltpu.VMEM((d, 2 * LANES), BF16)],
        compiler_params=_cparams(1),
        name="ln_router",
    )(x2, y2, mod, mod, mod, ln_g, ln_b, w_router, b_router)


MOE_GATHER_SLOTS = 3


def _cast_rows(src_ref, dst_ref, rows=128):
    def body(r, carry):
        sl = pl.ds(pl.multiple_of(r * rows, rows), rows)
        dst_ref[sl, :] = src_ref[sl, :].astype(dst_ref.dtype)
        return carry
    lax.fori_loop(0, src_ref.shape[0] // rows, body, 0)


def _moe_kernel(blk_e_ref, nxt_e_ref, first_ref, grp_ref, nused_ref,
                tok0_ref, tok1_ref, tok2_ref, hf_hbm, wgu_hbm, wdn_hbm, y_ref,
                xbuf, wgu_st, wdn_st, wgu_bf, wdn_bf, gsem, wsem, *, layer, d_expert):
    i = pl.program_id(0)
    bm = xbuf.shape[1]
    nused = nused_ref[0]
    slot = i % MOE_GATHER_SLOTS

    def row_copy(tok, r, s):
        return pltpu.make_async_copy(hf_hbm.at[pl.ds(tok, 1), :], xbuf.at[s, pl.ds(r, 1), :], gsem.at[s])

    def issue_rows(tok_ref, s):
        for r in range(bm):
            row_copy(tok_ref[0, r], r, s).start()

    def wait_rows(s):
        pltpu.make_async_copy(hf_hbm.at[pl.ds(0, bm), :], xbuf.at[s], gsem.at[s]).wait()

    def wgu_copy(e):
        return pltpu.make_async_copy(wgu_hbm.at[layer, e], wgu_st, wsem.at[0])

    def wdn_copy(e, s):
        return pltpu.make_async_copy(wdn_hbm.at[layer, e], wdn_st.at[s], wsem.at[1 + s])

    @pl.when(i == 0)
    def _():
        issue_rows(tok0_ref, 0)
        wgu_copy(blk_e_ref[0]).start(priority=1)
        wdn_copy(blk_e_ref[0], 0).start(priority=1)

    @pl.when((i == 0) & (nused > 1))
    def _():
        issue_rows(tok1_ref, 1)

    @pl.when((i < nused) & (first_ref[i] == 1))
    def _():
        par = grp_ref[i] % 2
        has_next = nxt_e_ref[i] >= 0

        @pl.when(has_next)
        def _():
            wdn_copy(nxt_e_ref[i], 1 - par).start(priority=1)

        wgu_copy(blk_e_ref[i]).wait()
        _cast_rows(wgu_st, wgu_bf)

        @pl.when(has_next)
        def _():
            wgu_copy(nxt_e_ref[i]).start(priority=1)

        wdn_copy(blk_e_ref[i], par).wait()
        _cast_rows(wdn_st.at[par], wdn_bf)

    @pl.when(i + 2 < nused)
    def _():
        issue_rows(tok2_ref, (i + 2) % MOE_GATHER_SLOTS)

    @pl.when(i < nused)
    def _():
        wait_rows(slot)
        xb = _unpack_bf16_pairs(xbuf[slot]).astype(BF16)
        a = _dot(xb, wgu_bf[...])
        h = _silu(a[:, :d_expert]) * a[:, d_expert:]
        y_ref[...] = _pack_bf16_pairs(_dot(h.astype(BF16), wdn_bf[...]))

    @pl.when(i >= nused)
    def _():
        y_ref[...] = jnp.zeros_like(y_ref)


def _moe_experts(hf, tok_pad, blk_e, nxt_e, first, grp, nused, w_gate_up, w_down, layer, bm):
    d = w_down.shape[3]
    nb = blk_e.shape[0]
    d_expert = w_down.shape[2]
    tok3 = tok_pad.reshape(nb, 1, bm)
    grid_spec = pltpu.PrefetchScalarGridSpec(
        num_scalar_prefetch=5,
        grid=(nb,),
        in_specs=[
            pl.BlockSpec((None, 1, bm), lambda i, *_: (i, 0, 0), memory_space=pltpu.SMEM),
            pl.BlockSpec((None, 1, bm), lambda i, *_: (jnp.minimum(i + 1, nb - 1), 0, 0), memory_space=pltpu.SMEM),
            pl.BlockSpec((None, 1, bm), lambda i, *_: (jnp.minimum(i + 2, nb - 1), 0, 0), memory_space=pltpu.SMEM),
            pl.BlockSpec(memory_space=pl.ANY),
            pl.BlockSpec(memory_space=pl.ANY),
            pl.BlockSpec(memory_space=pl.ANY),
        ],
        out_specs=pl.BlockSpec((bm, d // 2), lambda i, *_: (i, 0)),
        scratch_shapes=[
            pltpu.VMEM((MOE_GATHER_SLOTS, bm, d // 2), jnp.uint32),
            pltpu.VMEM((d, 2 * d_expert), F32),
            pltpu.VMEM((2, d_expert, d), F32),
            pltpu.VMEM((d, 2 * d_expert), BF16),
            pltpu.VMEM((d_expert, d), BF16),
            pltpu.SemaphoreType.DMA((MOE_GATHER_SLOTS,)),
            pltpu.SemaphoreType.DMA((3,)),
        ],
    )
    return pl.pallas_call(
        functools.partial(_moe_kernel, layer=layer, d_expert=d_expert),
        grid_spec=grid_spec,
        out_shape=jax.ShapeDtypeStruct((nb * bm, d // 2), jnp.uint32),
        compiler_params=_cparams(1),
        name="moe_experts",
    )(blk_e, nxt_e, first, grp, nused, tok3, tok3, tok3, hf, w_gate_up, w_down)


def _moe_plan(meta, counts_row, n_groups, n_experts, bm):
    t = meta.shape[0]
    eid = meta[:, 0:2].astype(I32)
    rank = meta[:, 4:6].astype(I32)
    counts = counts_row[0, n_groups:n_groups + n_experts].astype(I32)
    padded = (counts + bm - 1) // bm * bm
    pad_end = jnp.cumsum(padded)
    pad_start = pad_end - padded
    start_of = jnp.sum(jnp.where(eid[..., None] == jnp.arange(n_experts, dtype=I32), pad_start, 0), axis=-1)
    dest = start_of + rank
    nb = (2 * t) // bm + n_experts
    nused = pad_end[-1] // bm
    ids = jnp.arange(nb, dtype=I32)
    raw_e = jnp.minimum(jnp.sum((pad_end[None, :] <= (ids * bm)[:, None]).astype(I32), axis=1), n_experts - 1)
    used = ids < nused
    blk_e = jnp.where(used, raw_e, raw_e[nused - 1])
    prev_e = jnp.concatenate([jnp.full((1,), -1, I32), blk_e[:-1]])
    first = (used & (blk_e != prev_e)).astype(I32)
    grp = jnp.cumsum(first) - 1
    key = jnp.where(used, blk_e, n_experts)
    nxt_idx = jnp.sum((key[None, :] <= blk_e[:, None]).astype(I32), axis=1)
    nxt_e = jnp.where(nxt_idx < nused, key[jnp.minimum(nxt_idx, nb - 1)], -1).astype(I32)
    tok = jnp.repeat(jnp.arange(t, dtype=I32), 2)
    tok_pad = jnp.zeros((nb * bm,), I32).at[dest.reshape(-1)].set(tok)
    return dest, tok_pad, blk_e, nxt_e, first, grp.astype(I32), nused.reshape(1).astype(I32)


def _ln_combine_kernel(*refs, alpha, with_next):
    if with_next:
        (dc_ref, dn_ref, x_ref, meta_ref, gate_ref, lng_ref, lnb_ref, sc_ref, sh_ref, y_hbm,
         xo_ref, hm_ref, ybuf, sem) = refs
    else:
        (dc_ref, dn_ref, x_ref, meta_ref, gate_ref, lng_ref, lnb_ref, y_hbm,
         xo_ref, ybuf, sem) = refs
    i = pl.program_id(0)
    n = pl.num_programs(0)
    bm = x_ref.shape[0]
    slot = i % 2

    def row_copy(src, r, s):
        return pltpu.make_async_copy(y_hbm.at[pl.ds(src, 1), :], ybuf.at[s, pl.ds(r, 1), :], sem.at[s])

    def issue_rows(d_ref, s):
        for r in range(2 * bm):
            row_copy(d_ref[0, r], r, s).start()

    def wait_rows(s):
        pltpu.make_async_copy(y_hbm.at[pl.ds(0, 2 * bm), :], ybuf.at[s], sem.at[s]).wait()

    @pl.when(i == 0)
    def _():
        issue_rows(dc_ref, 0)

    @pl.when(i + 1 < n)
    def _():
        issue_rows(dn_ref, 1 - slot)

    wait_rows(slot)
    meta = meta_ref[...]
    y = (_unpack_bf16_pairs(ybuf[slot, 0:bm, :]) * meta[:, 2:3]
         + _unpack_bf16_pairs(ybuf[slot, bm:2 * bm, :]) * meta[:, 3:4])
    z = alpha * x_ref[...] + (1.0 + gate_ref[...]) * y
    xn = _layer_norm_rows(z, lng_ref[...], lnb_ref[...])
    xo_ref[...] = xn
    if with_next:
        hm_ref[...] = (xn * (1.0 + sc_ref[...]) + sh_ref[...]).astype(hm_ref.dtype)


def _ln_combine(x2, meta, dest, ysort, mod, row_of, next_row_of, ln_g, ln_b, alpha, seq):
    t, d = x2.shape
    bm = 128
    nbs = seq // bm
    nblk = t // bm
    with_next = next_row_of is not None
    dest3 = dest.reshape(nblk, bm, 2).transpose(0, 2, 1).reshape(nblk, 1, 2 * bm)

    def mrow(fn, which):
        return lambda i: (fn(i // nbs, which), 0, 0)

    in_specs = [
        pl.BlockSpec((None, 1, 2 * bm), lambda i: (i, 0, 0), memory_space=pltpu.SMEM),
        pl.BlockSpec((None, 1, 2 * bm), lambda i: (jnp.minimum(i + 1, nblk - 1), 0, 0), memory_space=pltpu.SMEM),
        pl.BlockSpec((bm, d), lambda i: (i, 0)),
        pl.BlockSpec((bm, LANES), lambda i: (i, 0)),
        pl.BlockSpec((None, 1, d), mrow(row_of, 5)),
        pl.BlockSpec((1, d), lambda i: (0, 0)),
        pl.BlockSpec((1, d), lambda i: (0, 0)),
    ]
    args = [dest3, dest3, x2, meta, mod, ln_g, ln_b]
    out_specs = [pl.BlockSpec((bm, d), lambda i: (i, 0))]
    out_shape = [jax.ShapeDtypeStruct((t, d), F32)]
    if with_next:
        in_specs += [pl.BlockSpec((None, 1, d), mrow(next_row_of, 1)),
                     pl.BlockSpec((None, 1, d), mrow(next_row_of, 0))]
        args += [mod, mod]
        out_specs.append(pl.BlockSpec((bm, d), lambda i: (i, 0)))
        out_shape.append(jax.ShapeDtypeStruct((t, d), BF16))
    in_specs.append(pl.BlockSpec(memory_space=pl.ANY))
    args.append(ysort)
    return pl.pallas_call(
        functools.partial(_ln_combine_kernel, alpha=alpha, with_next=with_next),
        grid=(nblk,),
        in_specs=in_specs,
        out_specs=out_specs,
        out_shape=out_shape,
        scratch_shapes=[pltpu.VMEM((2, 2 * bm, d // 2), jnp.uint32), pltpu.SemaphoreType.DMA((2,))],
        compiler_params=_cparams(1),
        name="ln_moe_combine",
    )(*args)


def kernel(x, c, w_ada, b_ada, ln_g, ln_b, w_in_a, lb_logits, head_gain_a, w_out_a, w_in_b, attn_sinks, w_out_b, rel_bias, w_router_group, b_router_group, w_router_expert, b_router_expert, w_gate_up, w_down):
    bsz, seq, d = x.shape
    depth = w_ada.shape[0]
    n_groups = w_router_group.shape[2]
    n_experts = w_router_expert.shape[2]
    alpha = (2 * depth) ** 0.25
    t = bsz * seq
    moe_bm = 128

    mod = _ada_modulation(c, w_ada, b_ada)

    def row_of_layer(layer):
        return lambda b, which: (layer * bsz + b) * 6 + which

    x2 = x.reshape(t, d).astype(F32)
    row0 = row_of_layer(0)
    hm = _modulate(x2, mod, lambda b: row0(b, 1), lambda b: row0(b, 0), bsz, seq)

    for layer in range(depth):
        row_of = row_of_layer(layer)
        j = layer // 2
        if layer % 2 == 0:
            proj = _matmul(hm, w_in_a, j, BF16)
            o = _hgrn_mixer(proj, lb_logits, head_gain_a, layer, j, bsz, seq)
            y = _matmul(o, w_out_a, j, F32)
        else:
            proj = _matmul(hm, w_in_b, j, BF16)
            o = _attn_mixer(proj, attn_sinks, rel_bias, j, bsz, seq, d)
            y = _matmul(o, w_out_b, j, F32)

        n_pad = LANES - n_groups - n_experts
        w_router = jnp.concatenate(
            [w_router_group[layer].astype(F32), w_router_expert[layer].astype(F32), jnp.zeros((d, n_pad), F32)], axis=1)
        b_router = jnp.concatenate(
            [b_router_group[layer].astype(F32), b_router_expert[layer].astype(F32), jnp.zeros((n_pad,), F32)]
        ).reshape(1, LANES)

        x2, hf, meta, counts = _ln_router(
            x2, y, mod, row_of, ln_g[layer, 0:1].astype(F32), ln_b[layer, 0:1].astype(F32),
            w_router, b_router, alpha, n_groups, n_experts, seq)
        dest, tok_pad, blk_e, nxt_e, first, grp, nused = _moe_plan(meta, counts, n_groups, n_experts, moe_bm)
        ysort = _moe_experts(hf, tok_pad, blk_e, nxt_e, first, grp, nused, w_gate_up, w_down, layer, moe_bm)
        next_row_of = row_of_layer(layer + 1) if layer + 1 < depth else None
        outs = _ln_combine(x2, meta, dest, ysort, mod, row_of, next_row_of,
                           ln_g[layer, 1:2].astype(F32), ln_b[layer, 1:2].astype(F32), alpha, seq)
        x2 = outs[0]
        if next_row_of is not None:
            hm = outs[1]

    return x2.reshape(bsz, seq, d).astype(x.dtype)
```

```python
import functools
import math

import jax
import jax.numpy as jnp
from jax import lax
from jax.experimental import pallas as pl
from jax.experimental.pallas import tpu as pltpu

F32 = jnp.float32
BF16 = jnp.bfloat16
I32 = jnp.int32

LANES = 128
SUBLANES = 8
V7X_VMEM_LIMIT_BYTES = 56 * 1024 * 1024

HG_HEAD_DIM = 128
ATT_HEAD_DIM = 64
ATT_BLOCK = 128
WINDOW = 128
N_BUCKETS = 32
MAX_DISTANCE = 128
LN_EPS = 1e-5
RMS_EPS = 1e-6
NEG_INF = float("-inf")


def _cparams(n_axes):
    return pltpu.CompilerParams(
        dimension_semantics=("arbitrary",) * n_axes,
        vmem_limit_bytes=V7X_VMEM_LIMIT_BYTES,
    )


def _sigmoid(x):
    return 0.5 * jnp.tanh(0.5 * x) + 0.5


def _silu(x):
    return x * _sigmoid(x)


def _dot_nt(a, b):
    return lax.dot_general(a, b, (((1,), (1,)), ((), ())), preferred_element_type=F32)


def _dot_tn(a, b):
    return lax.dot_general(a, b, (((0,), (0,)), ((), ())), preferred_element_type=F32)


def _dot(a, b):
    return jnp.dot(a, b, preferred_element_type=F32)


def _pack_bf16_pairs(x, rounded=False):
    n = x.shape[1] // 2
    bits = pltpu.bitcast(x if rounded else x.astype(BF16).astype(F32), jnp.uint32)
    return (bits[:, :n] >> 16) | (bits[:, n:] & jnp.uint32(0xFFFF0000))


def _unpack_bf16_pairs(w):
    lo = pltpu.bitcast(w << 16, F32)
    hi = pltpu.bitcast(w & jnp.uint32(0xFFFF0000), F32)
    return jnp.concatenate([lo, hi], axis=1)


def _split_bf16(x, parts):
    out = []
    r = x
    for _ in range(parts):
        h = r.astype(BF16)
        out.append(h)
        r = r - h.astype(F32)
    return out


def _ada_kernel(c_ref, w_ref, b_ref, o_ref):
    ca = _silu(c_ref[...]).astype(BF16)
    o_ref[...] = _dot(ca, w_ref[...].astype(BF16)) + b_ref[...]


def _ada_modulation(c, w_ada, b_ada):
    nl, d, n6 = w_ada.shape
    bsz = c.shape[0]
    rows = -(-bsz // SUBLANES) * SUBLANES
    c8 = jnp.zeros((rows, d), F32).at[:bsz].set(c.astype(F32))
    tn = 512
    out = pl.pallas_call(
        _ada_kernel,
        grid=(nl, n6 // tn),
        in_specs=[
            pl.BlockSpec((rows, d), lambda l, j: (0, 0)),
            pl.BlockSpec((None, d, tn), lambda l, j: (l, 0, j)),
            pl.BlockSpec((None, 1, tn), lambda l, j: (l, 0, j)),
        ],
        out_specs=pl.BlockSpec((None, rows, tn), lambda l, j: (l, 0, j)),
        out_shape=jax.ShapeDtypeStruct((nl, rows, n6), F32),
        compiler_params=_cparams(2),
        name="ada_modulation",
    )(c8, w_ada, b_ada.reshape(nl, 1, n6))
    return out[:, :bsz].reshape(nl * bsz * 6, 1, d)


def _modulate_kernel(x_ref, sc_ref, sh_ref, o_ref):
    o_ref[...] = (x_ref[...] * (1.0 + sc_ref[...]) + sh_ref[...]).astype(o_ref.dtype)


def _modulate(x2, mod, sc_row, sh_row, bsz, seq):
    t, d = x2.shape
    bs = min(512, seq)
    nbs = seq // bs
    return pl.pallas_call(
        _modulate_kernel,
        grid=(t // bs,),
        in_specs=[
            pl.BlockSpec((bs, d), lambda i: (i, 0)),
            pl.BlockSpec((None, 1, d), lambda i: (sc_row(i // nbs), 0, 0)),
            pl.BlockSpec((None, 1, d), lambda i: (sh_row(i // nbs), 0, 0)),
        ],
        out_specs=pl.BlockSpec((bs, d), lambda i: (i, 0)),
        out_shape=jax.ShapeDtypeStruct((t, d), BF16),
        compiler_params=_cparams(1),
        name="modulate",
    )(x2, mod, mod)


def _matmul_kernel(x_ref, w_hbm, o_ref, wbf_ref, stage_ref, sem, *, layer, bn, kc):
    j = pl.program_id(0)
    i = pl.program_id(1)
    nj = pl.num_programs(0)
    cur = j % 2

    def slab_copy(jb, c):
        return pltpu.make_async_copy(
            w_hbm.at[layer, pl.ds(c * kc, kc), pl.ds(jb * bn, bn)], stage_ref, sem.at[0])

    def cast_slab(c, buf):
        wbf_ref[buf, pl.ds(c * kc, kc), :] = stage_ref[...].astype(BF16)

    @pl.when((j == 0) & (i == 0))
    def _():
        def first_block(c, carry):
            cp = slab_copy(0, c)
            cp.start()
            cp.wait()
            cast_slab(pl.multiple_of(c, 1), 0)
            return carry
        lax.fori_loop(0, pl.num_programs(1), first_block, 0)

    @pl.when(j + 1 < nj)
    def _():
        slab_copy(j + 1, i).start()

    o_ref[...] = _dot(x_ref[...], wbf_ref[cur]).astype(o_ref.dtype)

    @pl.when(j + 1 < nj)
    def _():
        slab_copy(j + 1, i).wait()
        cast_slab(i, 1 - cur)


def _matmul(x, w3, layer, out_dtype):
    m, k = x.shape
    n = w3.shape[2]
    bm = min(1024, m)
    bn = 1024 if n % 1024 == 0 else 512
    steps = m // bm
    kc = k // steps
    assert k % steps == 0 and kc % SUBLANES == 0
    return pl.pallas_call(
        functools.partial(_matmul_kernel, layer=layer, bn=bn, kc=kc),
        grid=(n // bn, steps),
        in_specs=[
            pl.BlockSpec((bm, k), lambda j, i: (i, 0)),
            pl.BlockSpec(memory_space=pl.ANY),
        ],
        out_specs=pl.BlockSpec((bm, bn), lambda j, i: (i, j)),
        out_shape=jax.ShapeDtypeStruct((m, n), out_dtype),
        scratch_shapes=[
            pltpu.VMEM((2, k, bn), BF16),
            pltpu.VMEM((kc, bn), F32),
            pltpu.SemaphoreType.DMA((1,)),
        ],
        compiler_params=_cparams(2),
        name="dense_projection",
    )(x, w3)


HG_BASE = 16
HG_BASE_MAX_DECAY = 86.0


def _hgrn_kernel(q_ref, f_ref, v_ref, g_ref, lbl_ref, gain_ref, o_ref,
                 st_ref, b_ref, oi_ref, rest_ref, tri_ref, mask_ref, bmask_ref, cmask_ref, *, layer, chunk, heads):
    c = chunk
    hc = c // 2
    dh = HG_HEAD_DIM
    nlev = int(math.log2(c))
    base_lv = int(math.log2(HG_BASE))
    nbig = nlev - base_lv
    hs = range(heads)
    first = (pl.program_id(0) == 0) & (pl.program_id(1) == 0) & (pl.program_id(2) == 0)

    @pl.when(first)
    def _():
        row = lax.broadcasted_iota(I32, (c, c), 0)
        col = lax.broadcasted_iota(I32, (c, c), 1)
        tri_ref[...] = jnp.where(row >= col, 1.0, 0.0).astype(BF16)
        x = row ^ col
        mask_ref[0] = jnp.where(x == 0, 1.0, 0.0).astype(F32)
        for lv in range(1, base_lv + 1):
            mask_ref[lv] = jnp.where(x < (1 << lv), 1.0, 0.0).astype(F32)
        bmask_ref[...] = jnp.where((x < HG_BASE) & (row >= col), 1.0, 0.0).astype(F32)
        xh = lax.broadcasted_iota(I32, (hc, hc), 0) ^ lax.broadcasted_iota(I32, (hc, hc), 1)
        for lv in range(base_lv, nlev - 1):
            cmask_ref[lv - base_lv] = jnp.where(xh < (1 << lv), 1.0, 0.0).astype(F32)

    @pl.when(pl.program_id(2) == 0)
    def _():
        st_ref[...] = jnp.zeros_like(st_ref)

    lbl = lbl_ref[...]
    rows = [lbl[i:i + 1, :] for i in range(lbl.shape[0])]
    mx = functools.reduce(jnp.maximum, rows)
    es = [jnp.exp(r - mx) for r in rows]
    lb = functools.reduce(lambda a, b: a + b, es[:layer + 1]) / functools.reduce(lambda a, b: a + b, es)

    q_all = _silu(q_ref[...].astype(F32))
    forget_all = lb + (1.0 - lb) * _sigmoid(f_ref[...].astype(F32))
    k_all = 1.0 - forget_all
    v_all = v_ref[...]

    w = heads * dh
    bb = _dot(tri_ref[...], jnp.concatenate(_split_bf16(jnp.log2(forget_all), 3), axis=1))
    b_all = bb[:, 0:w] + bb[:, w:2 * w] + bb[:, 2 * w:3 * w]

    def lanes(x, h):
        return x[:, h * dh:(h + 1) * dh]

    for h in hs:
        b_ref[h] = lanes(b_all, h)

    q = [lanes(q_all, h) for h in hs]
    k = [lanes(k_all, h) for h in hs]
    v = [lanes(v_all, h) for h in hs]
    b = [lanes(b_all, h) for h in hs]

    def b_row(h, r, n):
        return jnp.broadcast_to(b_ref[h, pl.ds(r, 1), :], (n, dh))

    blk_i = lax.broadcasted_iota(I32, (c // HG_BASE, dh), 0)
    base_ok = []
    for h in hs:
        ends = b_ref[h, pl.ds(HG_BASE - 1, c // HG_BASE, stride=HG_BASE), :]
        drop = jnp.where(blk_i == 0, 0.0, pltpu.roll(ends, 1, axis=0)) - ends
        base_ok.append(jnp.max(drop) <= HG_BASE_MAX_DECAY)

    qs_l, ks_l, vs_l = [], [], []
    for lv in range(base_lv, nlev):
        m = 1 << lv
        n = 2 * m
        for h in hs:
            qs, ks, vs = [], [], []
            for a in range(c // n):
                mid = b_row(h, a * n + m - 1, m)
                qs.append(q[h][a * n + m:(a + 1) * n] * jnp.exp2(b[h][a * n + m:(a + 1) * n] - mid))
                ks.append(k[h][a * n:a * n + m] * jnp.exp2(mid - b[h][a * n:a * n + m]))
                vs.append(v[h][a * n:a * n + m])
            qs_l.append(jnp.concatenate(qs, axis=0).astype(BF16))
            ks_l.append(jnp.concatenate(ks, axis=0).astype(BF16))
            vs_l.append(jnp.concatenate(vs, axis=0))
    qb16, kb16, qe, kd, b_last, st = [], [], [], [], [], []
    for h in hs:
        start = jnp.concatenate(
            [jnp.zeros((HG_BASE, dh), F32)] + [b_row(h, j * HG_BASE - 1, HG_BASE) for j in range(1, c // HG_BASE)],
            axis=0)
        dlt = jnp.maximum(b[h] - start, -HG_BASE_MAX_DECAY)
        qb16.append((q[h] * jnp.exp2(dlt)).astype(BF16))
        kb16.append((k[h] * jnp.exp2(-dlt)).astype(BF16))
        b_last.append(b_row(h, c - 1, c))
        qe.append((q[h] * jnp.exp2(b[h])).astype(BF16))
        kd.append((k[h] * jnp.exp2(b_last[h] - b[h])).astype(BF16))
        st.append(st_ref[h])

    nprod = nbig * heads
    a_l = [_dot_nt(qs_l[i], ks_l[i]) for i in range(nprod)]
    a16 = [_dot_nt(qb16[h], kb16[h]) for h in hs]
    o_inter = [_dot_nt(qe[h], st[h].astype(BF16)) for h in hs]
    for h in hs:
        st_ref[h] = st[h] * jnp.exp2(b_last[h][0:1, :]) + _dot_tn(v[h], kd[h])

    a_l = [(a_l[i] * cmask_ref[i // heads] if i // heads < nbig - 1 else a_l[i]).astype(BF16) for i in range(nprod)]
    a16 = [(a16[h] * bmask_ref[...]).astype(BF16) for h in hs]
    o_l = [_dot(a_l[i], vs_l[i]) for i in range(nprod)]
    oi_ref[...] = jnp.concatenate([_dot(a16[h], v[h]) for h in hs], axis=1)

    rest = []
    for h in hs:
        pieces = [None] * (c // HG_BASE)
        for lvi in range(nbig):
            m = HG_BASE << lvi
            per = m // HG_BASE
            o_c = o_l[lvi * heads + h]
            for a in range(c // (2 * m)):
                for u in range(per):
                    dst = (a * 2 * m + m) // HG_BASE + u
                    src = o_c[(a * per + u) * HG_BASE:(a * per + u + 1) * HG_BASE]
                    pieces[dst] = src if pieces[dst] is None else pieces[dst] + src
        zero_slab = jnp.zeros((HG_BASE, dh), F32)
        rest.append(o_inter[h] + jnp.concatenate([zero_slab if p is None else p for p in pieces], axis=0))
    rest_ref[...] = jnp.concatenate(rest, axis=1)

    for h in hs:
        @pl.when(jnp.logical_not(base_ok[h]))
        def _(h=h):
            rowi = lax.broadcasted_iota(I32, (c, dh), 0)
            sub = lax.broadcasted_iota(I32, (SUBLANES, dh), 0)
            ntile = c // SUBLANES
            forget = lanes(forget_all, h)
            attn = _dot_nt(q[h].astype(BF16), k[h].astype(BF16)) * mask_ref[0]
            for lv in range(base_lv):
                m = 1 << lv
                isq = (rowi & m) != 0
                if m == 1:
                    e = jnp.where(isq, forget, 1.0)
                else:
                    if m >= SUBLANES:
                        tiles = [b_row(h, (j * SUBLANES // (2 * m)) * 2 * m + m - 1, SUBLANES) for j in range(ntile)]
                    elif m == 4:
                        tiles = [b_row(h, j * SUBLANES + 3, SUBLANES) for j in range(ntile)]
                    else:
                        tiles = [jnp.where(sub < 4, b_row(h, j * SUBLANES + 1, SUBLANES),
                                           b_row(h, j * SUBLANES + 5, SUBLANES)) for j in range(ntile)]
                    mid = jnp.concatenate(tiles, axis=0)
                    e = jnp.exp2(jnp.where(isq, b[h] - mid, mid - b[h]))
                qt = jnp.where(isq, q[h] * e, 0.0).astype(BF16)
                kt = jnp.where(isq, 0.0, k[h] * e).astype(BF16)
                attn = attn + _dot_nt(qt, kt) * mask_ref[lv + 1]
            oi_ref[:, h * dh:(h + 1) * dh] = _dot(attn.astype(BF16), v[h])

    o = oi_ref[...] + rest_ref[...]
    o = jnp.concatenate(
        [lanes(o, h) * lax.rsqrt(jnp.mean(lanes(o, h) * lanes(o, h), axis=-1, keepdims=True) + RMS_EPS) for h in hs],
        axis=1)
    o = o * gain_ref[...] * _silu(g_ref[...].astype(F32))
    o_ref[...] = o.astype(o_ref.dtype)


def _hgrn_mixer(proj, lb_logits, head_gain, layer, j, bsz, seq):
    t, d4 = proj.shape
    d = d4 // 4
    nh = d // HG_HEAD_DIM
    heads = next(n for n in (16, 8, 4, 2, 1) if nh % n == 0)
    w = heads * HG_HEAD_DIM
    nhp = nh // heads
    chunk = 256 if seq % 256 == 0 else 128
    nc = seq // chunk
    nlev = int(math.log2(chunk))
    base_lv = int(math.log2(HG_BASE))

    def col(part):
        return lambda b, h, c: (b * nc + c, part * nhp + h)

    return pl.pallas_call(
        functools.partial(_hgrn_kernel, layer=layer, chunk=chunk, heads=heads),
        grid=(bsz, nhp, nc),
        in_specs=[
            pl.BlockSpec((chunk, w), col(0)),
            pl.BlockSpec((chunk, w), col(1)),
            pl.BlockSpec((chunk, w), col(2)),
            pl.BlockSpec((chunk, w), col(3)),
            pl.BlockSpec((lb_logits.shape[0], w), lambda b, h, c: (0, h)),
            pl.BlockSpec((None, 1, w), lambda b, h, c: (j, 0, h)),
        ],
        out_specs=pl.BlockSpec((chunk, w), lambda b, h, c: (b * nc + c, h)),
        out_shape=jax.ShapeDtypeStruct((t, d), BF16),
        scratch_shapes=[
            pltpu.VMEM((heads, HG_HEAD_DIM, HG_HEAD_DIM), F32),
            pltpu.VMEM((heads, chunk, HG_HEAD_DIM), F32),
            pltpu.VMEM((chunk, w), F32),
            pltpu.VMEM((chunk, w), F32),
            pltpu.VMEM((chunk, chunk), BF16),
            pltpu.VMEM((base_lv + 1, chunk, chunk), F32),
            pltpu.VMEM((chunk, chunk), F32),
            pltpu.VMEM((nlev - 1 - base_lv, chunk // 2, chunk // 2), F32),
        ],
        compiler_params=_cparams(3),
        name="hgrn2_mixer",
    )(proj, proj, proj, proj, lb_logits.astype(F32), head_gain.astype(F32).reshape(head_gain.shape[0], 1, d))


def _t5_bucket(dist):
    max_exact = N_BUCKETS // 2
    n = jnp.maximum(dist, 0)
    large = max_exact + (jnp.log(jnp.maximum(n, 1).astype(F32) / max_exact)
                         / math.log(MAX_DISTANCE / max_exact)
                         * (N_BUCKETS - max_exact)).astype(I32)
    large = jnp.minimum(large, N_BUCKETS - 1)
    return jnp.where(n < max_exact, n, large)


def _attn_kernel(rb_ref, sink_ref, q_ref, kp_ref, kc_ref, vp_ref, vc_ref, bucket_ref, o_ref,
                 bias_ref, *, group, pairs, layer_j):
    blk = ATT_BLOCK
    hd = ATT_HEAD_DIM
    pr = pl.program_id(0)
    i = pl.program_id(2)
    heads_per_step = 2 * group * pairs
    upper = lax.broadcasted_iota(I32, (blk, blk), 1) > lax.broadcasted_iota(I32, (blk, blk), 0)

    @pl.when((pl.program_id(1) == 0) & (i == 0))
    def _():
        bucket = bucket_ref[...]
        bucket_m = jnp.where(upper, bucket[:, 0:blk], bucket[:, blk:2 * blk])

        def per_head(hh, carry):
            h = pr * heads_per_step + hh
            tbl = jnp.zeros((blk, blk), F32)
            for bk in range(N_BUCKETS):
                tbl = jnp.where(bucket_m == bk, rb_ref[bk, h], tbl)
            bias_ref[hh] = tbl
            bias_ref[heads_per_step + hh] = jnp.where(upper, NEG_INF, tbl)
            return carry

        lax.fori_loop(0, heads_per_step, per_head, 0)

    lane = lax.broadcasted_iota(I32, (2 * blk, LANES), 1)
    table0 = jnp.where(i == 0, heads_per_step, 0)

    q_all = q_ref[...] * (hd ** -0.5)

    kbds, vbds = [], []
    for pp in range(pairs):
        kk = jnp.concatenate([kp_ref[:, pp * LANES:(pp + 1) * LANES], kc_ref[:, pp * LANES:(pp + 1) * LANES]],
                             axis=0).astype(F32)
        vv = jnp.concatenate([vp_ref[:, pp * LANES:(pp + 1) * LANES], vc_ref[:, pp * LANES:(pp + 1) * LANES]],
                             axis=0).astype(F32)
        for c in range(2):
            if c == 0:
                klo = jnp.where(lane < hd, kk, 0.0)
                khi = pltpu.roll(klo, hd, axis=1)
                vlo = jnp.where(lane < hd, vv, 0.0)
                vhi = pltpu.roll(vlo, hd, axis=1)
            else:
                khi = jnp.where(lane >= hd, kk, 0.0)
                klo = pltpu.roll(khi, hd, axis=1)
                vhi = jnp.where(lane >= hd, vv, 0.0)
                vlo = pltpu.roll(vhi, hd, axis=1)
            kbds.append(jnp.concatenate([klo, khi], axis=0).astype(BF16))
            vbds.append(jnp.concatenate([vlo, vhi], axis=0).astype(BF16))

    half = group // 2
    tiles = [(c, p) for c in range(2 * pairs) for p in range(half)]
    lgs = [_dot_nt(q_all[:, (c * half + p) * LANES:(c * half + p + 1) * LANES], kbds[c]) for c, p in tiles]
    p2s, rinvs = [], []
    for (c, p), lg in zip(tiles, lgs):
        probs, rinv = [], []
        for hh in range(2):
            hl = c * group + 2 * p + hh
            sk = sink_ref[layer_j, pr * heads_per_step + hl]
            c0 = hh * 2 * blk
            l = jnp.where(upper, lg[:, c0:c0 + blk], lg[:, c0 + blk:c0 + 2 * blk]) + bias_ref[table0 + hl]
            mx = jnp.maximum(jnp.max(l, axis=-1, keepdims=True), sk)
            pe = jnp.exp(l - mx)
            rinv.append(1.0 / (jnp.sum(pe, axis=-1, keepdims=True) + jnp.exp(sk - mx)))
            pe = pe.astype(BF16)
            zero = jnp.zeros_like(pe)
            probs += [jnp.where(upper, pe, zero), jnp.where(upper, zero, pe)]
        p2s.append(jnp.concatenate(probs, axis=1))
        rinvs.append(rinv)
    lane_o = lax.broadcasted_iota(I32, (blk, LANES), 1)
    for (c, p), p2, rinv in zip(tiles, p2s, rinvs):
        tile = c * half + p
        o = _dot(p2, vbds[c]) * jnp.where(lane_o < hd, rinv[0], rinv[1])
        o_ref[:, tile * LANES:(tile + 1) * LANES] = o.astype(o_ref.dtype)


def _attn_mixer(proj, sinks, rel_bias, layer_j, bsz, seq, d):
    t, att_in = proj.shape
    kvw = (att_in - d) // 2
    n_heads = d // ATT_HEAD_DIM
    kvh = kvw // ATT_HEAD_DIM
    group = n_heads // kvh
    assert kvh % 2 == 0 and group % 2 == 0
    assert WINDOW == ATT_BLOCK
    blk = ATT_BLOCK
    nb = seq // blk
    npair = kvh // 2
    pairs = 1
    ngrp = npair // pairs
    qw = 2 * group * ATT_HEAD_DIM * pairs
    kw = LANES * pairs
    k0 = d // kw
    v0 = (d + kvw) // kw
    assert d % kw == 0 and (d + kvw) % kw == 0

    qi = jnp.arange(blk)[:, None]
    sj = jnp.arange(2 * blk)[None, :]
    bucket = _t5_bucket(qi + blk - sj).astype(I32)

    def prev(i):
        return jnp.maximum(i - 1, 0)

    grid_spec = pltpu.PrefetchScalarGridSpec(
        num_scalar_prefetch=2,
        grid=(ngrp, bsz, nb),
        in_specs=[
            pl.BlockSpec((blk, qw), lambda p, b, i, *_: (b * nb + i, p)),
            pl.BlockSpec((blk, kw), lambda p, b, i, *_: (b * nb + prev(i), k0 + p)),
            pl.BlockSpec((blk, kw), lambda p, b, i, *_: (b * nb + i, k0 + p)),
            pl.BlockSpec((blk, kw), lambda p, b, i, *_: (b * nb + prev(i), v0 + p)),
            pl.BlockSpec((blk, kw), lambda p, b, i, *_: (b * nb + i, v0 + p)),
            pl.BlockSpec((blk, 2 * blk), lambda p, b, i, *_: (0, 0)),
        ],
        out_specs=pl.BlockSpec((blk, qw), lambda p, b, i, *_: (b * nb + i, p)),
        scratch_shapes=[pltpu.VMEM((4 * group * pairs, blk, blk), F32)],
    )
    return pl.pallas_call(
        functools.partial(_attn_kernel, group=group, pairs=pairs, layer_j=layer_j),
        grid_spec=grid_spec,
        out_shape=jax.ShapeDtypeStruct((t, d), BF16),
        compiler_params=_cparams(3),
        name="swa_sink_mixer",
    )(rel_bias.astype(F32), sinks.astype(F32), proj, proj, proj, proj, proj, bucket)


def _layer_norm_rows(z, g, b):
    mu = jnp.mean(z, axis=-1, keepdims=True)
    zc = z - mu
    var = jnp.mean(zc * zc, axis=-1, keepdims=True)
    return zc * lax.rsqrt(var + LN_EPS) * g + b


def _ln_router_kernel(x_ref, y_ref, gate_ref, sc_ref, sh_ref, lng_ref, lnb_ref, wr_ref, br_ref,
                      xo_ref, hf_ref, meta_ref, cnt_ref, carry_ref, wsplit_ref, *, alpha, n_groups, n_experts):
    i = pl.program_id(0)
    bm = x_ref.shape[0]
    epg = n_experts // n_groups

    @pl.when(i == 0)
    def _():
        carry_ref[...] = jnp.zeros_like(carry_ref)
        wh, wl = _split_bf16(wr_ref[...], 2)
        wsplit_ref[:, 0:LANES] = wh
        wsplit_ref[:, LANES:2 * LANES] = wl

    z = alpha * x_ref[...] + (1.0 + gate_ref[...]) * y_ref[...]
    xn = _layer_norm_rows(z, lng_ref[...], lnb_ref[...])
    xo_ref[...] = xn
    hf = xn * (1.0 + sc_ref[...]) + sh_ref[...]
    xh = hf.astype(BF16)
    xh32 = xh.astype(F32)
    hf_ref[...] = _pack_bf16_pairs(xh32, rounded=True)

    xl = (hf - xh32).astype(BF16)
    hh = _dot(xh, wsplit_ref[...])
    lg = hh[:, 0:LANES] + hh[:, LANES:2 * LANES] + _dot(xl, wsplit_ref[:, 0:LANES]) + br_ref[...]

    lane = lax.broadcasted_iota(I32, (bm, LANES), 1)
    lanef = lane.astype(F32)
    big = float(LANES)

    gl = jnp.where(lane < n_groups, lg, NEG_INF)
    gmax = jnp.max(gl, axis=-1, keepdims=True)
    gsel = jnp.min(jnp.where(gl == gmax, lanef, big), axis=-1, keepdims=True)
    p_group = 1.0 / jnp.sum(jnp.exp(gl - gmax), axis=-1, keepdims=True)

    lo = n_groups + gsel * epg
    el = jnp.where((lanef >= lo) & (lanef < lo + epg), lg, NEG_INF)
    m1 = jnp.max(el, axis=-1, keepdims=True)
    i1 = jnp.min(jnp.where(el == m1, lanef, big), axis=-1, keepdims=True)
    el2 = jnp.where(lanef == i1, NEG_INF, el)
    m2 = jnp.max(el2, axis=-1, keepdims=True)
    i2 = jnp.min(jnp.where(el2 == m2, lanef, big), axis=-1, keepdims=True)
    e21 = jnp.exp(m2 - m1)
    g0 = p_group / (1.0 + e21)
    g1 = g0 * e21

    oh0 = lanef == i1
    oh1 = lanef == i2
    cnt = jnp.where(oh0 | oh1, 1.0, 0.0)
    row = lax.broadcasted_iota(I32, (bm, bm), 0)
    col = lax.broadcasted_iota(I32, (bm, bm), 1)
    stril = jnp.where(row > col, 1.0, 0.0).astype(BF16)
    before = _dot(stril, cnt.astype(BF16)) + carry_ref[...]
    r0 = jnp.sum(jnp.where(oh0, before, 0.0), axis=-1, keepdims=True)
    r1 = jnp.sum(jnp.where(oh1, before, 0.0), axis=-1, keepdims=True)
    carry_ref[...] = carry_ref[...] + jnp.sum(cnt, axis=0, keepdims=True)
    cnt_ref[...] = carry_ref[...]

    meta = jnp.where(lane == 0, i1 - n_groups, 0.0)
    meta = jnp.where(lane == 1, i2 - n_groups, meta)
    meta = jnp.where(lane == 2, g0, meta)
    meta = jnp.where(lane == 3, g1, meta)
    meta = jnp.where(lane == 4, r0, meta)
    meta = jnp.where(lane == 5, r1, meta)
    meta_ref[...] = meta


def _ln_router(x2, y2, mod, row_of, ln_g, ln_b, w_router, b_router, alpha, n_groups, n_experts, seq):
    t, d = x2.shape
    bm = 256
    nbs = seq // bm

    def mrow(which):
        return lambda i: (row_of(i // nbs, which), 0, 0)

    return pl.pallas_call(
        functools.partial(_ln_router_kernel, alpha=alpha, n_groups=n_groups, n_experts=n_experts),
        grid=(t // bm,),
        in_specs=[
            pl.BlockSpec((bm, d), lambda i: (i, 0)),
            pl.BlockSpec((bm, d), lambda i: (i, 0)),
            pl.BlockSpec((None, 1, d), mrow(2)),
            pl.BlockSpec((None, 1, d), mrow(4)),
            pl.BlockSpec((None, 1, d), mrow(3)),
            pl.BlockSpec((1, d), lambda i: (0, 0)),
            pl.BlockSpec((1, d), lambda i: (0, 0)),
            pl.BlockSpec((d, LANES), lambda i: (0, 0)),
            pl.BlockSpec((1, LANES), lambda i: (0, 0)),
        ],
        out_specs=[
            pl.BlockSpec((bm, d), lambda i: (i, 0)),
            pl.BlockSpec((bm, d // 2), lambda i: (i, 0)),
            pl.BlockSpec((bm, LANES), lambda i: (i, 0)),
            pl.BlockSpec((1, LANES), lambda i: (0, 0)),
        ],
        out_shape=[
            jax.ShapeDtypeStruct((t, d), F32),
            jax.ShapeDtypeStruct((t, d // 2), jnp.uint32),
            jax.ShapeDtypeStruct((t, LANES), F32),
            jax.ShapeDtypeStruct((1, LANES), F32),
        ],
        scratch_shapes=[pltpu.VMEM((1, LANES), F32), pltpu.VMEM((d, 2 * LANES), BF16)],
        compiler_params=_cparams(1),
        name="ln_router",
    )(x2, y2, mod, mod, mod, ln_g, ln_b, w_router, b_router)


MOE_GATHER_SLOTS = 3


def _cast_rows(src_ref, dst_ref, rows=128):
    def body(r, carry):
        sl = pl.ds(pl.multiple_of(r * rows, rows), rows)
        dst_ref[sl, :] = src_ref[sl, :].astype(dst_ref.dtype)
        return carry
    lax.fori_loop(0, src_ref.shape[0] // rows, body, 0)


def _moe_kernel(blk_e_ref, nxt_e_ref, first_ref, grp_ref, nused_ref,
                tok0_ref, tok1_ref, tok2_ref, hf_hbm, wgu_hbm, wdn_hbm, y_ref,
                xbuf, wgu_st, wdn_st, wgu_bf, wdn_bf, gsem, wsem, *, layer, d_expert):
    i = pl.program_id(0)
    bm = xbuf.shape[1]
    nused = nused_ref[0]
    slot = i % MOE_GATHER_SLOTS

    def row_copy(tok, r, s):
        return pltpu.make_async_copy(hf_hbm.at[pl.ds(tok, 1), :], xbuf.at[s, pl.ds(r, 1), :], gsem.at[s])

    def issue_rows(tok_ref, s):
        for r in range(bm):
            row_copy(tok_ref[0, r], r, s).start()

    def wait_rows(s):
        pltpu.make_async_copy(hf_hbm.at[pl.ds(0, bm), :], xbuf.at[s], gsem.at[s]).wait()

    def wgu_copy(e):
        return pltpu.make_async_copy(wgu_hbm.at[layer, e], wgu_st, wsem.at[0])

    def wdn_copy(e, s):
        return pltpu.make_async_copy(wdn_hbm.at[layer, e], wdn_st.at[s], wsem.at[1 + s])

    @pl.when(i == 0)
    def _():
        issue_rows(tok0_ref, 0)
        wgu_copy(blk_e_ref[0]).start(priority=1)
        wdn_copy(blk_e_ref[0], 0).start(priority=1)

    @pl.when((i == 0) & (nused > 1))
    def _():
        issue_rows(tok1_ref, 1)

    @pl.when((i < nused) & (first_ref[i] == 1))
    def _():
        par = grp_ref[i] % 2
        has_next = nxt_e_ref[i] >= 0

        @pl.when(has_next)
        def _():
            wdn_copy(nxt_e_ref[i], 1 - par).start(priority=1)

        wgu_copy(blk_e_ref[i]).wait()
        _cast_rows(wgu_st, wgu_bf)

        @pl.when(has_next)
        def _():
            wgu_copy(nxt_e_ref[i]).start(priority=1)

        wdn_copy(blk_e_ref[i], par).wait()
        _cast_rows(wdn_st.at[par], wdn_bf)

    @pl.when(i + 2 < nused)
    def _():
        issue_rows(tok2_ref, (i + 2) % MOE_GATHER_SLOTS)

    @pl.when(i < nused)
    def _():
        wait_rows(slot)
        xb = _unpack_bf16_pairs(xbuf[slot]).astype(BF16)
        a = _dot(xb, wgu_bf[...])
        h = _silu(a[:, :d_expert]) * a[:, d_expert:]
        y_ref[...] = _pack_bf16_pairs(_dot(h.astype(BF16), wdn_bf[...]))

    @pl.when(i >= nused)
    def _():
        y_ref[...] = jnp.zeros_like(y_ref)


def _moe_experts(hf, tok_pad, blk_e, nxt_e, first, grp, nused, w_gate_up, w_down, layer, bm):
    d = w_down.shape[3]
    nb = blk_e.shape[0]
    d_expert = w_down.shape[2]
    tok3 = tok_pad.reshape(nb, 1, bm)
    grid_spec = pltpu.PrefetchScalarGridSpec(
        num_scalar_prefetch=5,
        grid=(nb,),
        in_specs=[
            pl.BlockSpec((None, 1, bm), lambda i, *_: (i, 0, 0), memory_space=pltpu.SMEM),
            pl.BlockSpec((None, 1, bm), lambda i, *_: (jnp.minimum(i + 1, nb - 1), 0, 0), memory_space=pltpu.SMEM),
            pl.BlockSpec((None, 1, bm), lambda i, *_: (jnp.minimum(i + 2, nb - 1), 0, 0), memory_space=pltpu.SMEM),
            pl.BlockSpec(memory_space=pl.ANY),
            pl.BlockSpec(memory_space=pl.ANY),
            pl.BlockSpec(memory_space=pl.ANY),
        ],
        out_specs=pl.BlockSpec((bm, d // 2), lambda i, *_: (i, 0)),
        scratch_shapes=[
            pltpu.VMEM((MOE_GATHER_SLOTS, bm, d // 2), jnp.uint32),
            pltpu.VMEM((d, 2 * d_expert), F32),
            pltpu.VMEM((2, d_expert, d), F32),
            pltpu.VMEM((d, 2 * d_expert), BF16),
            pltpu.VMEM((d_expert, d), BF16),
            pltpu.SemaphoreType.DMA((MOE_GATHER_SLOTS,)),
            pltpu.SemaphoreType.DMA((3,)),
        ],
    )
    return pl.pallas_call(
        functools.partial(_moe_kernel, layer=layer, d_expert=d_expert),
        grid_spec=grid_spec,
        out_shape=jax.ShapeDtypeStruct((nb * bm, d // 2), jnp.uint32),
        compiler_params=_cparams(1),
        name="moe_experts",
    )(blk_e, nxt_e, first, grp, nused, tok3, tok3, tok3, hf, w_gate_up, w_down)


def _moe_plan(meta, counts_row, n_groups, n_experts, bm):
    t = meta.shape[0]
    eid = meta[:, 0:2].astype(I32)
    rank = meta[:, 4:6].astype(I32)
    counts = counts_row[0, n_groups:n_groups + n_experts].astype(I32)
    padded = (counts + bm - 1) // bm * bm
    pad_end = jnp.cumsum(padded)
    pad_start = pad_end - padded
    start_of = jnp.sum(jnp.where(eid[..., None] == jnp.arange(n_experts, dtype=I32), pad_start, 0), axis=-1)
    dest = start_of + rank
    nb = (2 * t) // bm + n_experts
    nused = pad_end[-1] // bm
    ids = jnp.arange(nb, dtype=I32)
    raw_e = jnp.minimum(jnp.sum((pad_end[None, :] <= (ids * bm)[:, None]).astype(I32), axis=1), n_experts - 1)
    used = ids < nused
    blk_e = jnp.where(used, raw_e, raw_e[nused - 1])
    prev_e = jnp.concatenate([jnp.full((1,), -1, I32), blk_e[:-1]])
    first = (used & (blk_e != prev_e)).astype(I32)
    grp = jnp.cumsum(first) - 1
    key = jnp.where(used, blk_e, n_experts)
    nxt_idx = jnp.sum((key[None, :] <= blk_e[:, None]).astype(I32), axis=1)
    nxt_e = jnp.where(nxt_idx < nused, key[jnp.minimum(nxt_idx, nb - 1)], -1).astype(I32)
    tok = jnp.repeat(jnp.arange(t, dtype=I32), 2)
    tok_pad = jnp.zeros((nb * bm,), I32).at[dest.reshape(-1)].set(tok)
    return dest, tok_pad, blk_e, nxt_e, first, grp.astype(I32), nused.reshape(1).astype(I32)


def _ln_combine_kernel(*refs, alpha, with_next):
    if with_next:
        (dc_ref, dn_ref, x_ref, meta_ref, gate_ref, lng_ref, lnb_ref, sc_ref, sh_ref, y_hbm,
         xo_ref, hm_ref, ybuf, sem) = refs
    else:
        (dc_ref, dn_ref, x_ref, meta_ref, gate_ref, lng_ref, lnb_ref, y_hbm,
         xo_ref, ybuf, sem) = refs
    i = pl.program_id(0)
    n = pl.num_programs(0)
    bm = x_ref.shape[0]
    slot = i % 2

    def row_copy(src, r, s):
        return pltpu.make_async_copy(y_hbm.at[pl.ds(src, 1), :], ybuf.at[s, pl.ds(r, 1), :], sem.at[s])

    def issue_rows(d_ref, s):
        for r in range(2 * bm):
            row_copy(d_ref[0, r], r, s).start()

    def wait_rows(s):
        pltpu.make_async_copy(y_hbm.at[pl.ds(0, 2 * bm), :], ybuf.at[s], sem.at[s]).wait()

    @pl.when(i == 0)
    def _():
        issue_rows(dc_ref, 0)

    @pl.when(i + 1 < n)
    def _():
        issue_rows(dn_ref, 1 - slot)

    wait_rows(slot)
    meta = meta_ref[...]
    y = (_unpack_bf16_pairs(ybuf[slot, 0:bm, :]) * meta[:, 2:3]
         + _unpack_bf16_pairs(ybuf[slot, bm:2 * bm, :]) * meta[:, 3:4])
    z = alpha * x_ref[...] + (1.0 + gate_ref[...]) * y
    xn = _layer_norm_rows(z, lng_ref[...], lnb_ref[...])
    xo_ref[...] = xn
    if with_next:
        hm_ref[...] = (xn * (1.0 + sc_ref[...]) + sh_ref[...]).astype(hm_ref.dtype)


def _ln_combine(x2, meta, dest, ysort, mod, row_of, next_row_of, ln_g, ln_b, alpha, seq):
    t, d = x2.shape
    bm = 256
    nbs = seq // bm
    nblk = t // bm
    with_next = next_row_of is not None
    dest3 = dest.reshape(nblk, bm, 2).transpose(0, 2, 1).reshape(nblk, 1, 2 * bm)

    def mrow(fn, which):
        return lambda i: (fn(i // nbs, which), 0, 0)

    in_specs = [
        pl.BlockSpec((None, 1, 2 * bm), lambda i: (i, 0, 0), memory_space=pltpu.SMEM),
        pl.BlockSpec((None, 1, 2 * bm), lambda i: (jnp.minimum(i + 1, nblk - 1), 0, 0), memory_space=pltpu.SMEM),
        pl.BlockSpec((bm, d), lambda i: (i, 0)),
        pl.BlockSpec((bm, LANES), lambda i: (i, 0)),
        pl.BlockSpec((None, 1, d), mrow(row_of, 5)),
        pl.BlockSpec((1, d), lambda i: (0, 0)),
        pl.BlockSpec((1, d), lambda i: (0, 0)),
    ]
    args = [dest3, dest3, x2, meta, mod, ln_g, ln_b]
    out_specs = [pl.BlockSpec((bm, d), lambda i: (i, 0))]
    out_shape = [jax.ShapeDtypeStruct((t, d), F32)]
    if with_next:
        in_specs += [pl.BlockSpec((None, 1, d), mrow(next_row_of, 1)),
                     pl.BlockSpec((None, 1, d), mrow(next_row_of, 0))]
        args += [mod, mod]
        out_specs.append(pl.BlockSpec((bm, d), lambda i: (i, 0)))
        out_shape.append(jax.ShapeDtypeStruct((t, d), BF16))
    in_specs.append(pl.BlockSpec(memory_space=pl.ANY))
    args.append(ysort)
    return pl.pallas_call(
        functools.partial(_ln_combine_kernel, alpha=alpha, with_next=with_next),
        grid=(nblk,),
        in_specs=in_specs,
        out_specs=out_specs,
        out_shape=out_shape,
        scratch_shapes=[pltpu.VMEM((2, 2 * bm, d // 2), jnp.uint32), pltpu.SemaphoreType.DMA((2,))],
        compiler_params=_cparams(1),
        name="ln_moe_combine",
    )(*args)


def kernel(x, c, w_ada, b_ada, ln_g, ln_b, w_in_a, lb_logits, head_gain_a, w_out_a, w_in_b, attn_sinks, w_out_b, rel_bias, w_router_group, b_router_group, w_router_expert, b_router_expert, w_gate_up, w_down):
    bsz, seq, d = x.shape
    depth = w_ada.shape[0]
    n_groups = w_router_group.shape[2]
    n_experts = w_router_expert.shape[2]
    alpha = (2 * depth) ** 0.25
    t = bsz * seq
    moe_bm = 128

    mod = _ada_modulation(c, w_ada, b_ada)

    def row_of_layer(layer):
        return lambda b, which: (layer * bsz + b) * 6 + which

    x2 = x.reshape(t, d).astype(F32)
    row0 = row_of_layer(0)
    hm = _modulate(x2, mod, lambda b: row0(b, 1), lambda b: row0(b, 0), bsz, seq)

    for layer in range(depth):
        row_of = row_of_layer(layer)
        j = layer // 2
        if layer % 2 == 0:
            proj = _matmul(hm, w_in_a, j, BF16)
            o = _hgrn_mixer(proj, lb_logits, head_gain_a, layer, j, bsz, seq)
            y = _matmul(o, w_out_a, j, F32)
        else:
            proj = _matmul(hm, w_in_b, j, BF16)
            o = _attn_mixer(proj, attn_sinks, rel_bias, j, bsz, seq, d)
            y = _matmul(o, w_out_b, j, F32)

        n_pad = LANES - n_groups - n_experts
        w_router = jnp.concatenate(
            [w_router_group[layer].astype(F32), w_router_expert[layer].astype(F32), jnp.zeros((d, n_pad), F32)], axis=1)
        b_router = jnp.concatenate(
            [b_router_group[layer].astype(F32), b_router_expert[layer].astype(F32), jnp.zeros((n_pad,), F32)]
        ).reshape(1, LANES)

        x2, hf, meta, counts = _ln_router(
            x2, y, mod, row_of, ln_g[layer, 0:1].astype(F32), ln_b[layer, 0:1].astype(F32),
            w_router, b_router, alpha, n_groups, n_experts, seq)
        dest, tok_pad, blk_e, nxt_e, first, grp, nused = _moe_plan(meta, counts, n_groups, n_experts, moe_bm)
        ysort = _moe_experts(hf, tok_pad, blk_e, nxt_e, first, grp, nused, w_gate_up, w_down, layer, moe_bm)
        next_row_of = row_of_layer(layer + 1) if layer + 1 < depth else None
        outs = _ln_combine(x2, meta, dest, ysort, mod, row_of, next_row_of,
                           ln_g[layer, 1:2].astype(F32), ln_b[layer, 1:2].astype(F32), alpha, seq)
        x2 = outs[0]
        if next_row_of is not None:
            hm = outs[1]

    return x2.reshape(bsz, seq, d).astype(x.dtype)
```

```python
import functools
import math

import jax
import jax.numpy as jnp
from jax import lax
from jax.experimental import pallas as pl
from jax.experimental.pallas import tpu as pltpu

F32 = jnp.float32
BF16 = jnp.bfloat16
I32 = jnp.int32

LANES = 128
SUBLANES = 8
V7X_VMEM_LIMIT_BYTES = 56 * 1024 * 1024

HG_HEAD_DIM = 128
ATT_HEAD_DIM = 64
ATT_BLOCK = 128
WINDOW = 128
N_BUCKETS = 32
MAX_DISTANCE = 128
LN_EPS = 1e-5
RMS_EPS = 1e-6
NEG_INF = float("-inf")


def _cparams(n_axes):
    return pltpu.CompilerParams(
        dimension_semantics=("arbitrary",) * n_axes,
        vmem_limit_bytes=V7X_VMEM_LIMIT_BYTES,
    )


def _sigmoid(x):
    return 0.5 * jnp.tanh(0.5 * x) + 0.5


def _silu(x):
    return x * _sigmoid(x)


def _dot_nt(a, b):
    return lax.dot_general(a, b, (((1,), (1,)), ((), ())), preferred_element_type=F32)


def _dot_tn(a, b):
    return lax.dot_general(a, b, (((0,), (0,)), ((), ())), preferred_element_type=F32)


def _dot(a, b):
    return jnp.dot(a, b, preferred_element_type=F32)


def _pack_bf16_pairs(x, rounded=False):
    n = x.shape[1] // 2
    bits = pltpu.bitcast(x if rounded else x.astype(BF16).astype(F32), jnp.uint32)
    return (bits[:, :n] >> 16) | (bits[:, n:] & jnp.uint32(0xFFFF0000))


def _unpack_bf16_pairs(w):
    lo = pltpu.bitcast(w << 16, F32)
    hi = pltpu.bitcast(w & jnp.uint32(0xFFFF0000), F32)
    return jnp.concatenate([lo, hi], axis=1)


def _split_bf16(x, parts):
    out = []
    r = x
    for _ in range(parts):
        h = r.astype(BF16)
        out.append(h)
        r = r - h.astype(F32)
    return out


def _ada_kernel(c_ref, w_ref, b_ref, o_ref):
    ca = _silu(c_ref[...]).astype(BF16)
    o_ref[...] = _dot(ca, w_ref[...].astype(BF16)) + b_ref[...]


def _ada_modulation(c, w_ada, b_ada):
    nl, d, n6 = w_ada.shape
    bsz = c.shape[0]
    rows = -(-bsz // SUBLANES) * SUBLANES
    c8 = jnp.zeros((rows, d), F32).at[:bsz].set(c.astype(F32))
    tn = 512
    out = pl.pallas_call(
        _ada_kernel,
        grid=(nl, n6 // tn),
        in_specs=[
            pl.BlockSpec((rows, d), lambda l, j: (0, 0)),
            pl.BlockSpec((None, d, tn), lambda l, j: (l, 0, j)),
            pl.BlockSpec((None, 1, tn), lambda l, j: (l, 0, j)),
        ],
        out_specs=pl.BlockSpec((None, rows, tn), lambda l, j: (l, 0, j)),
        out_shape=jax.ShapeDtypeStruct((nl, rows, n6), F32),
        compiler_params=_cparams(2),
        name="ada_modulation",
    )(c8, w_ada, b_ada.reshape(nl, 1, n6))
    return out[:, :bsz].reshape(nl * bsz * 6, 1, d)


def _modulate_kernel(x_ref, sc_ref, sh_ref, o_ref):
    o_ref[...] = (x_ref[...] * (1.0 + sc_ref[...]) + sh_ref[...]).astype(o_ref.dtype)


def _modulate(x2, mod, sc_row, sh_row, bsz, seq):
    t, d = x2.shape
    bs = min(512, seq)
    nbs = seq // bs
    return pl.pallas_call(
        _modulate_kernel,
        grid=(t // bs,),
        in_specs=[
            pl.BlockSpec((bs, d), lambda i: (i, 0)),
            pl.BlockSpec((None, 1, d), lambda i: (sc_row(i // nbs), 0, 0)),
            pl.BlockSpec((None, 1, d), lambda i: (sh_row(i // nbs), 0, 0)),
        ],
        out_specs=pl.BlockSpec((bs, d), lambda i: (i, 0)),
        out_shape=jax.ShapeDtypeStruct((t, d), BF16),
        compiler_params=_cparams(1),
        name="modulate",
    )(x2, mod, mod)


def _matmul_kernel(x_ref, w_hbm, o_ref, wbf_ref, stage_ref, sem, *, layer, bn, kc):
    j = pl.program_id(0)
    i = pl.program_id(1)
    nj = pl.num_programs(0)
    cur = j % 2

    def slab_copy(jb, c):
        return pltpu.make_async_copy(
            w_hbm.at[layer, pl.ds(c * kc, kc), pl.ds(jb * bn, bn)], stage_ref, sem.at[0])

    def cast_slab(c, buf):
        wbf_ref[buf, pl.ds(c * kc, kc), :] = stage_ref[...].astype(BF16)

    @pl.when((j == 0) & (i == 0))
    def _():
        def first_block(c, carry):
            cp = slab_copy(0, c)
            cp.start()
            cp.wait()
            cast_slab(pl.multiple_of(c, 1), 0)
            return carry
        lax.fori_loop(0, pl.num_programs(1), first_block, 0)

    @pl.when(j + 1 < nj)
    def _():
        slab_copy(j + 1, i).start()

    o_ref[...] = _dot(x_ref[...], wbf_ref[cur]).astype(o_ref.dtype)

    @pl.when(j + 1 < nj)
    def _():
        slab_copy(j + 1, i).wait()
        cast_slab(i, 1 - cur)


def _matmul(x, w3, layer, out_dtype):
    m, k = x.shape
    n = w3.shape[2]
    bm = min(1024, m)
    bn = 1024 if n % 1024 == 0 else 512
    steps = m // bm
    kc = k // steps
    assert k % steps == 0 and kc % SUBLANES == 0
    return pl.pallas_call(
        functools.partial(_matmul_kernel, layer=layer, bn=bn, kc=kc),
        grid=(n // bn, steps),
        in_specs=[
            pl.BlockSpec((bm, k), lambda j, i: (i, 0)),
            pl.BlockSpec(memory_space=pl.ANY),
        ],
        out_specs=pl.BlockSpec((bm, bn), lambda j, i: (i, j)),
        out_shape=jax.ShapeDtypeStruct((m, n), out_dtype),
        scratch_shapes=[
            pltpu.VMEM((2, k, bn), BF16),
            pltpu.VMEM((kc, bn), F32),
            pltpu.SemaphoreType.DMA((1,)),
        ],
        compiler_params=_cparams(2),
        name="dense_projection",
    )(x, w3)


HG_BASE = 16
HG_BASE_MAX_DECAY = 86.0


def _hgrn_kernel(q_ref, f_ref, v_ref, g_ref, lbl_ref, gain_ref, o_ref,
                 st_ref, b_ref, oi_ref, rest_ref, tri_ref, mask_ref, bmask_ref, cmask_ref, *, layer, chunk, heads):
    c = chunk
    hc = c // 2
    dh = HG_HEAD_DIM
    nlev = int(math.log2(c))
    base_lv = int(math.log2(HG_BASE))
    nbig = nlev - base_lv
    hs = range(heads)
    first = (pl.program_id(0) == 0) & (pl.program_id(1) == 0) & (pl.program_id(2) == 0)

    @pl.when(first)
    def _():
        row = lax.broadcasted_iota(I32, (c, c), 0)
        col = lax.broadcasted_iota(I32, (c, c), 1)
        tri_ref[...] = jnp.where(row >= col, 1.0, 0.0).astype(BF16)
        x = row ^ col
        mask_ref[0] = jnp.where(x == 0, 1.0, 0.0).astype(F32)
        for lv in range(1, base_lv + 1):
            mask_ref[lv] = jnp.where(x < (1 << lv), 1.0, 0.0).astype(F32)
        bmask_ref[...] = jnp.where((x < HG_BASE) & (row >= col), 1.0, 0.0).astype(F32)
        xh = lax.broadcasted_iota(I32, (hc, hc), 0) ^ lax.broadcasted_iota(I32, (hc, hc), 1)
        for lv in range(base_lv, nlev - 1):
            cmask_ref[lv - base_lv] = jnp.where(xh < (1 << lv), 1.0, 0.0).astype(F32)

    @pl.when(pl.program_id(2) == 0)
    def _():
        st_ref[...] = jnp.zeros_like(st_ref)

    lbl = lbl_ref[...]
    rows = [lbl[i:i + 1, :] for i in range(lbl.shape[0])]
    mx = functools.reduce(jnp.maximum, rows)
    es = [jnp.exp(r - mx) for r in rows]
    lb = functools.reduce(lambda a, b: a + b, es[:layer + 1]) / functools.reduce(lambda a, b: a + b, es)

    q_all = _silu(q_ref[...].astype(F32))
    forget_all = lb + (1.0 - lb) * _sigmoid(f_ref[...].astype(F32))
    k_all = 1.0 - forget_all
    v_all = v_ref[...]

    w = heads * dh
    bb = _dot(tri_ref[...], jnp.concatenate(_split_bf16(jnp.log2(forget_all), 3), axis=1))
    b_all = bb[:, 0:w] + bb[:, w:2 * w] + bb[:, 2 * w:3 * w]

    def lanes(x, h):
        return x[:, h * dh:(h + 1) * dh]

    for h in hs:
        b_ref[h] = lanes(b_all, h)

    q = [lanes(q_all, h) for h in hs]
    k = [lanes(k_all, h) for h in hs]
    v = [lanes(v_all, h) for h in hs]
    b = [lanes(b_all, h) for h in hs]

    def b_row(h, r, n):
        return jnp.broadcast_to(b_ref[h, pl.ds(r, 1), :], (n, dh))

    blk_i = lax.broadcasted_iota(I32, (c // HG_BASE, dh), 0)
    base_ok = []
    for h in hs:
        ends = b_ref[h, pl.ds(HG_BASE - 1, c // HG_BASE, stride=HG_BASE), :]
        drop = jnp.where(blk_i == 0, 0.0, pltpu.roll(ends, 1, axis=0)) - ends
        base_ok.append(jnp.max(drop) <= HG_BASE_MAX_DECAY)

    qs_l, ks_l, vs_l = [], [], []
    for lv in range(base_lv, nlev):
        m = 1 << lv
        n = 2 * m
        for h in hs:
            qs, ks, vs = [], [], []
            for a in range(c // n):
                mid = b_row(h, a * n + m - 1, m)
                qs.append(q[h][a * n + m:(a + 1) * n] * jnp.exp2(b[h][a * n + m:(a + 1) * n] - mid))
                ks.append(k[h][a * n:a * n + m] * jnp.exp2(mid - b[h][a * n:a * n + m]))
                vs.append(v[h][a * n:a * n + m])
            qs_l.append(jnp.concatenate(qs, axis=0).astype(BF16))
            ks_l.append(jnp.concatenate(ks, axis=0).astype(BF16))
            vs_l.append(jnp.concatenate(vs, axis=0))
    qb16, kb16, qe, kd, b_last, st = [], [], [], [], [], []
    for h in hs:
        start = jnp.concatenate(
            [jnp.zeros((HG_BASE, dh), F32)] + [b_row(h, j * HG_BASE - 1, HG_BASE) for j in range(1, c // HG_BASE)],
            axis=0)
        dlt = jnp.maximum(b[h] - start, -HG_BASE_MAX_DECAY)
        qb16.append((q[h] * jnp.exp2(dlt)).astype(BF16))
        kb16.append((k[h] * jnp.exp2(-dlt)).astype(BF16))
        b_last.append(b_row(h, c - 1, c))
        qe.append((q[h] * jnp.exp2(b[h])).astype(BF16))
        kd.append((k[h] * jnp.exp2(b_last[h] - b[h])).astype(BF16))
        st.append(st_ref[h])

    nprod = nbig * heads
    a_l = [_dot_nt(qs_l[i], ks_l[i]) for i in range(nprod)]
    a16 = [_dot_nt(qb16[h], kb16[h]) for h in hs]
    o_inter = [_dot_nt(qe[h], st[h].astype(BF16)) for h in hs]
    for h in hs:
        st_ref[h] = st[h] * jnp.exp2(b_last[h][0:1, :]) + _dot_tn(v[h], kd[h])

    a_l = [(a_l[i] * cmask_ref[i // heads] if i // heads < nbig - 1 else a_l[i]).astype(BF16) for i in range(nprod)]
    a16 = [(a16[h] * bmask_ref[...]).astype(BF16) for h in hs]
    o_l = [_dot(a_l[i], vs_l[i]) for i in range(nprod)]
    oi_ref[...] = jnp.concatenate([_dot(a16[h], v[h]) for h in hs], axis=1)

    rest = []
    for h in hs:
        pieces = [None] * (c // HG_BASE)
        for lvi in range(nbig):
            m = HG_BASE << lvi
            per = m // HG_BASE
            o_c = o_l[lvi * heads + h]
            for a in range(c // (2 * m)):
                for u in range(per):
                    dst = (a * 2 * m + m) // HG_BASE + u
                    src = o_c[(a * per + u) * HG_BASE:(a * per + u + 1) * HG_BASE]
                    pieces[dst] = src if pieces[dst] is None else pieces[dst] + src
        zero_slab = jnp.zeros((HG_BASE, dh), F32)
        rest.append(o_inter[h] + jnp.concatenate([zero_slab if p is None else p for p in pieces], axis=0))
    rest_ref[...] = jnp.concatenate(rest, axis=1)

    for h in hs:
        @pl.when(jnp.logical_not(base_ok[h]))
        def _(h=h):
            rowi = lax.broadcasted_iota(I32, (c, dh), 0)
            sub = lax.broadcasted_iota(I32, (SUBLANES, dh), 0)
            ntile = c // SUBLANES
            forget = lanes(forget_all, h)
            attn = _dot_nt(q[h].astype(BF16), k[h].astype(BF16)) * mask_ref[0]
            for lv in range(base_lv):
                m = 1 << lv
                isq = (rowi & m) != 0
                if m == 1:
                    e = jnp.where(isq, forget, 1.0)
                else:
                    if m >= SUBLANES:
                        tiles = [b_row(h, (j * SUBLANES // (2 * m)) * 2 * m + m - 1, SUBLANES) for j in range(ntile)]
                    elif m == 4:
                        tiles = [b_row(h, j * SUBLANES + 3, SUBLANES) for j in range(ntile)]
                    else:
                        tiles = [jnp.where(sub < 4, b_row(h, j * SUBLANES + 1, SUBLANES),
                                           b_row(h, j * SUBLANES + 5, SUBLANES)) for j in range(ntile)]
                    mid = jnp.concatenate(tiles, axis=0)
                    e = jnp.exp2(jnp.where(isq, b[h] - mid, mid - b[h]))
                qt = jnp.where(isq, q[h] * e, 0.0).astype(BF16)
                kt = jnp.where(isq, 0.0, k[h] * e).astype(BF16)
                attn = attn + _dot_nt(qt, kt) * mask_ref[lv + 1]
            oi_ref[:, h * dh:(h + 1) * dh] = _dot(attn.astype(BF16), v[h])

    o = oi_ref[...] + rest_ref[...]
    o = jnp.concatenate(
        [lanes(o, h) * lax.rsqrt(jnp.mean(lanes(o, h) * lanes(o, h), axis=-1, keepdims=True) + RMS_EPS) for h in hs],
        axis=1)
    o = o * gain_ref[...] * _silu(g_ref[...].astype(F32))
    o_ref[...] = o.astype(o_ref.dtype)


def _hgrn_mixer(proj, lb_logits, head_gain, layer, j, bsz, seq):
    t, d4 = proj.shape
    d = d4 // 4
    nh = d // HG_HEAD_DIM
    heads = next(n for n in (16, 8, 4, 2, 1) if nh % n == 0)
    w = heads * HG_HEAD_DIM
    nhp = nh // heads
    chunk = 256 if seq % 256 == 0 else 128
    nc = seq // chunk
    nlev = int(math.log2(chunk))
    base_lv = int(math.log2(HG_BASE))

    def col(part):
        return lambda b, h, c: (b * nc + c, part * nhp + h)

    return pl.pallas_call(
        functools.partial(_hgrn_kernel, layer=layer, chunk=chunk, heads=heads),
        grid=(bsz, nhp, nc),
        in_specs=[
            pl.BlockSpec((chunk, w), col(0)),
            pl.BlockSpec((chunk, w), col(1)),
            pl.BlockSpec((chunk, w), col(2)),
            pl.BlockSpec((chunk, w), col(3)),
            pl.BlockSpec((lb_logits.shape[0], w), lambda b, h, c: (0, h)),
            pl.BlockSpec((None, 1, w), lambda b, h, c: (j, 0, h)),
        ],
        out_specs=pl.BlockSpec((chunk, w), lambda b, h, c: (b * nc + c, h)),
        out_shape=jax.ShapeDtypeStruct((t, d), BF16),
        scratch_shapes=[
            pltpu.VMEM((heads, HG_HEAD_DIM, HG_HEAD_DIM), F32),
            pltpu.VMEM((heads, chunk, HG_HEAD_DIM), F32),
            pltpu.VMEM((chunk, w), F32),
            pltpu.VMEM((chunk, w), F32),
            pltpu.VMEM((chunk, chunk), BF16),
            pltpu.VMEM((base_lv + 1, chunk, chunk), F32),
            pltpu.VMEM((chunk, chunk), F32),
            pltpu.VMEM((nlev - 1 - base_lv, chunk // 2, chunk // 2), F32),
        ],
        compiler_params=_cparams(3),
        name="hgrn2_mixer",
    )(proj, proj, proj, proj, lb_logits.astype(F32), head_gain.astype(F32).reshape(head_gain.shape[0], 1, d))


def _t5_bucket(dist):
    max_exact = N_BUCKETS // 2
    n = jnp.maximum(dist, 0)
    large = max_exact + (jnp.log(jnp.maximum(n, 1).astype(F32) / max_exact)
                         / math.log(MAX_DISTANCE / max_exact)
                         * (N_BUCKETS - max_exact)).astype(I32)
    large = jnp.minimum(large, N_BUCKETS - 1)
    return jnp.where(n < max_exact, n, large)


def _attn_kernel(rb_ref, sink_ref, q_ref, kp_ref, kc_ref, vp_ref, vc_ref, bucket_ref, o_ref,
                 bias_ref, *, group, pairs, layer_j):
    blk = ATT_BLOCK
    hd = ATT_HEAD_DIM
    pr = pl.program_id(0)
    i = pl.program_id(2)
    heads_per_step = 2 * group * pairs
    upper = lax.broadcasted_iota(I32, (blk, blk), 1) > lax.broadcasted_iota(I32, (blk, blk), 0)

    @pl.when((pl.program_id(1) == 0) & (i == 0))
    def _():
        bucket = bucket_ref[...]
        bucket_m = jnp.where(upper, bucket[:, 0:blk], bucket[:, blk:2 * blk])

        def per_head(hh, carry):
            h = pr * heads_per_step + hh
            tbl = jnp.zeros((blk, blk), F32)
            for bk in range(N_BUCKETS):
                tbl = jnp.where(bucket_m == bk, rb_ref[bk, h], tbl)
            bias_ref[hh] = tbl
            bias_ref[heads_per_step + hh] = jnp.where(upper, NEG_INF, tbl)
            return carry

        lax.fori_loop(0, heads_per_step, per_head, 0)

    lane = lax.broadcasted_iota(I32, (2 * blk, LANES), 1)
    table0 = jnp.where(i == 0, heads_per_step, 0)

    q_all = q_ref[...] * (hd ** -0.5)

    kbds, vbds = [], []
    for pp in range(pairs):
        kk = jnp.concatenate([kp_ref[:, pp * LANES:(pp + 1) * LANES], kc_ref[:, pp * LANES:(pp + 1) * LANES]],
                             axis=0).astype(F32)
        vv = jnp.concatenate([vp_ref[:, pp * LANES:(pp + 1) * LANES], vc_ref[:, pp * LANES:(pp + 1) * LANES]],
                             axis=0).astype(F32)
        for c in range(2):
            if c == 0:
                klo = jnp.where(lane < hd, kk, 0.0)
                khi = pltpu.roll(klo, hd, axis=1)
                vlo = jnp.where(lane < hd, vv, 0.0)
                vhi = pltpu.roll(vlo, hd, axis=1)
            else:
                khi = jnp.where(lane >= hd, kk, 0.0)
                klo = pltpu.roll(khi, hd, axis=1)
                vhi = jnp.where(lane >= hd, vv, 0.0)
                vlo = pltpu.roll(vhi, hd, axis=1)
            kbds.append(jnp.concatenate([klo, khi], axis=0).astype(BF16))
            vbds.append(jnp.concatenate([vlo, vhi], axis=0).astype(BF16))

    half = group // 2
    tiles = [(c, p) for c in range(2 * pairs) for p in range(half)]
    lgs = [_dot_nt(q_all[:, (c * half + p) * LANES:(c * half + p + 1) * LANES], kbds[c]) for c, p in tiles]
    p2s, rinvs = [], []
    for (c, p), lg in zip(tiles, lgs):
        probs, rinv = [], []
        for hh in range(2):
            hl = c * group + 2 * p + hh
            sk = sink_ref[layer_j, pr * heads_per_step + hl]
            c0 = hh * 2 * blk
            l = jnp.where(upper, lg[:, c0:c0 + blk], lg[:, c0 + blk:c0 + 2 * blk]) + bias_ref[table0 + hl]
            mx = jnp.maximum(jnp.max(l, axis=-1, keepdims=True), sk)
            pe = jnp.exp(l - mx)
            rinv.append(1.0 / (jnp.sum(pe, axis=-1, keepdims=True) + jnp.exp(sk - mx)))
            pe = pe.astype(BF16)
            zero = jnp.zeros_like(pe)
            probs += [jnp.where(upper, pe, zero), jnp.where(upper, zero, pe)]
        p2s.append(jnp.concatenate(probs, axis=1))
        rinvs.append(rinv)
    lane_o = lax.broadcasted_iota(I32, (blk, LANES), 1)
    for (c, p), p2, rinv in zip(tiles, p2s, rinvs):
        tile = c * half + p
        o = _dot(p2, vbds[c]) * jnp.where(lane_o < hd, rinv[0], rinv[1])
        o_ref[:, tile * LANES:(tile + 1) * LANES] = o.astype(o_ref.dtype)


def _attn_mixer(proj, sinks, rel_bias, layer_j, bsz, seq, d):
    t, att_in = proj.shape
    kvw = (att_in - d) // 2
    n_heads = d // ATT_HEAD_DIM
    kvh = kvw // ATT_HEAD_DIM
    group = n_heads // kvh
    assert kvh % 2 == 0 and group % 2 == 0
    assert WINDOW == ATT_BLOCK
    blk = ATT_BLOCK
    nb = seq // blk
    npair = kvh // 2
    pairs = 1
    ngrp = npair // pairs
    qw = 2 * group * ATT_HEAD_DIM * pairs
    kw = LANES * pairs
    k0 = d // kw
    v0 = (d + kvw) // kw
    assert d % kw == 0 and (d + kvw) % kw == 0

    qi = jnp.arange(blk)[:, None]
    sj = jnp.arange(2 * blk)[None, :]
    bucket = _t5_bucket(qi + blk - sj).astype(I32)

    def prev(i):
        return jnp.maximum(i - 1, 0)

    grid_spec = pltpu.PrefetchScalarGridSpec(
        num_scalar_prefetch=2,
        grid=(ngrp, bsz, nb),
        in_specs=[
            pl.BlockSpec((blk, qw), lambda p, b, i, *_: (b * nb + i, p)),
            pl.BlockSpec((blk, kw), lambda p, b, i, *_: (b * nb + prev(i), k0 + p)),
            pl.BlockSpec((blk, kw), lambda p, b, i, *_: (b * nb + i, k0 + p)),
            pl.BlockSpec((blk, kw), lambda p, b, i, *_: (b * nb + prev(i), v0 + p)),
            pl.BlockSpec((blk, kw), lambda p, b, i, *_: (b * nb + i, v0 + p)),
            pl.BlockSpec((blk, 2 * blk), lambda p, b, i, *_: (0, 0)),
        ],
        out_specs=pl.BlockSpec((blk, qw), lambda p, b, i, *_: (b * nb + i, p)),
        scratch_shapes=[pltpu.VMEM((4 * group * pairs, blk, blk), F32)],
    )
    return pl.pallas_call(
        functools.partial(_attn_kernel, group=group, pairs=pairs, layer_j=layer_j),
        grid_spec=grid_spec,
        out_shape=jax.ShapeDtypeStruct((t, d), BF16),
        compiler_params=_cparams(3),
        name="swa_sink_mixer",
    )(rel_bias.astype(F32), sinks.astype(F32), proj, proj, proj, proj, proj, bucket)


def _layer_norm_rows(z, g, b):
    mu = jnp.mean(z, axis=-1, keepdims=True)
    zc = z - mu
    var = jnp.mean(zc * zc, axis=-1, keepdims=True)
    return zc * lax.rsqrt(var + LN_EPS) * g + b


def _ln_router_kernel(x_ref, y_ref, gate_ref, sc_ref, sh_ref, lng_ref, lnb_ref, wr_ref, br_ref,
                      xo_ref, hf_ref, meta_ref, cnt_ref, carry_ref, wsplit_ref, *, alpha, n_groups, n_experts):
    i = pl.program_id(0)
    bm = x_ref.shape[0]
    epg = n_experts // n_groups

    @pl.when(i == 0)
    def _():
        carry_ref[...] = jnp.zeros_like(carry_ref)
        wh, wl = _split_bf16(wr_ref[...], 2)
        wsplit_ref[:, 0:LANES] = wh
        wsplit_ref[:, LANES:2 * LANES] = wl

    z = alpha * x_ref[...] + (1.0 + gate_ref[...]) * y_ref[...]
    xn = _layer_norm_rows(z, lng_ref[...], lnb_ref[...])
    xo_ref[...] = xn
    hf = xn * (1.0 + sc_ref[...]) + sh_ref[...]
    xh = hf.astype(BF16)
    xh32 = xh.astype(F32)
    hf_ref[...] = _pack_bf16_pairs(xh32, rounded=True)

    xl = (hf - xh32).astype(BF16)
    hh = _dot(xh, wsplit_ref[...])
    lg = hh[:, 0:LANES] + hh[:, LANES:2 * LANES] + _dot(xl, wsplit_ref[:, 0:LANES]) + br_ref[...]

    lane = lax.broadcasted_iota(I32, (bm, LANES), 1)
    lanef = lane.astype(F32)
    big = float(LANES)

    gl = jnp.where(lane < n_groups, lg, NEG_INF)
    gmax = jnp.max(gl, axis=-1, keepdims=True)
    gsel = jnp.min(jnp.where(gl == gmax, lanef, big), axis=-1, keepdims=True)
    p_group = 1.0 / jnp.sum(jnp.exp(gl - gmax), axis=-1, keepdims=True)

    lo = n_groups + gsel * epg
    el = jnp.where((lanef >= lo) & (lanef < lo + epg), lg, NEG_INF)
    m1 = jnp.max(el, axis=-1, keepdims=True)
    i1 = jnp.min(jnp.where(el == m1, lanef, big), axis=-1, keepdims=True)
    el2 = jnp.where(lanef == i1, NEG_INF, el)
    m2 = jnp.max(el2, axis=-1, keepdims=True)
    i2 = jnp.min(jnp.where(el2 == m2, lanef, big), axis=-1, keepdims=True)
    e21 = jnp.exp(m2 - m1)
    g0 = p_group / (1.0 + e21)
    g1 = g0 * e21

    oh0 = lanef == i1
    oh1 = lanef == i2
    cnt = jnp.where(oh0 | oh1, 1.0, 0.0)
    row = lax.broadcasted_iota(I32, (bm, bm), 0)
    col = lax.broadcasted_iota(I32, (bm, bm), 1)
    stril = jnp.where(row > col, 1.0, 0.0).astype(BF16)
    before = _dot(stril, cnt.astype(BF16)) + carry_ref[...]
    r0 = jnp.sum(jnp.where(oh0, before, 0.0), axis=-1, keepdims=True)
    r1 = jnp.sum(jnp.where(oh1, before, 0.0), axis=-1, keepdims=True)
    carry_ref[...] = carry_ref[...] + jnp.sum(cnt, axis=0, keepdims=True)
    cnt_ref[...] = carry_ref[...]

    meta = jnp.where(lane == 0, i1 - n_groups, 0.0)
    meta = jnp.where(lane == 1, i2 - n_groups, meta)
    meta = jnp.where(lane == 2, g0, meta)
    meta = jnp.where(lane == 3, g1, meta)
    meta = jnp.where(lane == 4, r0, meta)
    meta = jnp.where(lane == 5, r1, meta)
    meta_ref[...] = meta


def _ln_router(x2, y2, mod, row_of, ln_g, ln_b, w_router, b_router, alpha, n_groups, n_experts, seq):
    t, d = x2.shape
    bm = 256
    nbs = seq // bm

    def mrow(which):
        return lambda i: (row_of(i // nbs, which), 0, 0)

    return pl.pallas_call(
        functools.partial(_ln_router_kernel, alpha=alpha, n_groups=n_groups, n_experts=n_experts),
        grid=(t // bm,),
        in_specs=[
            pl.BlockSpec((bm, d), lambda i: (i, 0)),
            pl.BlockSpec((bm, d), lambda i: (i, 0)),
            pl.BlockSpec((None, 1, d), mrow(2)),
            pl.BlockSpec((None, 1, d), mrow(4)),
            pl.BlockSpec((None, 1, d), mrow(3)),
            pl.BlockSpec((1, d), lambda i: (0, 0)),
            pl.BlockSpec((1, d), lambda i: (0, 0)),
            pl.BlockSpec((d, LANES), lambda i: (0, 0)),
            pl.BlockSpec((1, LANES), lambda i: (0, 0)),
        ],
        out_specs=[
            pl.BlockSpec((bm, d), lambda i: (i, 0)),
            pl.BlockSpec((bm, d // 2), lambda i: (i, 0)),
            pl.BlockSpec((bm, LANES), lambda i: (i, 0)),
            pl.BlockSpec((1, LANES), lambda i: (0, 0)),
        ],
        out_shape=[
            jax.ShapeDtypeStruct((t, d), F32),
            jax.ShapeDtypeStruct((t, d // 2), jnp.uint32),
            jax.ShapeDtypeStruct((t, LANES), F32),
            jax.ShapeDtypeStruct((1, LANES), F32),
        ],
        scratch_shapes=[pltpu.VMEM((1, LANES), F32), pltpu.VMEM((d, 2 * LANES), BF16)],
        compiler_params=_cparams(1),
        name="ln_router",
    )(x2, y2, mod, mod, mod, ln_g, ln_b, w_router, b_router)


MOE_GATHER_SLOTS = 3


def _cast_rows(src_ref, dst_ref, rows=128):
    def body(r, carry):
        sl = pl.ds(pl.multiple_of(r * rows, rows), rows)
        dst_ref[sl, :] = src_ref[sl, :].astype(dst_ref.dtype)
        return carry
    lax.fori_loop(0, src_ref.shape[0] // rows, body, 0)


def _moe_kernel(blk_e_ref, nxt_e_ref, first_ref, grp_ref, nused_ref,
                tok0_ref, tok1_ref, tok2_ref, hf_hbm, wgu_hbm, wdn_hbm, y_ref,
                xbuf, wgu_st, wdn_st, wgu_bf, wdn_bf, gsem, wsem, *, layer, d_expert):
    i = pl.program_id(0)
    bm = xbuf.shape[1]
    nused = nused_ref[0]
    slot = i % MOE_GATHER_SLOTS

    def row_copy(tok, r, s):
        return pltpu.make_async_copy(hf_hbm.at[pl.ds(tok, 1), :], xbuf.at[s, pl.ds(r, 1), :], gsem.at[s])

    def issue_rows(tok_ref, s):
        for r in range(bm):
            row_copy(tok_ref[0, r], r, s).start()

    def wait_rows(s):
        pltpu.make_async_copy(hf_hbm.at[pl.ds(0, bm), :], xbuf.at[s], gsem.at[s]).wait()

    def wgu_copy(e):
        return pltpu.make_async_copy(wgu_hbm.at[layer, e], wgu_st, wsem.at[0])

    def wdn_copy(e, s):
        return pltpu.make_async_copy(wdn_hbm.at[layer, e], wdn_st.at[s], wsem.at[1 + s])

    @pl.when(i == 0)
    def _():
        issue_rows(tok0_ref, 0)
        wgu_copy(blk_e_ref[0]).start(priority=1)
        wdn_copy(blk_e_ref[0], 0).start(priority=1)

    @pl.when((i == 0) & (nused > 1))
    def _():
        issue_rows(tok1_ref, 1)

    @pl.when((i < nused) & (first_ref[i] == 1))
    def _():
        par = grp_ref[i] % 2
        has_next = nxt_e_ref[i] >= 0

        @pl.when(has_next)
        def _():
            wdn_copy(nxt_e_ref[i], 1 - par).start()

        wgu_copy(blk_e_ref[i]).wait()
        _cast_rows(wgu_st, wgu_bf)

        @pl.when(has_next)
        def _():
            wgu_copy(nxt_e_ref[i]).start(priority=1)

        wdn_copy(blk_e_ref[i], par).wait()
        _cast_rows(wdn_st.at[par], wdn_bf)

    @pl.when(i + 2 < nused)
    def _():
        issue_rows(tok2_ref, (i + 2) % MOE_GATHER_SLOTS)

    @pl.when(i < nused)
    def _():
        wait_rows(slot)
        xb = _unpack_bf16_pairs(xbuf[slot]).astype(BF16)
        a = _dot(xb, wgu_bf[...])
        h = _silu(a[:, :d_expert]) * a[:, d_expert:]
        y_ref[...] = _pack_bf16_pairs(_dot(h.astype(BF16), wdn_bf[...]))

    @pl.when(i >= nused)
    def _():
        y_ref[...] = jnp.zeros_like(y_ref)


def _moe_experts(hf, tok_pad, blk_e, nxt_e, first, grp, nused, w_gate_up, w_down, layer, bm):
    d = w_down.shape[3]
    nb = blk_e.shape[0]
    d_expert = w_down.shape[2]
    tok3 = tok_pad.reshape(nb, 1, bm)
    grid_spec = pltpu.PrefetchScalarGridSpec(
        num_scalar_prefetch=5,
        grid=(nb,),
        in_specs=[
            pl.BlockSpec((None, 1, bm), lambda i, *_: (i, 0, 0), memory_space=pltpu.SMEM),
            pl.BlockSpec((None, 1, bm), lambda i, *_: (jnp.minimum(i + 1, nb - 1), 0, 0), memory_space=pltpu.SMEM),
            pl.BlockSpec((None, 1, bm), lambda i, *_: (jnp.minimum(i + 2, nb - 1), 0, 0), memory_space=pltpu.SMEM),
            pl.BlockSpec(memory_space=pl.ANY),
            pl.BlockSpec(memory_space=pl.ANY),
            pl.BlockSpec(memory_space=pl.ANY),
        ],
        out_specs=pl.BlockSpec((bm, d // 2), lambda i, *_: (i, 0)),
        scratch_shapes=[
            pltpu.VMEM((MOE_GATHER_SLOTS, bm, d // 2), jnp.uint32),
            pltpu.VMEM((d, 2 * d_expert), F32),
            pltpu.VMEM((2, d_expert, d), F32),
            pltpu.VMEM((d, 2 * d_expert), BF16),
            pltpu.VMEM((d_expert, d), BF16),
            pltpu.SemaphoreType.DMA((MOE_GATHER_SLOTS,)),
            pltpu.SemaphoreType.DMA((3,)),
        ],
    )
    return pl.pallas_call(
        functools.partial(_moe_kernel, layer=layer, d_expert=d_expert),
        grid_spec=grid_spec,
        out_shape=jax.ShapeDtypeStruct((nb * bm, d // 2), jnp.uint32),
        compiler_params=_cparams(1),
        name="moe_experts",
    )(blk_e, nxt_e, first, grp, nused, tok3, tok3, tok3, hf, w_gate_up, w_down)


def _moe_plan(meta, counts_row, n_groups, n_experts, bm):
    t = meta.shape[0]
    eid = meta[:, 0:2].astype(I32)
    rank = meta[:, 4:6].astype(I32)
    counts = counts_row[0, n_groups:n_groups + n_experts].astype(I32)
    padded = (counts + bm - 1) // bm * bm
    pad_end = jnp.cumsum(padded)
    pad_start = pad_end - padded
    start_of = jnp.sum(jnp.where(eid[..., None] == jnp.arange(n_experts, dtype=I32), pad_start, 0), axis=-1)
    dest = start_of + rank
    nb = (2 * t) // bm + n_experts
    nused = pad_end[-1] // bm
    ids = jnp.arange(nb, dtype=I32)
    raw_e = jnp.minimum(jnp.sum((pad_end[None, :] <= (ids * bm)[:, None]).astype(I32), axis=1), n_experts - 1)
    used = ids < nused
    blk_e = jnp.where(used, raw_e, raw_e[nused - 1])
    prev_e = jnp.concatenate([jnp.full((1,), -1, I32), blk_e[:-1]])
    first = (used & (blk_e != prev_e)).astype(I32)
    grp = jnp.cumsum(first) - 1
    key = jnp.where(used, blk_e, n_experts)
    nxt_idx = jnp.sum((key[None, :] <= blk_e[:, None]).astype(I32), axis=1)
    nxt_e = jnp.where(nxt_idx < nused, key[jnp.minimum(nxt_idx, nb - 1)], -1).astype(I32)
    tok = jnp.repeat(jnp.arange(t, dtype=I32), 2)
    tok_pad = jnp.zeros((nb * bm,), I32).at[dest.reshape(-1)].set(tok)
    return dest, tok_pad, blk_e, nxt_e, first, grp.astype(I32), nused.reshape(1).astype(I32)


def _ln_combine_kernel(*refs, alpha, with_next):
    if with_next:
        (dc_ref, dn_ref, x_ref, meta_ref, gate_ref, lng_ref, lnb_ref, sc_ref, sh_ref, y_hbm,
         xo_ref, hm_ref, ybuf, sem) = refs
    else:
        (dc_ref, dn_ref, x_ref, meta_ref, gate_ref, lng_ref, lnb_ref, y_hbm,
         xo_ref, ybuf, sem) = refs
    i = pl.program_id(0)
    n = pl.num_programs(0)
    bm = x_ref.shape[0]
    slot = i % 2

    def row_copy(src, r, s):
        return pltpu.make_async_copy(y_hbm.at[pl.ds(src, 1), :], ybuf.at[s, pl.ds(r, 1), :], sem.at[s])

    def issue_rows(d_ref, s):
        for r in range(2 * bm):
            row_copy(d_ref[0, r], r, s).start()

    def wait_rows(s):
        pltpu.make_async_copy(y_hbm.at[pl.ds(0, 2 * bm), :], ybuf.at[s], sem.at[s]).wait()

    @pl.when(i == 0)
    def _():
        issue_rows(dc_ref, 0)

    @pl.when(i + 1 < n)
    def _():
        issue_rows(dn_ref, 1 - slot)

    wait_rows(slot)
    meta = meta_ref[...]
    y = (_unpack_bf16_pairs(ybuf[slot, 0:bm, :]) * meta[:, 2:3]
         + _unpack_bf16_pairs(ybuf[slot, bm:2 * bm, :]) * meta[:, 3:4])
    z = alpha * x_ref[...] + (1.0 + gate_ref[...]) * y
    xn = _layer_norm_rows(z, lng_ref[...], lnb_ref[...])
    xo_ref[...] = xn
    if with_next:
        hm_ref[...] = (xn * (1.0 + sc_ref[...]) + sh_ref[...]).astype(hm_ref.dtype)


def _ln_combine(x2, meta, dest, ysort, mod, row_of, next_row_of, ln_g, ln_b, alpha, seq):
    t, d = x2.shape
    bm = 256
    nbs = seq // bm
    nblk = t // bm
    with_next = next_row_of is not None
    dest3 = dest.reshape(nblk, bm, 2).transpose(0, 2, 1).reshape(nblk, 1, 2 * bm)

    def mrow(fn, which):
        return lambda i: (fn(i // nbs, which), 0, 0)

    in_specs = [
        pl.BlockSpec((None, 1, 2 * bm), lambda i: (i, 0, 0), memory_space=pltpu.SMEM),
        pl.BlockSpec((None, 1, 2 * bm), lambda i: (jnp.minimum(i + 1, nblk - 1), 0, 0), memory_space=pltpu.SMEM),
        pl.BlockSpec((bm, d), lambda i: (i, 0)),
        pl.BlockSpec((bm, LANES), lambda i: (i, 0)),
        pl.BlockSpec((None, 1, d), mrow(row_of, 5)),
        pl.BlockSpec((1, d), lambda i: (0, 0)),
        pl.BlockSpec((1, d), lambda i: (0, 0)),
    ]
    args = [dest3, dest3, x2, meta, mod, ln_g, ln_b]
    out_specs = [pl.BlockSpec((bm, d), lambda i: (i, 0))]
    out_shape = [jax.ShapeDtypeStruct((t, d), F32)]
    if with_next:
        in_specs += [pl.BlockSpec((None, 1, d), mrow(next_row_of, 1)),
                     pl.BlockSpec((None, 1, d), mrow(next_row_of, 0))]
        args += [mod, mod]
        out_specs.append(pl.BlockSpec((bm, d), lambda i: (i, 0)))
        out_shape.append(jax.ShapeDtypeStruct((t, d), BF16))
    in_specs.append(pl.BlockSpec(memory_space=pl.ANY))
    args.append(ysort)
    return pl.pallas_call(
        functools.partial(_ln_combine_kernel, alpha=alpha, with_next=with_next),
        grid=(nblk,),
        in_specs=in_specs,
        out_specs=out_specs,
        out_shape=out_shape,
        scratch_shapes=[pltpu.VMEM((2, 2 * bm, d // 2), jnp.uint32), pltpu.SemaphoreType.DMA((2,))],
        compiler_params=_cparams(1),
        name="ln_moe_combine",
    )(*args)


def kernel(x, c, w_ada, b_ada, ln_g, ln_b, w_in_a, lb_logits, head_gain_a, w_out_a, w_in_b, attn_sinks, w_out_b, rel_bias, w_router_group, b_router_group, w_router_expert, b_router_expert, w_gate_up, w_down):
    bsz, seq, d = x.shape
    depth = w_ada.shape[0]
    n_groups = w_router_group.shape[2]
    n_experts = w_router_expert.shape[2]
    alpha = (2 * depth) ** 0.25
    t = bsz * seq
    moe_bm = 128

    mod = _ada_modulation(c, w_ada, b_ada)

    def row_of_layer(layer):
        return lambda b, which: (layer * bsz + b) * 6 + which

    x2 = x.reshape(t, d).astype(F32)
    row0 = row_of_layer(0)
    hm = _modulate(x2, mod, lambda b: row0(b, 1), lambda b: row0(b, 0), bsz, seq)

    for layer in range(depth):
        row_of = row_of_layer(layer)
        j = layer // 2
        if layer % 2 == 0:
            proj = _matmul(hm, w_in_a, j, BF16)
            o = _hgrn_mixer(proj, lb_logits, head_gain_a, layer, j, bsz, seq)
            y = _matmul(o, w_out_a, j, F32)
        else:
            proj = _matmul(hm, w_in_b, j, BF16)
            o = _attn_mixer(proj, attn_sinks, rel_bias, j, bsz, seq, d)
            y = _matmul(o, w_out_b, j, F32)

        n_pad = LANES - n_groups - n_experts
        w_router = jnp.concatenate(
            [w_router_group[layer].astype(F32), w_router_expert[layer].astype(F32), jnp.zeros((d, n_pad), F32)], axis=1)
        b_router = jnp.concatenate(
            [b_router_group[layer].astype(F32), b_router_expert[layer].astype(F32), jnp.zeros((n_pad,), F32)]
        ).reshape(1, LANES)

        x2, hf, meta, counts = _ln_router(
            x2, y, mod, row_of, ln_g[layer, 0:1].astype(F32), ln_b[layer, 0:1].astype(F32),
            w_router, b_router, alpha, n_groups, n_experts, seq)
        dest, tok_pad, blk_e, nxt_e, first, grp, nused = _moe_plan(meta, counts, n_groups, n_experts, moe_bm)
        ysort = _moe_experts(hf, tok_pad, blk_e, nxt_e, first, grp, nused, w_gate_up, w_down, layer, moe_bm)
        next_row_of = row_of_layer(layer + 1) if layer + 1 < depth else None
        outs = _ln_combine(x2, meta, dest, ysort, mod, row_of, next_row_of,
                           ln_g[layer, 1:2].astype(F32), ln_b[layer, 1:2].astype(F32), alpha, seq)
        x2 = outs[0]
        if next_row_of is not None:
            hm = outs[1]

    return x2.reshape(bsz, seq, d).astype(x.dtype)
```

```python
import functools
import math

import jax
import jax.numpy as jnp
from jax import lax
from jax.experimental import pallas as pl
from jax.experimental.pallas import tpu as pltpu

F32 = jnp.float32
BF16 = jnp.bfloat16
I32 = jnp.int32

LANES = 128
SUBLANES = 8
V7X_VMEM_LIMIT_BYTES = 56 * 1024 * 1024

HG_HEAD_DIM = 128
ATT_HEAD_DIM = 64
ATT_BLOCK = 128
WINDOW = 128
N_BUCKETS = 32
MAX_DISTANCE = 128
LN_EPS = 1e-5
RMS_EPS = 1e-6
NEG_INF = float("-inf")


def _cparams(n_axes):
    return pltpu.CompilerParams(
        dimension_semantics=("arbitrary",) * n_axes,
        vmem_limit_bytes=V7X_VMEM_LIMIT_BYTES,
    )


def _sigmoid(x):
    return 0.5 * jnp.tanh(0.5 * x) + 0.5


def _silu(x):
    return x * _sigmoid(x)


def _dot_nt(a, b):
    return lax.dot_general(a, b, (((1,), (1,)), ((), ())), preferred_element_type=F32)


def _dot_tn(a, b):
    return lax.dot_general(a, b, (((0,), (0,)), ((), ())), preferred_element_type=F32)


def _dot(a, b):
    return jnp.dot(a, b, preferred_element_type=F32)


def _pack_bf16_pairs(x, rounded=False):
    n = x.shape[1] // 2
    bits = pltpu.bitcast(x if rounded else x.astype(BF16).astype(F32), jnp.uint32)
    return (bits[:, :n] >> 16) | (bits[:, n:] & jnp.uint32(0xFFFF0000))


def _unpack_bf16_pairs(w):
    lo = pltpu.bitcast(w << 16, F32)
    hi = pltpu.bitcast(w & jnp.uint32(0xFFFF0000), F32)
    return jnp.concatenate([lo, hi], axis=1)


def _split_bf16(x, parts):
    out = []
    r = x
    for _ in range(parts):
        h = r.astype(BF16)
        out.append(h)
        r = r - h.astype(F32)
    return out


def _ada_kernel(c_ref, w_ref, b_ref, o_ref):
    ca = _silu(c_ref[...]).astype(BF16)
    o_ref[...] = _dot(ca, w_ref[...].astype(BF16)) + b_ref[...]


def _ada_modulation(c, w_ada, b_ada):
    nl, d, n6 = w_ada.shape
    bsz = c.shape[0]
    rows = -(-bsz // SUBLANES) * SUBLANES
    c8 = jnp.zeros((rows, d), F32).at[:bsz].set(c.astype(F32))
    tn = 512
    out = pl.pallas_call(
        _ada_kernel,
        grid=(nl, n6 // tn),
        in_specs=[
            pl.BlockSpec((rows, d), lambda l, j: (0, 0)),
            pl.BlockSpec((None, d, tn), lambda l, j: (l, 0, j)),
            pl.BlockSpec((None, 1, tn), lambda l, j: (l, 0, j)),
        ],
        out_specs=pl.BlockSpec((None, rows, tn), lambda l, j: (l, 0, j)),
        out_shape=jax.ShapeDtypeStruct((nl, rows, n6), F32),
        compiler_params=_cparams(2),
        name="ada_modulation",
    )(c8, w_ada, b_ada.reshape(nl, 1, n6))
    return out[:, :bsz].reshape(nl * bsz * 6, 1, d)


def _modulate_kernel(x_ref, sc_ref, sh_ref, o_ref):
    o_ref[...] = (x_ref[...] * (1.0 + sc_ref[...]) + sh_ref[...]).astype(o_ref.dtype)


def _modulate(x2, mod, sc_row, sh_row, bsz, seq):
    t, d = x2.shape
    bs = min(512, seq)
    nbs = seq // bs
    return pl.pallas_call(
        _modulate_kernel,
        grid=(t // bs,),
        in_specs=[
            pl.BlockSpec((bs, d), lambda i: (i, 0)),
            pl.BlockSpec((None, 1, d), lambda i: (sc_row(i // nbs), 0, 0)),
            pl.BlockSpec((None, 1, d), lambda i: (sh_row(i // nbs), 0, 0)),
        ],
        out_specs=pl.BlockSpec((bs, d), lambda i: (i, 0)),
        out_shape=jax.ShapeDtypeStruct((t, d), BF16),
        compiler_params=_cparams(1),
        name="modulate",
    )(x2, mod, mod)


def _matmul_kernel(x_ref, w_hbm, o_ref, wbf_ref, stage_ref, sem, *, layer, bn, kc):
    j = pl.program_id(0)
    i = pl.program_id(1)
    nj = pl.num_programs(0)
    cur = j % 2

    def slab_copy(jb, c):
        return pltpu.make_async_copy(
            w_hbm.at[layer, pl.ds(c * kc, kc), pl.ds(jb * bn, bn)], stage_ref, sem.at[0])

    def cast_slab(c, buf):
        wbf_ref[buf, pl.ds(c * kc, kc), :] = stage_ref[...].astype(BF16)

    @pl.when((j == 0) & (i == 0))
    def _():
        def first_block(c, carry):
            cp = slab_copy(0, c)
            cp.start()
            cp.wait()
            cast_slab(pl.multiple_of(c, 1), 0)
            return carry
        lax.fori_loop(0, pl.num_programs(1), first_block, 0)

    @pl.when(j + 1 < nj)
    def _():
        slab_copy(j + 1, i).start()

    o_ref[...] = _dot(x_ref[...], wbf_ref[cur]).astype(o_ref.dtype)

    @pl.when(j + 1 < nj)
    def _():
        slab_copy(j + 1, i).wait()
        cast_slab(i, 1 - cur)


def _matmul(x, w3, layer, out_dtype):
    m, k = x.shape
    n = w3.shape[2]
    bm = min(1024, m)
    bn = 1024 if n % 1024 == 0 else 512
    steps = m // bm
    kc = k // steps
    assert k % steps == 0 and kc % SUBLANES == 0
    return pl.pallas_call(
        functools.partial(_matmul_kernel, layer=layer, bn=bn, kc=kc),
        grid=(n // bn, steps),
        in_specs=[
            pl.BlockSpec((bm, k), lambda j, i: (i, 0)),
            pl.BlockSpec(memory_space=pl.ANY),
        ],
        out_specs=pl.BlockSpec((bm, bn), lambda j, i: (i, j)),
        out_shape=jax.ShapeDtypeStruct((m, n), out_dtype),
        scratch_shapes=[
            pltpu.VMEM((2, k, bn), BF16),
            pltpu.VMEM((kc, bn), F32),
            pltpu.SemaphoreType.DMA((1,)),
        ],
        compiler_params=_cparams(2),
        name="dense_projection",
    )(x, w3)


HG_BASE = 16
HG_BASE_MAX_DECAY = 86.0


def _hgrn_kernel(q_ref, f_ref, v_ref, g_ref, lbl_ref, gain_ref, o_ref,
                 st_ref, b_ref, oi_ref, rest_ref, tri_ref, mask_ref, bmask_ref, cmask_ref, *, layer, chunk, heads):
    c = chunk
    hc = c // 2
    dh = HG_HEAD_DIM
    nlev = int(math.log2(c))
    base_lv = int(math.log2(HG_BASE))
    nbig = nlev - base_lv
    hs = range(heads)
    first = (pl.program_id(0) == 0) & (pl.program_id(1) == 0) & (pl.program_id(2) == 0)

    @pl.when(first)
    def _():
        row = lax.broadcasted_iota(I32, (c, c), 0)
        col = lax.broadcasted_iota(I32, (c, c), 1)
        tri_ref[...] = jnp.where(row >= col, 1.0, 0.0).astype(BF16)
        x = row ^ col
        mask_ref[0] = jnp.where(x == 0, 1.0, 0.0).astype(F32)
        for lv in range(1, base_lv + 1):
            mask_ref[lv] = jnp.where(x < (1 << lv), 1.0, 0.0).astype(F32)
        bmask_ref[...] = jnp.where((x < HG_BASE) & (row >= col), 1.0, 0.0).astype(F32)
        xh = lax.broadcasted_iota(I32, (hc, hc), 0) ^ lax.broadcasted_iota(I32, (hc, hc), 1)
        for lv in range(base_lv, nlev - 1):
            cmask_ref[lv - base_lv] = jnp.where(xh < (1 << lv), 1.0, 0.0).astype(F32)

    @pl.when(pl.program_id(2) == 0)
    def _():
        st_ref[...] = jnp.zeros_like(st_ref)

    lbl = lbl_ref[...]
    rows = [lbl[i:i + 1, :] for i in range(lbl.shape[0])]
    mx = functools.reduce(jnp.maximum, rows)
    es = [jnp.exp(r - mx) for r in rows]
    lb = functools.reduce(lambda a, b: a + b, es[:layer + 1]) / functools.reduce(lambda a, b: a + b, es)

    q_all = _silu(q_ref[...].astype(F32))
    forget_all = lb + (1.0 - lb) * _sigmoid(f_ref[...].astype(F32))
    k_all = 1.0 - forget_all
    v_all = v_ref[...]

    w = heads * dh
    bb = _dot(tri_ref[...], jnp.concatenate(_split_bf16(jnp.log2(forget_all), 3), axis=1))
    b_all = bb[:, 0:w] + bb[:, w:2 * w] + bb[:, 2 * w:3 * w]

    def lanes(x, h):
        return x[:, h * dh:(h + 1) * dh]

    for h in hs:
        b_ref[h] = lanes(b_all, h)

    q = [lanes(q_all, h) for h in hs]
    k = [lanes(k_all, h) for h in hs]
    v = [lanes(v_all, h) for h in hs]
    b = [lanes(b_all, h) for h in hs]

    def b_row(h, r, n):
        return jnp.broadcast_to(b_ref[h, pl.ds(r, 1), :], (n, dh))

    blk_i = lax.broadcasted_iota(I32, (c // HG_BASE, dh), 0)
    base_ok = []
    for h in hs:
        ends = b_ref[h, pl.ds(HG_BASE - 1, c // HG_BASE, stride=HG_BASE), :]
        drop = jnp.where(blk_i == 0, 0.0, pltpu.roll(ends, 1, axis=0)) - ends
        base_ok.append(jnp.max(drop) <= HG_BASE_MAX_DECAY)

    qs_l, ks_l, vs_l = [], [], []
    for lv in range(base_lv, nlev):
        m = 1 << lv
        n = 2 * m
        for h in hs:
            qs, ks, vs = [], [], []
            for a in range(c // n):
                mid = b_row(h, a * n + m - 1, m)
                qs.append(q[h][a * n + m:(a + 1) * n] * jnp.exp2(b[h][a * n + m:(a + 1) * n] - mid))
                ks.append(k[h][a * n:a * n + m] * jnp.exp2(mid - b[h][a * n:a * n + m]))
                vs.append(v[h][a * n:a * n + m])
            qs_l.append(jnp.concatenate(qs, axis=0).astype(BF16))
            ks_l.append(jnp.concatenate(ks, axis=0).astype(BF16))
            vs_l.append(jnp.concatenate(vs, axis=0))
    qb16, kb16, qe, kd, b_last, st = [], [], [], [], [], []
    for h in hs:
        start = jnp.concatenate(
            [jnp.zeros((HG_BASE, dh), F32)] + [b_row(h, j * HG_BASE - 1, HG_BASE) for j in range(1, c // HG_BASE)],
            axis=0)
        dlt = jnp.maximum(b[h] - start, -HG_BASE_MAX_DECAY)
        qb16.append((q[h] * jnp.exp2(dlt)).astype(BF16))
        kb16.append((k[h] * jnp.exp2(-dlt)).astype(BF16))
        b_last.append(b_row(h, c - 1, c))
        qe.append((q[h] * jnp.exp2(b[h])).astype(BF16))
        kd.append((k[h] * jnp.exp2(b_last[h] - b[h])).astype(BF16))
        st.append(st_ref[h])

    nprod = nbig * heads
    a_l = [_dot_nt(qs_l[i], ks_l[i]) for i in range(nprod)]
    a16 = [_dot_nt(qb16[h], kb16[h]) for h in hs]
    o_inter = [_dot_nt(qe[h], st[h].astype(BF16)) for h in hs]
    for h in hs:
        st_ref[h] = st[h] * jnp.exp2(b_last[h][0:1, :]) + _dot_tn(v[h], kd[h])

    a_l = [(a_l[i] * cmask_ref[i // heads] if i // heads < nbig - 1 else a_l[i]).astype(BF16) for i in range(nprod)]
    a16 = [(a16[h] * bmask_ref[...]).astype(BF16) for h in hs]
    o_l = [_dot(a_l[i], vs_l[i]) for i in range(nprod)]
    oi_ref[...] = jnp.concatenate([_dot(a16[h], v[h]) for h in hs], axis=1)

    rest = []
    for h in hs:
        pieces = [None] * (c // HG_BASE)
        for lvi in range(nbig):
            m = HG_BASE << lvi
            per = m // HG_BASE
            o_c = o_l[lvi * heads + h]
            for a in range(c // (2 * m)):
                for u in range(per):
                    dst = (a * 2 * m + m) // HG_BASE + u
                    src = o_c[(a * per + u) * HG_BASE:(a * per + u + 1) * HG_BASE]
                    pieces[dst] = src if pieces[dst] is None else pieces[dst] + src
        zero_slab = jnp.zeros((HG_BASE, dh), F32)
        rest.append(o_inter[h] + jnp.concatenate([zero_slab if p is None else p for p in pieces], axis=0))
    rest_ref[...] = jnp.concatenate(rest, axis=1)

    for h in hs:
        @pl.when(jnp.logical_not(base_ok[h]))
        def _(h=h):
            rowi = lax.broadcasted_iota(I32, (c, dh), 0)
            sub = lax.broadcasted_iota(I32, (SUBLANES, dh), 0)
            ntile = c // SUBLANES
            forget = lanes(forget_all, h)
            attn = _dot_nt(q[h].astype(BF16), k[h].astype(BF16)) * mask_ref[0]
            for lv in range(base_lv):
                m = 1 << lv
                isq = (rowi & m) != 0
                if m == 1:
                    e = jnp.where(isq, forget, 1.0)
                else:
                    if m >= SUBLANES:
                        tiles = [b_row(h, (j * SUBLANES // (2 * m)) * 2 * m + m - 1, SUBLANES) for j in range(ntile)]
                    elif m == 4:
                        tiles = [b_row(h, j * SUBLANES + 3, SUBLANES) for j in range(ntile)]
                    else:
                        tiles = [jnp.where(sub < 4, b_row(h, j * SUBLANES + 1, SUBLANES),
                                           b_row(h, j * SUBLANES + 5, SUBLANES)) for j in range(ntile)]
                    mid = jnp.concatenate(tiles, axis=0)
                    e = jnp.exp2(jnp.where(isq, b[h] - mid, mid - b[h]))
                qt = jnp.where(isq, q[h] * e, 0.0).astype(BF16)
                kt = jnp.where(isq, 0.0, k[h] * e).astype(BF16)
                attn = attn + _dot_nt(qt, kt) * mask_ref[lv + 1]
            oi_ref[:, h * dh:(h + 1) * dh] = _dot(attn.astype(BF16), v[h])

    o = oi_ref[...] + rest_ref[...]
    o = jnp.concatenate(
        [lanes(o, h) * lax.rsqrt(jnp.mean(lanes(o, h) * lanes(o, h), axis=-1, keepdims=True) + RMS_EPS) for h in hs],
        axis=1)
    o = o * gain_ref[...] * _silu(g_ref[...].astype(F32))
    o_ref[...] = o.astype(o_ref.dtype)


def _hgrn_mixer(proj, lb_logits, head_gain, layer, j, bsz, seq):
    t, d4 = proj.shape
    d = d4 // 4
    nh = d // HG_HEAD_DIM
    heads = next(n for n in (16, 8, 4, 2, 1) if nh % n == 0)
    w = heads * HG_HEAD_DIM
    nhp = nh // heads
    chunk = 256 if seq % 256 == 0 else 128
    nc = seq // chunk
    nlev = int(math.log2(chunk))
    base_lv = int(math.log2(HG_BASE))

    def col(part):
        return lambda b, h, c: (b * nc + c, part * nhp + h)

    return pl.pallas_call(
        functools.partial(_hgrn_kernel, layer=layer, chunk=chunk, heads=heads),
        grid=(bsz, nhp, nc),
        in_specs=[
            pl.BlockSpec((chunk, w), col(0)),
            pl.BlockSpec((chunk, w), col(1)),
            pl.BlockSpec((chunk, w), col(2)),
            pl.BlockSpec((chunk, w), col(3)),
            pl.BlockSpec((lb_logits.shape[0], w), lambda b, h, c: (0, h)),
            pl.BlockSpec((None, 1, w), lambda b, h, c: (j, 0, h)),
        ],
        out_specs=pl.BlockSpec((chunk, w), lambda b, h, c: (b * nc + c, h)),
        out_shape=jax.ShapeDtypeStruct((t, d), BF16),
        scratch_shapes=[
            pltpu.VMEM((heads, HG_HEAD_DIM, HG_HEAD_DIM), F32),
            pltpu.VMEM((heads, chunk, HG_HEAD_DIM), F32),
            pltpu.VMEM((chunk, w), F32),
            pltpu.VMEM((chunk, w), F32),
            pltpu.VMEM((chunk, chunk), BF16),
            pltpu.VMEM((base_lv + 1, chunk, chunk), F32),
            pltpu.VMEM((chunk, chunk), F32),
            pltpu.VMEM((nlev - 1 - base_lv, chunk // 2, chunk // 2), F32),
        ],
        compiler_params=_cparams(3),
        name="hgrn2_mixer",
    )(proj, proj, proj, proj, lb_logits.astype(F32), head_gain.astype(F32).reshape(head_gain.shape[0], 1, d))


def _t5_bucket(dist):
    max_exact = N_BUCKETS // 2
    n = jnp.maximum(dist, 0)
    large = max_exact + (jnp.log(jnp.maximum(n, 1).astype(F32) / max_exact)
                         / math.log(MAX_DISTANCE / max_exact)
                         * (N_BUCKETS - max_exact)).astype(I32)
    large = jnp.minimum(large, N_BUCKETS - 1)
    return jnp.where(n < max_exact, n, large)


def _attn_kernel(rb_ref, sink_ref, q_ref, kp_ref, kc_ref, vp_ref, vc_ref, bucket_ref, o_ref,
                 bias_ref, *, group, pairs, layer_j):
    blk = ATT_BLOCK
    hd = ATT_HEAD_DIM
    pr = pl.program_id(0)
    i = pl.program_id(2)
    heads_per_step = 2 * group * pairs
    upper = lax.broadcasted_iota(I32, (blk, blk), 1) > lax.broadcasted_iota(I32, (blk, blk), 0)

    @pl.when((pl.program_id(1) == 0) & (i == 0))
    def _():
        bucket = bucket_ref[...]
        bucket_m = jnp.where(upper, bucket[:, 0:blk], bucket[:, blk:2 * blk])

        def per_head(hh, carry):
            h = pr * heads_per_step + hh
            tbl = jnp.zeros((blk, blk), F32)
            for bk in range(N_BUCKETS):
                tbl = jnp.where(bucket_m == bk, rb_ref[bk, h], tbl)
            bias_ref[hh] = tbl
            bias_ref[heads_per_step + hh] = jnp.where(upper, NEG_INF, tbl)
            return carry

        lax.fori_loop(0, heads_per_step, per_head, 0)

    lane = lax.broadcasted_iota(I32, (2 * blk, LANES), 1)
    table0 = jnp.where(i == 0, heads_per_step, 0)

    q_all = q_ref[...] * (hd ** -0.5)

    kbds, vbds = [], []
    for pp in range(pairs):
        kk = jnp.concatenate([kp_ref[:, pp * LANES:(pp + 1) * LANES], kc_ref[:, pp * LANES:(pp + 1) * LANES]],
                             axis=0).astype(F32)
        vv = jnp.concatenate([vp_ref[:, pp * LANES:(pp + 1) * LANES], vc_ref[:, pp * LANES:(pp + 1) * LANES]],
                             axis=0).astype(F32)
        for c in range(2):
            if c == 0:
                klo = jnp.where(lane < hd, kk, 0.0)
                khi = pltpu.roll(klo, hd, axis=1)
                vlo = jnp.where(lane < hd, vv, 0.0)
                vhi = pltpu.roll(vlo, hd, axis=1)
            else:
                khi = jnp.where(lane >= hd, kk, 0.0)
                klo = pltpu.roll(khi, hd, axis=1)
                vhi = jnp.where(lane >= hd, vv, 0.0)
                vlo = pltpu.roll(vhi, hd, axis=1)
            kbds.append(jnp.concatenate([klo, khi], axis=0).astype(BF16))
            vbds.append(jnp.concatenate([vlo, vhi], axis=0).astype(BF16))

    half = group // 2
    tiles = [(c, p) for c in range(2 * pairs) for p in range(half)]
    lgs = [_dot_nt(q_all[:, (c * half + p) * LANES:(c * half + p + 1) * LANES], kbds[c]) for c, p in tiles]
    p2s, rinvs = [], []
    for (c, p), lg in zip(tiles, lgs):
        probs, rinv = [], []
        for hh in range(2):
            hl = c * group + 2 * p + hh
            sk = sink_ref[layer_j, pr * heads_per_step + hl]
            c0 = hh * 2 * blk
            l = jnp.where(upper, lg[:, c0:c0 + blk], lg[:, c0 + blk:c0 + 2 * blk]) + bias_ref[table0 + hl]
            mx = jnp.maximum(jnp.max(l, axis=-1, keepdims=True), sk)
            pe = jnp.exp(l - mx)
            rinv.append(1.0 / (jnp.sum(pe, axis=-1, keepdims=True) + jnp.exp(sk - mx)))
            pe = pe.astype(BF16)
            zero = jnp.zeros_like(pe)
            probs += [jnp.where(upper, pe, zero), jnp.where(upper, zero, pe)]
        p2s.append(jnp.concatenate(probs, axis=1))
        rinvs.append(rinv)
    lane_o = lax.broadcasted_iota(I32, (blk, LANES), 1)
    for (c, p), p2, rinv in zip(tiles, p2s, rinvs):
        tile = c * half + p
        o = _dot(p2, vbds[c]) * jnp.where(lane_o < hd, rinv[0], rinv[1])
        o_ref[:, tile * LANES:(tile + 1) * LANES] = o.astype(o_ref.dtype)


def _attn_mixer(proj, sinks, rel_bias, layer_j, bsz, seq, d):
    t, att_in = proj.shape
    kvw = (att_in - d) // 2
    n_heads = d // ATT_HEAD_DIM
    kvh = kvw // ATT_HEAD_DIM
    group = n_heads // kvh
    assert kvh % 2 == 0 and group % 2 == 0
    assert WINDOW == ATT_BLOCK
    blk = ATT_BLOCK
    nb = seq // blk
    npair = kvh // 2
    pairs = 1
    ngrp = npair // pairs
    qw = 2 * group * ATT_HEAD_DIM * pairs
    kw = LANES * pairs
    k0 = d // kw
    v0 = (d + kvw) // kw
    assert d % kw == 0 and (d + kvw) % kw == 0

    qi = jnp.arange(blk)[:, None]
    sj = jnp.arange(2 * blk)[None, :]
    bucket = _t5_bucket(qi + blk - sj).astype(I32)

    def prev(i):
        return jnp.maximum(i - 1, 0)

    grid_spec = pltpu.PrefetchScalarGridSpec(
        num_scalar_prefetch=2,
        grid=(ngrp, bsz, nb),
        in_specs=[
            pl.BlockSpec((blk, qw), lambda p, b, i, *_: (b * nb + i, p)),
            pl.BlockSpec((blk, kw), lambda p, b, i, *_: (b * nb + prev(i), k0 + p)),
            pl.BlockSpec((blk, kw), lambda p, b, i, *_: (b * nb + i, k0 + p)),
            pl.BlockSpec((blk, kw), lambda p, b, i, *_: (b * nb + prev(i), v0 + p)),
            pl.BlockSpec((blk, kw), lambda p, b, i, *_: (b * nb + i, v0 + p)),
            pl.BlockSpec((blk, 2 * blk), lambda p, b, i, *_: (0, 0)),
        ],
        out_specs=pl.BlockSpec((blk, qw), lambda p, b, i, *_: (b * nb + i, p)),
        scratch_shapes=[pltpu.VMEM((4 * group * pairs, blk, blk), F32)],
    )
    return pl.pallas_call(
        functools.partial(_attn_kernel, group=group, pairs=pairs, layer_j=layer_j),
        grid_spec=grid_spec,
        out_shape=jax.ShapeDtypeStruct((t, d), BF16),
        compiler_params=_cparams(3),
        name="swa_sink_mixer",
    )(rel_bias.astype(F32), sinks.astype(F32), proj, proj, proj, proj, proj, bucket)


def _layer_norm_rows(z, g, b):
    mu = jnp.mean(z, axis=-1, keepdims=True)
    zc = z - mu
    var = jnp.mean(zc * zc, axis=-1, keepdims=True)
    return zc * lax.rsqrt(var + LN_EPS) * g + b


def _ln_router_kernel(x_ref, y_ref, gate_ref, sc_ref, sh_ref, lng_ref, lnb_ref, wr_ref, br_ref,
                      xo_ref, hf_ref, meta_ref, cnt_ref, carry_ref, wsplit_ref, *, alpha, n_groups, n_experts):
    i = pl.program_id(0)
    bm = x_ref.shape[0]
    epg = n_experts // n_groups

    @pl.when(i == 0)
    def _():
        carry_ref[...] = jnp.zeros_like(carry_ref)
        wh, wl = _split_bf16(wr_ref[...], 2)
        wsplit_ref[:, 0:LANES] = wh
        wsplit_ref[:, LANES:2 * LANES] = wl

    z = alpha * x_ref[...] + (1.0 + gate_ref[...]) * y_ref[...]
    xn = _layer_norm_rows(z, lng_ref[...], lnb_ref[...])
    xo_ref[...] = xn
    hf = xn * (1.0 + sc_ref[...]) + sh_ref[...]
    xh = hf.astype(BF16)
    xh32 = xh.astype(F32)
    hf_ref[...] = _pack_bf16_pairs(xh32, rounded=True)

    xl = (hf - xh32).astype(BF16)
    hh = _dot(xh, wsplit_ref[...])
    lg = hh[:, 0:LANES] + hh[:, LANES:2 * LANES] + _dot(xl, wsplit_ref[:, 0:LANES]) + br_ref[...]

    lane = lax.broadcasted_iota(I32, (bm, LANES), 1)
    lanef = lane.astype(F32)
    big = float(LANES)

    gl = jnp.where(lane < n_groups, lg, NEG_INF)
    gmax = jnp.max(gl, axis=-1, keepdims=True)
    gsel = jnp.min(jnp.where(gl == gmax, lanef, big), axis=-1, keepdims=True)
    p_group = 1.0 / jnp.sum(jnp.exp(gl - gmax), axis=-1, keepdims=True)

    lo = n_groups + gsel * epg
    el = jnp.where((lanef >= lo) & (lanef < lo + epg), lg, NEG_INF)
    m1 = jnp.max(el, axis=-1, keepdims=True)
    i1 = jnp.min(jnp.where(el == m1, lanef, big), axis=-1, keepdims=True)
    el2 = jnp.where(lanef == i1, NEG_INF, el)
    m2 = jnp.max(el2, axis=-1, keepdims=True)
    i2 = jnp.min(jnp.where(el2 == m2, lanef, big), axis=-1, keepdims=True)
    e21 = jnp.exp(m2 - m1)
    g0 = p_group / (1.0 + e21)
    g1 = g0 * e21

    oh0 = lanef == i1
    oh1 = lanef == i2
    cnt = jnp.where(oh0 | oh1, 1.0, 0.0)
    row = lax.broadcasted_iota(I32, (bm, bm), 0)
    col = lax.broadcasted_iota(I32, (bm, bm), 1)
    stril = jnp.where(row > col, 1.0, 0.0).astype(BF16)
    before = _dot(stril, cnt.astype(BF16)) + carry_ref[...]
    r0 = jnp.sum(jnp.where(oh0, before, 0.0), axis=-1, keepdims=True)
    r1 = jnp.sum(jnp.where(oh1, before, 0.0), axis=-1, keepdims=True)
    carry_ref[...] = carry_ref[...] + jnp.sum(cnt, axis=0, keepdims=True)
    cnt_ref[...] = carry_ref[...]

    meta = jnp.where(lane == 0, i1 - n_groups, 0.0)
    meta = jnp.where(lane == 1, i2 - n_groups, meta)
    meta = jnp.where(lane == 2, g0, meta)
    meta = jnp.where(lane == 3, g1, meta)
    meta = jnp.where(lane == 4, r0, meta)
    meta = jnp.where(lane == 5, r1, meta)
    meta_ref[...] = meta


def _ln_router(x2, y2, mod, row_of, ln_g, ln_b, w_router, b_router, alpha, n_groups, n_experts, seq):
    t, d = x2.shape
    bm = 256
    nbs = seq // bm

    def mrow(which):
        return lambda i: (row_of(i // nbs, which), 0, 0)

    return pl.pallas_call(
        functools.partial(_ln_router_kernel, alpha=alpha, n_groups=n_groups, n_experts=n_experts),
        grid=(t // bm,),
        in_specs=[
            pl.BlockSpec((bm, d), lambda i: (i, 0)),
            pl.BlockSpec((bm, d), lambda i: (i, 0)),
            pl.BlockSpec((None, 1, d), mrow(2)),
            pl.BlockSpec((None, 1, d), mrow(4)),
            pl.BlockSpec((None, 1, d), mrow(3)),
            pl.BlockSpec((1, d), lambda i: (0, 0)),
            pl.BlockSpec((1, d), lambda i: (0, 0)),
            pl.BlockSpec((d, LANES), lambda i: (0, 0)),
            pl.BlockSpec((1, LANES), lambda i: (0, 0)),
        ],
        out_specs=[
            pl.BlockSpec((bm, d), lambda i: (i, 0)),
            pl.BlockSpec((bm, d // 2), lambda i: (i, 0)),
            pl.BlockSpec((bm, LANES), lambda i: (i, 0)),
            pl.BlockSpec((1, LANES), lambda i: (0, 0)),
        ],
        out_shape=[
            jax.ShapeDtypeStruct((t, d), F32),
            jax.ShapeDtypeStruct((t, d // 2), jnp.uint32),
            jax.ShapeDtypeStruct((t, LANES), F32),
            jax.ShapeDtypeStruct((1, LANES), F32),
        ],
        scratch_shapes=[pltpu.VMEM((1, LANES), F32), pltpu.VMEM((d, 2 * LANES), BF16)],
        compiler_params=_cparams(1),
        name="ln_router",
    )(x2, y2, mod, mod, mod, ln_g, ln_b, w_router, b_router)


MOE_GATHER_SLOTS = 3


def _cast_rows(src_ref, dst_ref, rows=128):
    def body(r, carry):
        sl = pl.ds(pl.multiple_of(r * rows, rows), rows)
        dst_ref[sl, :] = src_ref[sl, :].astype(dst_ref.dtype)
        return carry
    lax.fori_loop(0, src_ref.shape[0] // rows, body, 0)


def _moe_kernel(blk_e_ref, nxt_e_ref, first_ref, grp_ref, nused_ref,
                tok0_ref, tok1_ref, tok2_ref, hf_hbm, wgu_hbm, wdn_hbm, y_ref,
                xbuf, wgu_st, wdn_st, wgu_bf, wdn_bf, gsem, wsem, *, layer, d_expert):
    i = pl.program_id(0)
    bm = xbuf.shape[1]
    nused = nused_ref[0]
    slot = i % MOE_GATHER_SLOTS

    def row_copy(tok, r, s):
        return pltpu.make_async_copy(hf_hbm.at[pl.ds(tok, 1), :], xbuf.at[s, pl.ds(r, 1), :], gsem.at[s])

    def issue_rows(tok_ref, s):
        for r in range(bm):
            row_copy(tok_ref[0, r], r, s).start()

    def wait_rows(s):
        pltpu.make_async_copy(hf_hbm.at[pl.ds(0, bm), :], xbuf.at[s], gsem.at[s]).wait()

    def wgu_copy(e):
        return pltpu.make_async_copy(wgu_hbm.at[layer, e], wgu_st, wsem.at[0])

    def wdn_copy(e, s):
        return pltpu.make_async_copy(wdn_hbm.at[layer, e], wdn_st.at[s], wsem.at[1 + s])

    @pl.when(i == 0)
    def _():
        issue_rows(tok0_ref, 0)
        wgu_copy(blk_e_ref[0]).start(priority=1)
        wdn_copy(blk_e_ref[0], 0).start(priority=1)

    @pl.when((i == 0) & (nused > 1))
    def _():
        issue_rows(tok1_ref, 1)

    @pl.when((i < nused) & (first_ref[i] == 1))
    def _():
        par = grp_ref[i] % 2
        has_next = nxt_e_ref[i] >= 0

        @pl.when(has_next)
        def _():
            wdn_copy(nxt_e_ref[i], 1 - par).start(priority=1)

        wgu_copy(blk_e_ref[i]).wait()
        _cast_rows(wgu_st, wgu_bf)

        @pl.when(has_next)
        def _():
            wgu_copy(nxt_e_ref[i]).start(priority=1)

        wdn_copy(blk_e_ref[i], par).wait()
        _cast_rows(wdn_st.at[par], wdn_bf)

    @pl.when(i + 2 < nused)
    def _():
        issue_rows(tok2_ref, (i + 2) % MOE_GATHER_SLOTS)

    @pl.when(i < nused)
    def _():
        wait_rows(slot)
        xb = _unpack_bf16_pairs(xbuf[slot]).astype(BF16)
        a = _dot(xb, wgu_bf[...])
        h = _silu(a[:, :d_expert]) * a[:, d_expert:]
        y_ref[...] = _pack_bf16_pairs(_dot(h.astype(BF16), wdn_bf[...]))

    @pl.when(i >= nused)
    def _():
        y_ref[...] = jnp.zeros_like(y_ref)


def _moe_experts(hf, tok_pad, blk_e, nxt_e, first, grp, nused, w_gate_up, w_down, layer, bm):
    d = w_down.shape[3]
    nb = blk_e.shape[0]
    d_expert = w_down.shape[2]
    tok3 = tok_pad.reshape(nb, 1, bm)
    grid_spec = pltpu.PrefetchScalarGridSpec(
        num_scalar_prefetch=5,
        grid=(nb,),
        in_specs=[
            pl.BlockSpec((None, 1, bm), lambda i, *_: (i, 0, 0), memory_space=pltpu.SMEM),
            pl.BlockSpec((None, 1, bm), lambda i, *_: (jnp.minimum(i + 1, nb - 1), 0, 0), memory_space=pltpu.SMEM),
            pl.BlockSpec((None, 1, bm), lambda i, *_: (jnp.minimum(i + 2, nb - 1), 0, 0), memory_space=pltpu.SMEM),
            pl.BlockSpec(memory_space=pl.ANY),
            pl.BlockSpec(memory_space=pl.ANY),
            pl.BlockSpec(memory_space=pl.ANY),
        ],
        out_specs=pl.BlockSpec((bm, d // 2), lambda i, *_: (i, 0)),
        scratch_shapes=[
            pltpu.VMEM((MOE_GATHER_SLOTS, bm, d // 2), jnp.uint32),
            pltpu.VMEM((d, 2 * d_expert), F32),
            pltpu.VMEM((2, d_expert, d), F32),
            pltpu.VMEM((d, 2 * d_expert), BF16),
            pltpu.VMEM((d_expert, d), BF16),
            pltpu.SemaphoreType.DMA((MOE_GATHER_SLOTS,)),
            pltpu.SemaphoreType.DMA((3,)),
        ],
    )
    return pl.pallas_call(
        functools.partial(_moe_kernel, layer=layer, d_expert=d_expert),
        grid_spec=grid_spec,
        out_shape=jax.ShapeDtypeStruct((nb * bm, d // 2), jnp.uint32),
        compiler_params=_cparams(1),
        name="moe_experts",
    )(blk_e, nxt_e, first, grp, nused, tok3, tok3, tok3, hf, w_gate_up, w_down)


def _moe_plan(meta, counts_row, n_groups, n_experts, bm):
    t = meta.shape[0]
    eid = meta[:, 0:2].astype(I32)
    rank = meta[:, 4:6].astype(I32)
    counts = counts_row[0, n_groups:n_groups + n_experts].astype(I32)
    padded = (counts + bm - 1) // bm * bm
    pad_end = jnp.cumsum(padded)
    pad_start = pad_end - padded
    start_of = jnp.sum(jnp.where(eid[..., None] == jnp.arange(n_experts, dtype=I32), pad_start, 0), axis=-1)
    dest = start_of + rank
    nb = (2 * t) // bm + n_experts
    nused = pad_end[-1] // bm
    ids = jnp.arange(nb, dtype=I32)
    raw_e = jnp.minimum(jnp.sum((pad_end[None, :] <= (ids * bm)[:, None]).astype(I32), axis=1), n_experts - 1)
    used = ids < nused
    blk_e = jnp.where(used, raw_e, raw_e[nused - 1])
    prev_e = jnp.concatenate([jnp.full((1,), -1, I32), blk_e[:-1]])
    first = (used & (blk_e != prev_e)).astype(I32)
    grp = jnp.cumsum(first) - 1
    key = jnp.where(used, blk_e, n_experts)
    nxt_idx = jnp.sum((key[None, :] <= blk_e[:, None]).astype(I32), axis=1)
    nxt_e = jnp.where(nxt_idx < nused, key[jnp.minimum(nxt_idx, nb - 1)], -1).astype(I32)
    tok = jnp.repeat(jnp.arange(t, dtype=I32), 2)
    tok_pad = jnp.zeros((nb * bm,), I32).at[dest.reshape(-1)].set(tok)
    return dest, tok_pad, blk_e, nxt_e, first, grp.astype(I32), nused.reshape(1).astype(I32)


def _ln_combine_kernel(*refs, alpha, with_next):
    if with_next:
        (dc_ref, dn_ref, x_ref, meta_ref, gate_ref, lng_ref, lnb_ref, sc_ref, sh_ref, y_hbm,
         xo_ref, hm_ref, ybuf, sem) = refs
    else:
        (dc_ref, dn_ref, x_ref, meta_ref, gate_ref, lng_ref, lnb_ref, y_hbm,
         xo_ref, ybuf, sem) = refs
    i = pl.program_id(0)
    n = pl.num_programs(0)
    bm = x_ref.shape[0]
    slot = i % 2

    def row_copy(src, r, s):
        return pltpu.make_async_copy(y_hbm.at[pl.ds(src, 1), :], ybuf.at[s, pl.ds(r, 1), :], sem.at[s])

    def issue_rows(d_ref, s):
        for r in range(2 * bm):
            row_copy(d_ref[0, r], r, s).start()

    def wait_rows(s):
        pltpu.make_async_copy(y_hbm.at[pl.ds(0, 2 * bm), :], ybuf.at[s], sem.at[s]).wait()

    @pl.when(i == 0)
    def _():
        issue_rows(dc_ref, 0)

    @pl.when(i + 1 < n)
    def _():
        issue_rows(dn_ref, 1 - slot)

    wait_rows(slot)
    meta = meta_ref[...]
    y = (_unpack_bf16_pairs(ybuf[slot, 0:bm, :]) * meta[:, 2:3]
         + _unpack_bf16_pairs(ybuf[slot, bm:2 * bm, :]) * meta[:, 3:4])
    z = alpha * x_ref[...] + (1.0 + gate_ref[...]) * y
    xn = _layer_norm_rows(z, lng_ref[...], lnb_ref[...])
    xo_ref[...] = xn
    if with_next:
        hm_ref[...] = (xn * (1.0 + sc_ref[...]) + sh_ref[...]).astype(hm_ref.dtype)


def _ln_combine(x2, meta, dest, ysort, mod, row_of, next_row_of, ln_g, ln_b, alpha, seq):
    t, d = x2.shape
    bm = 256
    nbs = seq // bm
    nblk = t // bm
    with_next = next_row_of is not None
    dest3 = dest.reshape(nblk, bm, 2).transpose(0, 2, 1).reshape(nblk, 1, 2 * bm)

    def mrow(fn, which):
        return lambda i: (fn(i // nbs, which), 0, 0)

    in_specs = [
        pl.BlockSpec((None, 1, 2 * bm), lambda i: (i, 0, 0), memory_space=pltpu.SMEM),
        pl.BlockSpec((None, 1, 2 * bm), lambda i: (jnp.minimum(i + 1, nblk - 1), 0, 0), memory_space=pltpu.SMEM),
        pl.BlockSpec((bm, d), lambda i: (i, 0)),
        pl.BlockSpec((bm, LANES), lambda i: (i, 0)),
        pl.BlockSpec((None, 1, d), mrow(row_of, 5)),
        pl.BlockSpec((1, d), lambda i: (0, 0)),
        pl.BlockSpec((1, d), lambda i: (0, 0)),
    ]
    args = [dest3, dest3, x2, meta, mod, ln_g, ln_b]
    out_specs = [pl.BlockSpec((bm, d), lambda i: (i, 0))]
    out_shape = [jax.ShapeDtypeStruct((t, d), F32)]
    if with_next:
        in_specs += [pl.BlockSpec((None, 1, d), mrow(next_row_of, 1)),
                     pl.BlockSpec((None, 1, d), mrow(next_row_of, 0))]
        args += [mod, mod]
        out_specs.append(pl.BlockSpec((bm, d), lambda i: (i, 0)))
        out_shape.append(jax.ShapeDtypeStruct((t, d), BF16))
    in_specs.append(pl.BlockSpec(memory_space=pl.ANY))
    args.append(ysort)
    return pl.pallas_call(
        functools.partial(_ln_combine_kernel, alpha=alpha, with_next=with_next),
        grid=(nblk,),
        in_specs=in_specs,
        out_specs=out_specs,
        out_shape=out_shape,
        scratch_shapes=[pltpu.VMEM((2, 2 * bm, d // 2), jnp.uint32), pltpu.SemaphoreType.DMA((2,))],
        compiler_params=_cparams(1),
        name="ln_moe_combine",
    )(*args)


def kernel(x, c, w_ada, b_ada, ln_g, ln_b, w_in_a, lb_logits, head_gain_a, w_out_a, w_in_b, attn_sinks, w_out_b, rel_bias, w_router_group, b_router_group, w_router_expert, b_router_expert, w_gate_up, w_down):
    bsz, seq, d = x.shape
    depth = w_ada.shape[0]
    n_groups = w_router_group.shape[2]
    n_experts = w_router_expert.shape[2]
    alpha = (2 * depth) ** 0.25
    t = bsz * seq
    moe_bm = 128

    mod = _ada_modulation(c, w_ada, b_ada)

    def row_of_layer(layer):
        return lambda b, which: (layer * bsz + b) * 6 + which

    x2 = x.reshape(t, d).astype(F32)
    row0 = row_of_layer(0)
    hm = _modulate(x2, mod, lambda b: row0(b, 1), lambda b: row0(b, 0), bsz, seq)

    for layer in range(depth):
        row_of = row_of_layer(layer)
        j = layer // 2
        if layer % 2 == 0:
            proj = _matmul(hm, w_in_a, j, BF16)
            o = _hgrn_mixer(proj, lb_logits, head_gain_a, layer, j, bsz, seq)
            y = _matmul(o, w_out_a, j, F32)
        else:
            proj = _matmul(hm, w_in_b, j, BF16)
            o = _attn_mixer(proj, attn_sinks, rel_bias, j, bsz, seq, d)
            y = _matmul(o, w_out_b, j, F32)

        n_pad = LANES - n_groups - n_experts
        w_router = jnp.concatenate(
            [w_router_group[layer].astype(F32), w_router_expert[layer].astype(F32), jnp.zeros((d, n_pad), F32)], axis=1)
        b_router = jnp.concatenate(
            [b_router_group[layer].astype(F32), b_router_expert[layer].astype(F32), jnp.zeros((n_pad,), F32)]
        ).reshape(1, LANES)

        x2, hf, meta, counts = _ln_router(
            x2, y, mod, row_of, ln_g[layer, 0:1].astype(F32), ln_b[layer, 0:1].astype(F32),
            w_router, b_router, alpha, n_groups, n_experts, seq)
        dest, tok_pad, blk_e, nxt_e, first, grp, nused = _moe_plan(meta, counts, n_groups, n_experts, moe_bm)
        ysort = _moe_experts(hf, tok_pad, blk_e, nxt_e, first, grp, nused, w_gate_up, w_down, layer, moe_bm)
        next_row_of = row_of_layer(layer + 1) if layer + 1 < depth else None
        outs = _ln_combine(x2, meta, dest, ysort, mod, row_of, next_row_of,
                           ln_g[layer, 1:2].astype(F32), ln_b[layer, 1:2].astype(F32), alpha, seq)
        x2 = outs[0]
        if next_row_of is not None:
            hm = outs[1]

    return x2.reshape(bsz, seq, d).astype(x.dtype)
```

```python
import functools
import math

import jax
import jax.numpy as jnp
from jax import lax
from jax.experimental import pallas as pl
from jax.experimental.pallas import tpu as pltpu

F32 = jnp.float32
BF16 = jnp.bfloat16
I32 = jnp.int32

LANES = 128
SUBLANES = 8
V7X_VMEM_LIMIT_BYTES = 56 * 1024 * 1024

HG_HEAD_DIM = 128
ATT_HEAD_DIM = 64
ATT_BLOCK = 128
WINDOW = 128
N_BUCKETS = 32
MAX_DISTANCE = 128
LN_EPS = 1e-5
RMS_EPS = 1e-6
NEG_INF = float("-inf")


def _cparams(n_axes):
    return pltpu.CompilerParams(
        dimension_semantics=("arbitrary",) * n_axes,
        vmem_limit_bytes=V7X_VMEM_LIMIT_BYTES,
    )


def _sigmoid(x):
    return 0.5 * jnp.tanh(0.5 * x) + 0.5


def _silu(x):
    return x * _sigmoid(x)


def _dot_nt(a, b):
    return lax.dot_general(a, b, (((1,), (1,)), ((), ())), preferred_element_type=F32)


def _dot_tn(a, b):
    return lax.dot_general(a, b, (((0,), (0,)), ((), ())), preferred_element_type=F32)


def _dot(a, b):
    return jnp.dot(a, b, preferred_element_type=F32)


def _pack_bf16_pairs(x, rounded=False):
    n = x.shape[1] // 2
    bits = pltpu.bitcast(x if rounded else x.astype(BF16).astype(F32), jnp.uint32)
    return (bits[:, :n] >> 16) | (bits[:, n:] & jnp.uint32(0xFFFF0000))


def _unpack_bf16_pairs(w):
    lo = pltpu.bitcast(w << 16, F32)
    hi = pltpu.bitcast(w & jnp.uint32(0xFFFF0000), F32)
    return jnp.concatenate([lo, hi], axis=1)


def _split_bf16(x, parts):
    out = []
    r = x
    for _ in range(parts):
        h = r.astype(BF16)
        out.append(h)
        r = r - h.astype(F32)
    return out


def _ada_kernel(c_ref, w_ref, b_ref, o_ref):
    ca = _silu(c_ref[...]).astype(BF16)
    o_ref[...] = _dot(ca, w_ref[...].astype(BF16)) + b_ref[...]


def _ada_modulation(c, w_ada, b_ada):
    nl, d, n6 = w_ada.shape
    bsz = c.shape[0]
    rows = -(-bsz // SUBLANES) * SUBLANES
    c8 = jnp.zeros((rows, d), F32).at[:bsz].set(c.astype(F32))
    tn = 512
    out = pl.pallas_call(
        _ada_kernel,
        grid=(nl, n6 // tn),
        in_specs=[
            pl.BlockSpec((rows, d), lambda l, j: (0, 0)),
            pl.BlockSpec((None, d, tn), lambda l, j: (l, 0, j)),
            pl.BlockSpec((None, 1, tn), lambda l, j: (l, 0, j)),
        ],
        out_specs=pl.BlockSpec((None, rows, tn), lambda l, j: (l, 0, j)),
        out_shape=jax.ShapeDtypeStruct((nl, rows, n6), F32),
        compiler_params=_cparams(2),
        name="ada_modulation",
    )(c8, w_ada, b_ada.reshape(nl, 1, n6))
    return out[:, :bsz].reshape(nl * bsz * 6, 1, d)


def _modulate_kernel(x_ref, sc_ref, sh_ref, o_ref):
    o_ref[...] = (x_ref[...] * (1.0 + sc_ref[...]) + sh_ref[...]).astype(o_ref.dtype)


def _modulate(x2, mod, sc_row, sh_row, bsz, seq):
    t, d = x2.shape
    bs = min(512, seq)
    nbs = seq // bs
    return pl.pallas_call(
        _modulate_kernel,
        grid=(t // bs,),
        in_specs=[
            pl.BlockSpec((bs, d), lambda i: (i, 0)),
            pl.BlockSpec((None, 1, d), lambda i: (sc_row(i // nbs), 0, 0)),
            pl.BlockSpec((None, 1, d), lambda i: (sh_row(i // nbs), 0, 0)),
        ],
        out_specs=pl.BlockSpec((bs, d), lambda i: (i, 0)),
        out_shape=jax.ShapeDtypeStruct((t, d), BF16),
        compiler_params=_cparams(1),
        name="modulate",
    )(x2, mod, mod)


def _matmul_kernel(x_ref, w_hbm, o_ref, wbf_ref, stage_ref, sem, *, layer, bn, kc):
    j = pl.program_id(0)
    i = pl.program_id(1)
    nj = pl.num_programs(0)
    cur = j % 2

    def slab_copy(jb, c):
        return pltpu.make_async_copy(
            w_hbm.at[layer, pl.ds(c * kc, kc), pl.ds(jb * bn, bn)], stage_ref, sem.at[0])

    def cast_slab(c, buf):
        wbf_ref[buf, pl.ds(c * kc, kc), :] = stage_ref[...].astype(BF16)

    @pl.when((j == 0) & (i == 0))
    def _():
        def first_block(c, carry):
            cp = slab_copy(0, c)
            cp.start()
            cp.wait()
            cast_slab(pl.multiple_of(c, 1), 0)
            return carry
        lax.fori_loop(0, pl.num_programs(1), first_block, 0)

    @pl.when(j + 1 < nj)
    def _():
        slab_copy(j + 1, i).start()

    o_ref[...] = _dot(x_ref[...], wbf_ref[cur]).astype(o_ref.dtype)

    @pl.when(j + 1 < nj)
    def _():
        slab_copy(j + 1, i).wait()
        cast_slab(i, 1 - cur)


def _matmul(x, w3, layer, out_dtype):
    m, k = x.shape
    n = w3.shape[2]
    bm = min(1024, m)
    bn = 1024 if n % 1024 == 0 else 512
    steps = m // bm
    kc = k // steps
    assert k % steps == 0 and kc % SUBLANES == 0
    return pl.pallas_call(
        functools.partial(_matmul_kernel, layer=layer, bn=bn, kc=kc),
        grid=(n // bn, steps),
        in_specs=[
            pl.BlockSpec((bm, k), lambda j, i: (i, 0)),
            pl.BlockSpec(memory_space=pl.ANY),
        ],
        out_specs=pl.BlockSpec((bm, bn), lambda j, i: (i, j)),
        out_shape=jax.ShapeDtypeStruct((m, n), out_dtype),
        scratch_shapes=[
            pltpu.VMEM((2, k, bn), BF16),
            pltpu.VMEM((kc, bn), F32),
            pltpu.SemaphoreType.DMA((1,)),
        ],
        compiler_params=_cparams(2),
        name="dense_projection",
    )(x, w3)


HG_BASE = 32
HG_BASE_MAX_DECAY = 86.0


def _hgrn_kernel(q_ref, f_ref, v_ref, g_ref, lbl_ref, gain_ref, o_ref,
                 st_ref, b_ref, oi_ref, rest_ref, tri_ref, mask_ref, bmask_ref, cmask_ref, *, layer, chunk, heads):
    c = chunk
    hc = c // 2
    dh = HG_HEAD_DIM
    nlev = int(math.log2(c))
    base_lv = int(math.log2(HG_BASE))
    nbig = nlev - base_lv
    hs = range(heads)
    first = (pl.program_id(0) == 0) & (pl.program_id(1) == 0) & (pl.program_id(2) == 0)

    @pl.when(first)
    def _():
        row = lax.broadcasted_iota(I32, (c, c), 0)
        col = lax.broadcasted_iota(I32, (c, c), 1)
        tri_ref[...] = jnp.where(row >= col, 1.0, 0.0).astype(BF16)
        x = row ^ col
        mask_ref[0] = jnp.where(x == 0, 1.0, 0.0).astype(F32)
        for lv in range(1, base_lv + 1):
            mask_ref[lv] = jnp.where(x < (1 << lv), 1.0, 0.0).astype(F32)
        bmask_ref[...] = jnp.where((x < HG_BASE) & (row >= col), 1.0, 0.0).astype(F32)
        xh = lax.broadcasted_iota(I32, (hc, hc), 0) ^ lax.broadcasted_iota(I32, (hc, hc), 1)
        for lv in range(base_lv, nlev - 1):
            cmask_ref[lv - base_lv] = jnp.where(xh < (1 << lv), 1.0, 0.0).astype(F32)

    @pl.when(pl.program_id(2) == 0)
    def _():
        st_ref[...] = jnp.zeros_like(st_ref)

    lbl = lbl_ref[...]
    rows = [lbl[i:i + 1, :] for i in range(lbl.shape[0])]
    mx = functools.reduce(jnp.maximum, rows)
    es = [jnp.exp(r - mx) for r in rows]
    lb = functools.reduce(lambda a, b: a + b, es[:layer + 1]) / functools.reduce(lambda a, b: a + b, es)

    q_all = _silu(q_ref[...].astype(F32))
    forget_all = lb + (1.0 - lb) * _sigmoid(f_ref[...].astype(F32))
    k_all = 1.0 - forget_all
    v_all = v_ref[...]

    w = heads * dh
    bb = _dot(tri_ref[...], jnp.concatenate(_split_bf16(jnp.log2(forget_all), 3), axis=1))
    b_all = bb[:, 0:w] + bb[:, w:2 * w] + bb[:, 2 * w:3 * w]

    def lanes(x, h):
        return x[:, h * dh:(h + 1) * dh]

    for h in hs:
        b_ref[h] = lanes(b_all, h)

    q = [lanes(q_all, h) for h in hs]
    k = [lanes(k_all, h) for h in hs]
    v = [lanes(v_all, h) for h in hs]
    b = [lanes(b_all, h) for h in hs]

    def b_row(h, r, n):
        return jnp.broadcast_to(b_ref[h, pl.ds(r, 1), :], (n, dh))

    blk_i = lax.broadcasted_iota(I32, (c // HG_BASE, dh), 0)
    base_ok = []
    for h in hs:
        ends = b_ref[h, pl.ds(HG_BASE - 1, c // HG_BASE, stride=HG_BASE), :]
        drop = jnp.where(blk_i == 0, 0.0, pltpu.roll(ends, 1, axis=0)) - ends
        base_ok.append(jnp.max(drop) <= HG_BASE_MAX_DECAY)

    qs_l, ks_l, vs_l = [], [], []
    for lv in range(base_lv, nlev):
        m = 1 << lv
        n = 2 * m
        for h in hs:
            qs, ks, vs = [], [], []
            for a in range(c // n):
                mid = b_row(h, a * n + m - 1, m)
                qs.append(q[h][a * n + m:(a + 1) * n] * jnp.exp2(b[h][a * n + m:(a + 1) * n] - mid))
                ks.append(k[h][a * n:a * n + m] * jnp.exp2(mid - b[h][a * n:a * n + m]))
                vs.append(v[h][a * n:a * n + m])
            qs_l.append(jnp.concatenate(qs, axis=0).astype(BF16))
            ks_l.append(jnp.concatenate(ks, axis=0).astype(BF16))
            vs_l.append(jnp.concatenate(vs, axis=0))
    qb16, kb16, qe, kd, b_last, st = [], [], [], [], [], []
    for h in hs:
        start = jnp.concatenate(
            [jnp.zeros((HG_BASE, dh), F32)] + [b_row(h, j * HG_BASE - 1, HG_BASE) for j in range(1, c // HG_BASE)],
            axis=0)
        dlt = jnp.maximum(b[h] - start, -HG_BASE_MAX_DECAY)
        qb16.append((q[h] * jnp.exp2(dlt)).astype(BF16))
        kb16.append((k[h] * jnp.exp2(-dlt)).astype(BF16))
        b_last.append(b_row(h, c - 1, c))
        qe.append((q[h] * jnp.exp2(b[h])).astype(BF16))
        kd.append((k[h] * jnp.exp2(b_last[h] - b[h])).astype(BF16))
        st.append(st_ref[h])

    nprod = nbig * heads
    a_l = [_dot_nt(qs_l[i], ks_l[i]) for i in range(nprod)]
    a16 = [_dot_nt(qb16[h], kb16[h]) for h in hs]
    o_inter = [_dot_nt(qe[h], st[h].astype(BF16)) for h in hs]
    for h in hs:
        st_ref[h] = st[h] * jnp.exp2(b_last[h][0:1, :]) + _dot_tn(v[h], kd[h])

    a_l = [(a_l[i] * cmask_ref[i // heads] if i // heads < nbig - 1 else a_l[i]).astype(BF16) for i in range(nprod)]
    a16 = [(a16[h] * bmask_ref[...]).astype(BF16) for h in hs]
    o_l = [_dot(a_l[i], vs_l[i]) for i in range(nprod)]
    oi_ref[...] = jnp.concatenate([_dot(a16[h], v[h]) for h in hs], axis=1)

    rest = []
    for h in hs:
        pieces = [None] * (c // HG_BASE)
        for lvi in range(nbig):
            m = HG_BASE << lvi
            per = m // HG_BASE
            o_c = o_l[lvi * heads + h]
            for a in range(c // (2 * m)):
                for u in range(per):
                    dst = (a * 2 * m + m) // HG_BASE + u
                    src = o_c[(a * per + u) * HG_BASE:(a * per + u + 1) * HG_BASE]
                    pieces[dst] = src if pieces[dst] is None else pieces[dst] + src
        zero_slab = jnp.zeros((HG_BASE, dh), F32)
        rest.append(o_inter[h] + jnp.concatenate([zero_slab if p is None else p for p in pieces], axis=0))
    rest_ref[...] = jnp.concatenate(rest, axis=1)

    for h in hs:
        @pl.when(jnp.logical_not(base_ok[h]))
        def _(h=h):
            rowi = lax.broadcasted_iota(I32, (c, dh), 0)
            sub = lax.broadcasted_iota(I32, (SUBLANES, dh), 0)
            ntile = c // SUBLANES
            forget = lanes(forget_all, h)
            attn = _dot_nt(q[h].astype(BF16), k[h].astype(BF16)) * mask_ref[0]
            for lv in range(base_lv):
                m = 1 << lv
                isq = (rowi & m) != 0
                if m == 1:
                    e = jnp.where(isq, forget, 1.0)
                else:
                    if m >= SUBLANES:
                        tiles = [b_row(h, (j * SUBLANES // (2 * m)) * 2 * m + m - 1, SUBLANES) for j in range(ntile)]
                    elif m == 4:
                        tiles = [b_row(h, j * SUBLANES + 3, SUBLANES) for j in range(ntile)]
                    else:
                        tiles = [jnp.where(sub < 4, b_row(h, j * SUBLANES + 1, SUBLANES),
                                           b_row(h, j * SUBLANES + 5, SUBLANES)) for j in range(ntile)]
                    mid = jnp.concatenate(tiles, axis=0)
                    e = jnp.exp2(jnp.where(isq, b[h] - mid, mid - b[h]))
                qt = jnp.where(isq, q[h] * e, 0.0).astype(BF16)
                kt = jnp.where(isq, 0.0, k[h] * e).astype(BF16)
                attn = attn + _dot_nt(qt, kt) * mask_ref[lv + 1]
            oi_ref[:, h * dh:(h + 1) * dh] = _dot(attn.astype(BF16), v[h])

    o = oi_ref[...] + rest_ref[...]
    o = jnp.concatenate(
        [lanes(o, h) * lax.rsqrt(jnp.mean(lanes(o, h) * lanes(o, h), axis=-1, keepdims=True) + RMS_EPS) for h in hs],
        axis=1)
    o = o * gain_ref[...] * _silu(g_ref[...].astype(F32))
    o_ref[...] = o.astype(o_ref.dtype)


def _hgrn_mixer(proj, lb_logits, head_gain, layer, j, bsz, seq):
    t, d4 = proj.shape
    d = d4 // 4
    nh = d // HG_HEAD_DIM
    heads = next(n for n in (16, 8, 4, 2, 1) if nh % n == 0)
    w = heads * HG_HEAD_DIM
    nhp = nh // heads
    chunk = 256 if seq % 256 == 0 else 128
    nc = seq // chunk
    nlev = int(math.log2(chunk))
    base_lv = int(math.log2(HG_BASE))

    def col(part):
        return lambda b, h, c: (b * nc + c, part * nhp + h)

    return pl.pallas_call(
        functools.partial(_hgrn_kernel, layer=layer, chunk=chunk, heads=heads),
        grid=(bsz, nhp, nc),
        in_specs=[
            pl.BlockSpec((chunk, w), col(0)),
            pl.BlockSpec((chunk, w), col(1)),
            pl.BlockSpec((chunk, w), col(2)),
            pl.BlockSpec((chunk, w), col(3)),
            pl.BlockSpec((lb_logits.shape[0], w), lambda b, h, c: (0, h)),
            pl.BlockSpec((None, 1, w), lambda b, h, c: (j, 0, h)),
        ],
        out_specs=pl.BlockSpec((chunk, w), lambda b, h, c: (b * nc + c, h)),
        out_shape=jax.ShapeDtypeStruct((t, d), BF16),
        scratch_shapes=[
            pltpu.VMEM((heads, HG_HEAD_DIM, HG_HEAD_DIM), F32),
            pltpu.VMEM((heads, chunk, HG_HEAD_DIM), F32),
            pltpu.VMEM((chunk, w), F32),
            pltpu.VMEM((chunk, w), F32),
            pltpu.VMEM((chunk, chunk), BF16),
            pltpu.VMEM((base_lv + 1, chunk, chunk), F32),
            pltpu.VMEM((chunk, chunk), F32),
            pltpu.VMEM((nlev - 1 - base_lv, chunk // 2, chunk // 2), F32),
        ],
        compiler_params=_cparams(3),
        name="hgrn2_mixer",
    )(proj, proj, proj, proj, lb_logits.astype(F32), head_gain.astype(F32).reshape(head_gain.shape[0], 1, d))


def _t5_bucket(dist):
    max_exact = N_BUCKETS // 2
    n = jnp.maximum(dist, 0)
    large = max_exact + (jnp.log(jnp.maximum(n, 1).astype(F32) / max_exact)
                         / math.log(MAX_DISTANCE / max_exact)
                         * (N_BUCKETS - max_exact)).astype(I32)
    large = jnp.minimum(large, N_BUCKETS - 1)
    return jnp.where(n < max_exact, n, large)


def _attn_kernel(rb_ref, sink_ref, q_ref, kp_ref, kc_ref, vp_ref, vc_ref, bucket_ref, o_ref,
                 bias_ref, *, group, pairs, layer_j):
    blk = ATT_BLOCK
    hd = ATT_HEAD_DIM
    pr = pl.program_id(0)
    i = pl.program_id(2)
    heads_per_step = 2 * group * pairs
    upper = lax.broadcasted_iota(I32, (blk, blk), 1) > lax.broadcasted_iota(I32, (blk, blk), 0)

    @pl.when((pl.program_id(1) == 0) & (i == 0))
    def _():
        bucket = bucket_ref[...]
        bucket_m = jnp.where(upper, bucket[:, 0:blk], bucket[:, blk:2 * blk])

        def per_head(hh, carry):
            h = pr * heads_per_step + hh
            tbl = jnp.zeros((blk, blk), F32)
            for bk in range(N_BUCKETS):
                tbl = jnp.where(bucket_m == bk, rb_ref[bk, h], tbl)
            bias_ref[hh] = tbl
            bias_ref[heads_per_step + hh] = jnp.where(upper, NEG_INF, tbl)
            return carry

        lax.fori_loop(0, heads_per_step, per_head, 0)

    lane = lax.broadcasted_iota(I32, (2 * blk, LANES), 1)
    table0 = jnp.where(i == 0, heads_per_step, 0)

    q_all = q_ref[...] * (hd ** -0.5)

    kbds, vbds = [], []
    for pp in range(pairs):
        kk = jnp.concatenate([kp_ref[:, pp * LANES:(pp + 1) * LANES], kc_ref[:, pp * LANES:(pp + 1) * LANES]],
                             axis=0).astype(F32)
        vv = jnp.concatenate([vp_ref[:, pp * LANES:(pp + 1) * LANES], vc_ref[:, pp * LANES:(pp + 1) * LANES]],
                             axis=0).astype(F32)
        for c in range(2):
            if c == 0:
                klo = jnp.where(lane < hd, kk, 0.0)
                khi = pltpu.roll(klo, hd, axis=1)
                vlo = jnp.where(lane < hd, vv, 0.0)
                vhi = pltpu.roll(vlo, hd, axis=1)
            else:
                khi = jnp.where(lane >= hd, kk, 0.0)
                klo = pltpu.roll(khi, hd, axis=1)
                vhi = jnp.where(lane >= hd, vv, 0.0)
                vlo = pltpu.roll(vhi, hd, axis=1)
            kbds.append(jnp.concatenate([klo, khi], axis=0).astype(BF16))
            vbds.append(jnp.concatenate([vlo, vhi], axis=0).astype(BF16))

    half = group // 2
    tiles = [(c, p) for c in range(2 * pairs) for p in range(half)]
    lgs = [_dot_nt(q_all[:, (c * half + p) * LANES:(c * half + p + 1) * LANES], kbds[c]) for c, p in tiles]
    p2s, rinvs = [], []
    for (c, p), lg in zip(tiles, lgs):
        probs, rinv = [], []
        for hh in range(2):
            hl = c * group + 2 * p + hh
            sk = sink_ref[layer_j, pr * heads_per_step + hl]
            c0 = hh * 2 * blk
            l = jnp.where(upper, lg[:, c0:c0 + blk], lg[:, c0 + blk:c0 + 2 * blk]) + bias_ref[table0 + hl]
            mx = jnp.maximum(jnp.max(l, axis=-1, keepdims=True), sk)
            pe = jnp.exp(l - mx)
            rinv.append(1.0 / (jnp.sum(pe, axis=-1, keepdims=True) + jnp.exp(sk - mx)))
            pe = pe.astype(BF16)
            zero = jnp.zeros_like(pe)
            probs += [jnp.where(upper, pe, zero), jnp.where(upper, zero, pe)]
        p2s.append(jnp.concatenate(probs, axis=1))
        rinvs.append(rinv)
    lane_o = lax.broadcasted_iota(I32, (blk, LANES), 1)
    for (c, p), p2, rinv in zip(tiles, p2s, rinvs):
        tile = c * half + p
        o = _dot(p2, vbds[c]) * jnp.where(lane_o < hd, rinv[0], rinv[1])
        o_ref[:, tile * LANES:(tile + 1) * LANES] = o.astype(o_ref.dtype)


def _attn_mixer(proj, sinks, rel_bias, layer_j, bsz, seq, d):
    t, att_in = proj.shape
    kvw = (att_in - d) // 2
    n_heads = d // ATT_HEAD_DIM
    kvh = kvw // ATT_HEAD_DIM
    group = n_heads // kvh
    assert kvh % 2 == 0 and group % 2 == 0
    assert WINDOW == ATT_BLOCK
    blk = ATT_BLOCK
    nb = seq // blk
    npair = kvh // 2
    pairs = 1
    ngrp = npair // pairs
    qw = 2 * group * ATT_HEAD_DIM * pairs
    kw = LANES * pairs
    k0 = d // kw
    v0 = (d + kvw) // kw
    assert d % kw == 0 and (d + kvw) % kw == 0

    qi = jnp.arange(blk)[:, None]
    sj = jnp.arange(2 * blk)[None, :]
    bucket = _t5_bucket(qi + blk - sj).astype(I32)

    def prev(i):
        return jnp.maximum(i - 1, 0)

    grid_spec = pltpu.PrefetchScalarGridSpec(
        num_scalar_prefetch=2,
        grid=(ngrp, bsz, nb),
        in_specs=[
            pl.BlockSpec((blk, qw), lambda p, b, i, *_: (b * nb + i, p)),
            pl.BlockSpec((blk, kw), lambda p, b, i, *_: (b * nb + prev(i), k0 + p)),
            pl.BlockSpec((blk, kw), lambda p, b, i, *_: (b * nb + i, k0 + p)),
            pl.BlockSpec((blk, kw), lambda p, b, i, *_: (b * nb + prev(i), v0 + p)),
            pl.BlockSpec((blk, kw), lambda p, b, i, *_: (b * nb + i, v0 + p)),
            pl.BlockSpec((blk, 2 * blk), lambda p, b, i, *_: (0, 0)),
        ],
        out_specs=pl.BlockSpec((blk, qw), lambda p, b, i, *_: (b * nb + i, p)),
        scratch_shapes=[pltpu.VMEM((4 * group * pairs, blk, blk), F32)],
    )
    return pl.pallas_call(
        functools.partial(_attn_kernel, group=group, pairs=pairs, layer_j=layer_j),
        grid_spec=grid_spec,
        out_shape=jax.ShapeDtypeStruct((t, d), BF16),
        compiler_params=_cparams(3),
        name="swa_sink_mixer",
    )(rel_bias.astype(F32), sinks.astype(F32), proj, proj, proj, proj, proj, bucket)


def _layer_norm_rows(z, g, b):
    mu = jnp.mean(z, axis=-1, keepdims=True)
    zc = z - mu
    var = jnp.mean(zc * zc, axis=-1, keepdims=True)
    return zc * lax.rsqrt(var + LN_EPS) * g + b


def _ln_router_kernel(x_ref, y_ref, gate_ref, sc_ref, sh_ref, lng_ref, lnb_ref, wr_ref, br_ref,
                      xo_ref, hf_ref, meta_ref, cnt_ref, carry_ref, wsplit_ref, *, alpha, n_groups, n_experts):
    i = pl.program_id(0)
    bm = x_ref.shape[0]
    epg = n_experts // n_groups

    @pl.when(i == 0)
    def _():
        carry_ref[...] = jnp.zeros_like(carry_ref)
        wh, wl = _split_bf16(wr_ref[...], 2)
        wsplit_ref[:, 0:LANES] = wh
        wsplit_ref[:, LANES:2 * LANES] = wl

    z = alpha * x_ref[...] + (1.0 + gate_ref[...]) * y_ref[...]
    xn = _layer_norm_rows(z, lng_ref[...], lnb_ref[...])
    xo_ref[...] = xn
    hf = xn * (1.0 + sc_ref[...]) + sh_ref[...]
    xh = hf.astype(BF16)
    xh32 = xh.astype(F32)
    hf_ref[...] = _pack_bf16_pairs(xh32, rounded=True)

    xl = (hf - xh32).astype(BF16)
    hh = _dot(xh, wsplit_ref[...])
    lg = hh[:, 0:LANES] + hh[:, LANES:2 * LANES] + _dot(xl, wsplit_ref[:, 0:LANES]) + br_ref[...]

    lane = lax.broadcasted_iota(I32, (bm, LANES), 1)
    lanef = lane.astype(F32)
    big = float(LANES)

    gl = jnp.where(lane < n_groups, lg, NEG_INF)
    gmax = jnp.max(gl, axis=-1, keepdims=True)
    gsel = jnp.min(jnp.where(gl == gmax, lanef, big), axis=-1, keepdims=True)
    p_group = 1.0 / jnp.sum(jnp.exp(gl - gmax), axis=-1, keepdims=True)

    lo = n_groups + gsel * epg
    el = jnp.where((lanef >= lo) & (lanef < lo + epg), lg, NEG_INF)
    m1 = jnp.max(el, axis=-1, keepdims=True)
    i1 = jnp.min(jnp.where(el == m1, lanef, big), axis=-1, keepdims=True)
    el2 = jnp.where(lanef == i1, NEG_INF, el)
    m2 = jnp.max(el2, axis=-1, keepdims=True)
    i2 = jnp.min(jnp.where(el2 == m2, lanef, big), axis=-1, keepdims=True)
    e21 = jnp.exp(m2 - m1)
    g0 = p_group / (1.0 + e21)
    g1 = g0 * e21

    oh0 = lanef == i1
    oh1 = lanef == i2
    cnt = jnp.where(oh0 | oh1, 1.0, 0.0)
    row = lax.broadcasted_iota(I32, (bm, bm), 0)
    col = lax.broadcasted_iota(I32, (bm, bm), 1)
    stril = jnp.where(row > col, 1.0, 0.0).astype(BF16)
    before = _dot(stril, cnt.astype(BF16)) + carry_ref[...]
    r0 = jnp.sum(jnp.where(oh0, before, 0.0), axis=-1, keepdims=True)
    r1 = jnp.sum(jnp.where(oh1, before, 0.0), axis=-1, keepdims=True)
    carry_ref[...] = carry_ref[...] + jnp.sum(cnt, axis=0, keepdims=True)
    cnt_ref[...] = carry_ref[...]

    meta = jnp.where(lane == 0, i1 - n_groups, 0.0)
    meta = jnp.where(lane == 1, i2 - n_groups, meta)
    meta = jnp.where(lane == 2, g0, meta)
    meta = jnp.where(lane == 3, g1, meta)
    meta = jnp.where(lane == 4, r0, meta)
    meta = jnp.where(lane == 5, r1, meta)
    meta_ref[...] = meta


def _ln_router(x2, y2, mod, row_of, ln_g, ln_b, w_router, b_router, alpha, n_groups, n_experts, seq):
    t, d = x2.shape
    bm = 256
    nbs = seq // bm

    def mrow(which):
        return lambda i: (row_of(i // nbs, which), 0, 0)

    return pl.pallas_call(
        functools.partial(_ln_router_kernel, alpha=alpha, n_groups=n_groups, n_experts=n_experts),
        grid=(t // bm,),
        in_specs=[
            pl.BlockSpec((bm, d), lambda i: (i, 0)),
            pl.BlockSpec((bm, d), lambda i: (i, 0)),
            pl.BlockSpec((None, 1, d), mrow(2)),
            pl.BlockSpec((None, 1, d), mrow(4)),
            pl.BlockSpec((None, 1, d), mrow(3)),
            pl.BlockSpec((1, d), lambda i: (0, 0)),
            pl.BlockSpec((1, d), lambda i: (0, 0)),
            pl.BlockSpec((d, LANES), lambda i: (0, 0)),
            pl.BlockSpec((1, LANES), lambda i: (0, 0)),
        ],
        out_specs=[
            pl.BlockSpec((bm, d), lambda i: (i, 0)),
            pl.BlockSpec((bm, d // 2), lambda i: (i, 0)),
            pl.BlockSpec((bm, LANES), lambda i: (i, 0)),
            pl.BlockSpec((1, LANES), lambda i: (0, 0)),
        ],
        out_shape=[
            jax.ShapeDtypeStruct((t, d), F32),
            jax.ShapeDtypeStruct((t, d // 2), jnp.uint32),
            jax.ShapeDtypeStruct((t, LANES), F32),
            jax.ShapeDtypeStruct((1, LANES), F32),
        ],
        scratch_shapes=[pltpu.VMEM((1, LANES), F32), pltpu.VMEM((d, 2 * LANES), BF16)],
        compiler_params=_cparams(1),
        name="ln_router",
    )(x2, y2, mod, mod, mod, ln_g, ln_b, w_router, b_router)


MOE_GATHER_SLOTS = 3


def _cast_rows(src_ref, dst_ref, rows=128):
    def body(r, carry):
        sl = pl.ds(pl.multiple_of(r * rows, rows), rows)
        dst_ref[sl, :] = src_ref[sl, :].astype(dst_ref.dtype)
        return carry
    lax.fori_loop(0, src_ref.shape[0] // rows, body, 0)


def _moe_kernel(blk_e_ref, nxt_e_ref, first_ref, grp_ref, nused_ref,
                tok0_ref, tok1_ref, tok2_ref, hf_hbm, wgu_hbm, wdn_hbm, y_ref,
                xbuf, wgu_st, wdn_st, wgu_bf, wdn_bf, gsem, wsem, *, layer, d_expert):
    i = pl.program_id(0)
    bm = xbuf.shape[1]
    nused = nused_ref[0]
    slot = i % MOE_GATHER_SLOTS

    def row_copy(tok, r, s):
        return pltpu.make_async_copy(hf_hbm.at[pl.ds(tok, 1), :], xbuf.at[s, pl.ds(r, 1), :], gsem.at[s])

    def issue_rows(tok_ref, s):
        for r in range(bm):
            row_copy(tok_ref[0, r], r, s).start()

    def wait_rows(s):
        pltpu.make_async_copy(hf_hbm.at[pl.ds(0, bm), :], xbuf.at[s], gsem.at[s]).wait()

    def wgu_copy(e):
        return pltpu.make_async_copy(wgu_hbm.at[layer, e], wgu_st, wsem.at[0])

    def wdn_copy(e, s):
        return pltpu.make_async_copy(wdn_hbm.at[layer, e], wdn_st.at[s], wsem.at[1 + s])

    @pl.when(i == 0)
    def _():
        issue_rows(tok0_ref, 0)
        wgu_copy(blk_e_ref[0]).start(priority=1)
        wdn_copy(blk_e_ref[0], 0).start(priority=1)

    @pl.when((i == 0) & (nused > 1))
    def _():
        issue_rows(tok1_ref, 1)

    @pl.when((i < nused) & (first_ref[i] == 1))
    def _():
        par = grp_ref[i] % 2
        has_next = nxt_e_ref[i] >= 0

        @pl.when(has_next)
        def _():
            wdn_copy(nxt_e_ref[i], 1 - par).start(priority=1)

        wgu_copy(blk_e_ref[i]).wait()
        _cast_rows(wgu_st, wgu_bf)

        @pl.when(has_next)
        def _():
            wgu_copy(nxt_e_ref[i]).start(priority=1)

        wdn_copy(blk_e_ref[i], par).wait()
        _cast_rows(wdn_st.at[par], wdn_bf)

    @pl.when(i + 2 < nused)
    def _():
        issue_rows(tok2_ref, (i + 2) % MOE_GATHER_SLOTS)

    @pl.when(i < nused)
    def _():
        wait_rows(slot)
        xb = _unpack_bf16_pairs(xbuf[slot]).astype(BF16)
        a = _dot(xb, wgu_bf[...])
        h = _silu(a[:, :d_expert]) * a[:, d_expert:]
        y_ref[...] = _pack_bf16_pairs(_dot(h.astype(BF16), wdn_bf[...]))

    @pl.when(i >= nused)
    def _():
        y_ref[...] = jnp.zeros_like(y_ref)


def _moe_experts(hf, tok_pad, blk_e, nxt_e, first, grp, nused, w_gate_up, w_down, layer, bm):
    d = w_down.shape[3]
    nb = blk_e.shape[0]
    d_expert = w_down.shape[2]
    tok3 = tok_pad.reshape(nb, 1, bm)
    grid_spec = pltpu.PrefetchScalarGridSpec(
        num_scalar_prefetch=5,
        grid=(nb,),
        in_specs=[
            pl.BlockSpec((None, 1, bm), lambda i, *_: (i, 0, 0), memory_space=pltpu.SMEM),
            pl.BlockSpec((None, 1, bm), lambda i, *_: (jnp.minimum(i + 1, nb - 1), 0, 0), memory_space=pltpu.SMEM),
            pl.BlockSpec((None, 1, bm), lambda i, *_: (jnp.minimum(i + 2, nb - 1), 0, 0), memory_space=pltpu.SMEM),
            pl.BlockSpec(memory_space=pl.ANY),
            pl.BlockSpec(memory_space=pl.ANY),
            pl.BlockSpec(memory_space=pl.ANY),
        ],
        out_specs=pl.BlockSpec((bm, d // 2), lambda i, *_: (i, 0)),
        scratch_shapes=[
            pltpu.VMEM((MOE_GATHER_SLOTS, bm, d // 2), jnp.uint32),
            pltpu.VMEM((d, 2 * d_expert), F32),
            pltpu.VMEM((2, d_expert, d), F32),
            pltpu.VMEM((d, 2 * d_expert), BF16),
            pltpu.VMEM((d_expert, d), BF16),
            pltpu.SemaphoreType.DMA((MOE_GATHER_SLOTS,)),
            pltpu.SemaphoreType.DMA((3,)),
        ],
    )
    return pl.pallas_call(
        functools.partial(_moe_kernel, layer=layer, d_expert=d_expert),
        grid_spec=grid_spec,
        out_shape=jax.ShapeDtypeStruct((nb * bm, d // 2), jnp.uint32),
        compiler_params=_cparams(1),
        name="moe_experts",
    )(blk_e, nxt_e, first, grp, nused, tok3, tok3, tok3, hf, w_gate_up, w_down)


def _moe_plan(meta, counts_row, n_groups, n_experts, bm):
    t = meta.shape[0]
    eid = meta[:, 0:2].astype(I32)
    rank = meta[:, 4:6].astype(I32)
    counts = counts_row[0, n_groups:n_groups + n_experts].astype(I32)
    padded = (counts + bm - 1) // bm * bm
    pad_end = jnp.cumsum(padded)
    pad_start = pad_end - padded
    start_of = jnp.sum(jnp.where(eid[..., None] == jnp.arange(n_experts, dtype=I32), pad_start, 0), axis=-1)
    dest = start_of + rank
    nb = (2 * t) // bm + n_experts
    nused = pad_end[-1] // bm
    ids = jnp.arange(nb, dtype=I32)
    raw_e = jnp.minimum(jnp.sum((pad_end[None, :] <= (ids * bm)[:, None]).astype(I32), axis=1), n_experts - 1)
    used = ids < nused
    blk_e = jnp.where(used, raw_e, raw_e[nused - 1])
    prev_e = jnp.concatenate([jnp.full((1,), -1, I32), blk_e[:-1]])
    first = (used & (blk_e != prev_e)).astype(I32)
    grp = jnp.cumsum(first) - 1
    key = jnp.where(used, blk_e, n_experts)
    nxt_idx = jnp.sum((key[None, :] <= blk_e[:, None]).astype(I32), axis=1)
    nxt_e = jnp.where(nxt_idx < nused, key[jnp.minimum(nxt_idx, nb - 1)], -1).astype(I32)
    tok = jnp.repeat(jnp.arange(t, dtype=I32), 2)
    tok_pad = jnp.zeros((nb * bm,), I32).at[dest.reshape(-1)].set(tok)
    return dest, tok_pad, blk_e, nxt_e, first, grp.astype(I32), nused.reshape(1).astype(I32)


def _ln_combine_kernel(*refs, alpha, with_next):
    if with_next:
        (dc_ref, dn_ref, x_ref, meta_ref, gate_ref, lng_ref, lnb_ref, sc_ref, sh_ref, y_hbm,
         xo_ref, hm_ref, ybuf, sem) = refs
    else:
        (dc_ref, dn_ref, x_ref, meta_ref, gate_ref, lng_ref, lnb_ref, y_hbm,
         xo_ref, ybuf, sem) = refs
    i = pl.program_id(0)
    n = pl.num_programs(0)
    bm = x_ref.shape[0]
    slot = i % 2

    def row_copy(src, r, s):
        return pltpu.make_async_copy(y_hbm.at[pl.ds(src, 1), :], ybuf.at[s, pl.ds(r, 1), :], sem.at[s])

    def issue_rows(d_ref, s):
        for r in range(2 * bm):
            row_copy(d_ref[0, r], r, s).start()

    def wait_rows(s):
        pltpu.make_async_copy(y_hbm.at[pl.ds(0, 2 * bm), :], ybuf.at[s], sem.at[s]).wait()

    @pl.when(i == 0)
    def _():
        issue_rows(dc_ref, 0)

    @pl.when(i + 1 < n)
    def _():
        issue_rows(dn_ref, 1 - slot)

    wait_rows(slot)
    meta = meta_ref[...]
    y = (_unpack_bf16_pairs(ybuf[slot, 0:bm, :]) * meta[:, 2:3]
         + _unpack_bf16_pairs(ybuf[slot, bm:2 * bm, :]) * meta[:, 3:4])
    z = alpha * x_ref[...] + (1.0 + gate_ref[...]) * y
    xn = _layer_norm_rows(z, lng_ref[...], lnb_ref[...])
    xo_ref[...] = xn
    if with_next:
        hm_ref[...] = (xn * (1.0 + sc_ref[...]) + sh_ref[...]).astype(hm_ref.dtype)


def _ln_combine(x2, meta, dest, ysort, mod, row_of, next_row_of, ln_g, ln_b, alpha, seq):
    t, d = x2.shape
    bm = 256
    nbs = seq // bm
    nblk = t // bm
    with_next = next_row_of is not None
    dest3 = dest.reshape(nblk, bm, 2).transpose(0, 2, 1).reshape(nblk, 1, 2 * bm)

    def mrow(fn, which):
        return lambda i: (fn(i // nbs, which), 0, 0)

    in_specs = [
        pl.BlockSpec((None, 1, 2 * bm), lambda i: (i, 0, 0), memory_space=pltpu.SMEM),
        pl.BlockSpec((None, 1, 2 * bm), lambda i: (jnp.minimum(i + 1, nblk - 1), 0, 0), memory_space=pltpu.SMEM),
        pl.BlockSpec((bm, d), lambda i: (i, 0)),
        pl.BlockSpec((bm, LANES), lambda i: (i, 0)),
        pl.BlockSpec((None, 1, d), mrow(row_of, 5)),
        pl.BlockSpec((1, d), lambda i: (0, 0)),
        pl.BlockSpec((1, d), lambda i: (0, 0)),
    ]
    args = [dest3, dest3, x2, meta, mod, ln_g, ln_b]
    out_specs = [pl.BlockSpec((bm, d), lambda i: (i, 0))]
    out_shape = [jax.ShapeDtypeStruct((t, d), F32)]
    if with_next:
        in_specs += [pl.BlockSpec((None, 1, d), mrow(next_row_of, 1)),
                     pl.BlockSpec((None, 1, d), mrow(next_row_of, 0))]
        args += [mod, mod]
        out_specs.append(pl.BlockSpec((bm, d), lambda i: (i, 0)))
        out_shape.append(jax.ShapeDtypeStruct((t, d), BF16))
    in_specs.append(pl.BlockSpec(memory_space=pl.ANY))
    args.append(ysort)
    return pl.pallas_call(
        functools.partial(_ln_combine_kernel, alpha=alpha, with_next=with_next),
        grid=(nblk,),
        in_specs=in_specs,
        out_specs=out_specs,
        out_shape=out_shape,
        scratch_shapes=[pltpu.VMEM((2, 2 * bm, d // 2), jnp.uint32), pltpu.SemaphoreType.DMA((2,))],
        compiler_params=_cparams(1),
        name="ln_moe_combine",
    )(*args)


def kernel(x, c, w_ada, b_ada, ln_g, ln_b, w_in_a, lb_logits, head_gain_a, w_out_a, w_in_b, attn_sinks, w_out_b, rel_bias, w_router_group, b_router_group, w_router_expert, b_router_expert, w_gate_up, w_down):
    bsz, seq, d = x.shape
    depth = w_ada.shape[0]
    n_groups = w_router_group.shape[2]
    n_experts = w_router_expert.shape[2]
    alpha = (2 * depth) ** 0.25
    t = bsz * seq
    moe_bm = 128

    mod = _ada_modulation(c, w_ada, b_ada)

    def row_of_layer(layer):
        return lambda b, which: (layer * bsz + b) * 6 + which

    x2 = x.reshape(t, d).astype(F32)
    row0 = row_of_layer(0)
    hm = _modulate(x2, mod, lambda b: row0(b, 1), lambda b: row0(b, 0), bsz, seq)

    for layer in range(depth):
        row_of = row_of_layer(layer)
        j = layer // 2
        if layer % 2 == 0:
            proj = _matmul(hm, w_in_a, j, BF16)
            o = _hgrn_mixer(proj, lb_logits, head_gain_a, layer, j, bsz, seq)
            y = _matmul(o, w_out_a, j, F32)
        else:
            proj = _matmul(hm, w_in_b, j, BF16)
            o = _attn_mixer(proj, attn_sinks, rel_bias, j, bsz, seq, d)
            y = _matmul(o, w_out_b, j, F32)

        n_pad = LANES - n_groups - n_experts
        w_router = jnp.concatenate(
            [w_router_group[layer].astype(F32), w_router_expert[layer].astype(F32), jnp.zeros((d, n_pad), F32)], axis=1)
        b_router = jnp.concatenate(
            [b_router_group[layer].astype(F32), b_router_expert[layer].astype(F32), jnp.zeros((n_pad,), F32)]
        ).reshape(1, LANES)

        x2, hf, meta, counts = _ln_router(
            x2, y, mod, row_of, ln_g[layer, 0:1].astype(F32), ln_b[layer, 0:1].astype(F32),
            w_router, b_router, alpha, n_groups, n_experts, seq)
        dest, tok_pad, blk_e, nxt_e, first, grp, nused = _moe_plan(meta, counts, n_groups, n_experts, moe_bm)
        ysort = _moe_experts(hf, tok_pad, blk_e, nxt_e, first, grp, nused, w_gate_up, w_down, layer, moe_bm)
        next_row_of = row_of_layer(layer + 1) if layer + 1 < depth else None
        outs = _ln_combine(x2, meta, dest, ysort, mod, row_of, next_row_of,
                           ln_g[layer, 1:2].astype(F32), ln_b[layer, 1:2].astype(F32), alpha, seq)
        x2 = outs[0]
        if next_row_of is not None:
            hm = outs[1]

    return x2.reshape(bsz, seq, d).astype(x.dtype)
```
